```python
import math
import jax, jax.numpy as jnp
from jax import lax
import numpy as np

D_MODEL = 1024
BATCH = 8
SEQ = 8192
DEPTH = 4
DEC_BATCH = 16
DEC_SEQ = 2048
PAST_LEN = 128

D_MIX = D_MODEL
SSD_WIDTH = D_MIX // 2
SSD_HEAD_DIM = 64
SSD_HEADS = SSD_WIDTH // SSD_HEAD_DIM
SSD_GROUPS = 2
SSD_STATE = 64
SSD_CONV = 5
SSD_CHUNK = 128
DT_MIN = 1e-3
DT_MAX = 1e-1
FNET_WIDTH = D_MIX // 4
FNET_GROUPS = 4
FNET_GDIM = FNET_WIDTH // FNET_GROUPS
GMLP_WIDTH = D_MIX - SSD_WIDTH - FNET_WIDTH
GMLP_HEADS = 4
GMLP_HDIM = GMLP_WIDTH // GMLP_HEADS
GMLP_CHUNK = 128
N_EXPERTS = 32
TOP_K = 4
D_EXPERT = D_MODEL
SWIGLU_LIMIT = 7.0
SWIGLU_ALPHA = 1.702
MOE_BLOCK = 128
LN_EPS = 1e-5
RMS_EPS = 1e-5
DN_ALPHA = (2 * DEPTH) ** 0.25
DN_BETA = (8 * DEPTH) ** -0.25

SSD_GN = SSD_GROUPS * SSD_STATE
SSD_XBC = SSD_WIDTH + 2 * SSD_GN
SSD_IN = SSD_WIDTH + SSD_XBC + 2 * SSD_HEADS
D_IN = SSD_IN + FNET_WIDTH + 2 * GMLP_WIDTH

kernel_name = "hybrid_ssd_fnet_gmlp_moe_encoder"


def layer_norm(x, g, b):
    xf = x.astype(jnp.float32)
    mu = jnp.mean(xf, -1, keepdims=True)
    var = jnp.mean(jnp.square(xf - mu), -1, keepdims=True)
    return ((xf - mu) * lax.rsqrt(var + LN_EPS)).astype(x.dtype) * g + b


def centred_dwconv(x, w, b):
    pad = w.shape[0] // 2
    y = lax.conv_general_dilated(x, w[:, None, :], window_strides=(1,), padding=[(pad, pad)],
                                 dimension_numbers=("NWC", "WIO", "NWC"),
                                 feature_group_count=x.shape[-1])
    return y + b


def segsum_exp(a_cum, strict):
    T = a_cum.shape[-1]
    diff = a_cum[..., :, None] - a_cum[..., None, :]
    mask = jnp.tril(jnp.ones((T, T), dtype=bool), -1 if strict else 0)
    return jnp.exp(jnp.where(mask, diff, -jnp.inf))


def ssd_scan(x, dt, a, bm, cm, strict):
    Bsz, L, H, P = x.shape
    G = bm.shape[2]
    R = H // G
    Q = SSD_CHUNK
    C = L // Q
    xdt = (x * dt[..., None]).reshape(Bsz, C, Q, G, R, P)
    adt = (dt * a).reshape(Bsz, C, Q, G, R).transpose(0, 3, 4, 1, 2)
    bm = bm.reshape(Bsz, C, Q, G, -1)
    cm = cm.reshape(Bsz, C, Q, G, -1)
    a_cum = jnp.cumsum(adt, axis=-1)
    decay = segsum_exp(a_cum, strict)
    cb = jnp.einsum("bclgn,bcsgn->bgcls", cm, bm)
    y_diag = jnp.einsum("bgrcls,bcsgrp->bclgrp", cb[:, :, None] * decay, xdt)
    decay_to_end = jnp.exp(a_cum[..., -1:] - a_cum).transpose(0, 3, 4, 1, 2)
    states = jnp.einsum("bclgn,bclgrp->bcgrpn", bm, xdt * decay_to_end[..., None])
    chunk_cum = jnp.cumsum(jnp.pad(a_cum[..., -1], ((0, 0), (0, 0), (0, 0), (1, 0))), axis=-1)
    decay_chunk = segsum_exp(chunk_cum, False)
    states_p = jnp.concatenate([jnp.zeros_like(states[:, :1]), states], axis=1)
    prev = jnp.einsum("bgrzc,bcgrpn->bzgrpn", decay_chunk, states_p)[:, :-1]
    decay_in = jnp.exp(a_cum).transpose(0, 3, 4, 1, 2)
    y_off = jnp.einsum("bclgn,bcgrpn->bclgrp", cm, prev) * decay_in[..., None]
    return (y_diag + y_off).reshape(Bsz, L, H, P)


def ssd_mixer(z, xbc, dt_raw, conv_w, conv_b, dt_bias, a_log, d_skip, norm_g):
    Bsz, L, _ = z.shape
    xbc = jax.nn.silu(centred_dwconv(xbc, conv_w, conv_b))
    xs = xbc[..., :SSD_WIDTH].reshape(Bsz, L, SSD_HEADS, SSD_HEAD_DIM)
    bm = xbc[..., SSD_WIDTH:SSD_WIDTH + SSD_GN].reshape(Bsz, L, SSD_GROUPS, SSD_STATE)
    cm = xbc[..., SSD_WIDTH + SSD_GN:].reshape(Bsz, L, SSD_GROUPS, SSD_STATE)
    dt = jax.nn.softplus(dt_raw.astype(jnp.float32).reshape(Bsz, L, 2, SSD_HEADS)
                         + dt_bias.astype(jnp.float32))
    a = -jnp.exp(a_log.astype(jnp.float32))
    y_f = ssd_scan(xs, dt[:, :, 0], a[0], bm, cm, strict=False)
    flip = lambda t: jnp.flip(t, axis=1)
    y_b = flip(ssd_scan(flip(xs), flip(dt[:, :, 1]), a[1], flip(bm), flip(cm), strict=True))
    y = (y_f + y_b + xs * d_skip[:, None]).reshape(Bsz, L, SSD_WIDTH) * jax.nn.silu(z)
    yg = y.reshape(Bsz, L, SSD_GROUPS, -1).astype(jnp.float32)
    yg = yg * lax.rsqrt(jnp.mean(jnp.square(yg), -1, keepdims=True) + RMS_EPS)
    return yg.reshape(Bsz, L, SSD_WIDTH).astype(z.dtype) * norm_g


def fourier_mixer(f, w_f, b_f):
    Bsz, L, _ = f.shape
    fg = f.astype(jnp.float32).reshape(Bsz, L, FNET_GROUPS, FNET_GDIM)
    mixed = jnp.fft.fft2(fg, axes=(1, 3), norm="ortho").real.astype(f.dtype)
    return jnp.einsum("blgc,gcd->blgd", mixed, w_f).reshape(Bsz, L, FNET_WIDTH) + b_f


def gmlp_mixer(uv, ln_g, ln_b, w_s, b_s):
    Bsz, L, _ = uv.shape
    uv = jax.nn.gelu(uv, approximate=False)
    u, v = uv[..., :GMLP_WIDTH], uv[..., GMLP_WIDTH:]
    v = layer_norm(v, ln_g, ln_b)
    vc = v.reshape(Bsz, L // GMLP_CHUNK, GMLP_CHUNK, GMLP_HEADS, GMLP_HDIM)
    sv = jnp.einsum("hts,bcshd->bcthd", w_s, vc) + b_s.T[:, :, None]
    return u * sv.reshape(Bsz, L, GMLP_WIDTH)


def moe(h, w_r, b_r, w_gu, b_gu, w_dn, b_dn):
    Bsz, L, D = h.shape
    T = Bsz * L
    A = T * TOP_K
    ht = h.reshape(T, D)
    logits = (ht @ w_r + b_r).astype(jnp.float32)
    top_val, top_idx = lax.top_k(logits, TOP_K)
    gates = jax.nn.softmax(top_val, axis=-1).astype(h.dtype)
    e_flat = top_idx.reshape(A)
    order = jnp.argsort(e_flat)
    e_sorted = e_flat[order]
    tok_sorted = (order // TOP_K).astype(jnp.int32)
    gates_sorted = gates.reshape(A)[order]
    counts = jnp.bincount(e_flat, length=N_EXPERTS)
    start = jnp.cumsum(counts) - counts
    padded = (counts + MOE_BLOCK - 1) // MOE_BLOCK * MOE_BLOCK
    pend = jnp.cumsum(padded)
    pstart = pend - padded
    pos = pstart[e_sorted] + jnp.arange(A, dtype=jnp.int32) - start[e_sorted]
    P = A + N_EXPERTS * MOE_BLOCK
    n_blocks = P // MOE_BLOCK
    tok_buf = jnp.full((P,), T, dtype=jnp.int32).at[pos].set(tok_sorted)
    x_buf = jnp.concatenate([ht, jnp.zeros((1, D), h.dtype)], axis=0)[tok_buf]
    blk_start = jnp.arange(n_blocks, dtype=jnp.int32) * MOE_BLOCK
    blk_e = jnp.minimum(jnp.searchsorted(pend, blk_start, side="right"), N_EXPERTS - 1)

    def run_block(args):
        xb, e = args
        gu = xb @ w_gu[e] + b_gu[e]
        g = jnp.minimum(gu[..., :D_EXPERT], SWIGLU_LIMIT)
        u = jnp.clip(gu[..., D_EXPERT:], -SWIGLU_LIMIT, SWIGLU_LIMIT)
        act = (u + 1.0) * (g * jax.nn.sigmoid(SWIGLU_ALPHA * g))
        return act @ w_dn[e] + b_dn[e]

    y_buf = lax.map(run_block, (x_buf.reshape(n_blocks, MOE_BLOCK, D), blk_e)).reshape(P, D)
    y = jax.ops.segment_sum(y_buf[pos] * gates_sorted[:, None], tok_sorted, num_segments=T)
    return y.reshape(Bsz, L, D)


def trunk(x, emb_ln_g, emb_ln_b, w_in, conv_w, conv_b, dt_bias, a_log, d_skip, ssd_norm_g,
          fnet_w, fnet_b, gmlp_ln_g, gmlp_ln_b, gmlp_ws, gmlp_bs, w_out, b_out, ln1_g, ln1_b,
          router_w, router_b, exp_w_gu, exp_b_gu, exp_w_dn, exp_b_dn, ln2_g, ln2_b):
    x = layer_norm(x, emb_ln_g, emb_ln_b)
    for l in range(DEPTH):
        proj = x @ w_in[l]
        z = proj[..., :SSD_WIDTH]
        xbc = proj[..., SSD_WIDTH:SSD_WIDTH + SSD_XBC]
        dt_raw = proj[..., SSD_WIDTH + SSD_XBC:SSD_IN]
        f = proj[..., SSD_IN:SSD_IN + FNET_WIDTH]
        uv = proj[..., SSD_IN + FNET_WIDTH:]
        mix = jnp.concatenate([
            ssd_mixer(z, xbc, dt_raw, conv_w[l], conv_b[l], dt_bias[l], a_log[l], d_skip[l], ssd_norm_g[l]),
            fourier_mixer(f, fnet_w[l], fnet_b[l]),
            gmlp_mixer(uv, gmlp_ln_g[l], gmlp_ln_b[l], gmlp_ws[l], gmlp_bs[l]),
        ], axis=-1)
        x = layer_norm(DN_ALPHA * x + mix @ w_out[l] + b_out[l], ln1_g[l], ln1_b[l])
        x = layer_norm(DN_ALPHA * x + moe(x, router_w[l], router_b[l], exp_w_gu[l], exp_b_gu[l],
                                          exp_w_dn[l], exp_b_dn[l]), ln2_g[l], ln2_b[l])
    return x


def setup_inputs(seed: int = 0) -> dict:
    key = jax.random.key(seed)
    ks = jax.random.split(key, 32)
    nrm = lambda k, shape, scale: jax.random.normal(k, shape, jnp.float32) * scale
    dt0 = jnp.exp(jax.random.uniform(ks[7], (DEPTH, 2, SSD_HEADS), jnp.float32,
                                     math.log(DT_MIN), math.log(DT_MAX)))
    return {
        "x_prompt": nrm(ks[0], (BATCH, SEQ, D_MODEL), 1.0),
        "x_sample": nrm(ks[1], (DEC_BATCH, DEC_SEQ, D_MODEL), 1.0),
        "emb_ln_g": 1.0 + nrm(ks[2], (D_MODEL,), 0.02),
        "emb_ln_b": nrm(ks[3], (D_MODEL,), 0.02),
        "w_in": nrm(ks[4], (DEPTH, D_MODEL, D_IN), D_MODEL ** -0.5),
        "conv_w": nrm(ks[5], (DEPTH, SSD_CONV, SSD_XBC), SSD_CONV ** -0.5),
        "conv_b": nrm(ks[6], (DEPTH, SSD_XBC), 0.02),
        "dt_bias": dt0 + jnp.log(-jnp.expm1(-dt0)),
        "a_log": jnp.log(jax.random.uniform(ks[8], (DEPTH, 2, SSD_HEADS), jnp.float32, 1.0, 16.0)),
        "d_skip": 1.0 + nrm(ks[9], (DEPTH, SSD_HEADS), 0.02),
        "ssd_norm_g": 1.0 + nrm(ks[10], (DEPTH, SSD_WIDTH), 0.02),
        "fnet_w": nrm(ks[11], (DEPTH, FNET_GROUPS, FNET_GDIM, FNET_GDIM), FNET_GDIM ** -0.5),
        "fnet_b": nrm(ks[12], (DEPTH, FNET_WIDTH), 0.02),
        "gmlp_ln_g": 1.0 + nrm(ks[13], (DEPTH, GMLP_WIDTH), 0.02),
        "gmlp_ln_b": nrm(ks[14], (DEPTH, GMLP_WIDTH), 0.02),
        "gmlp_ws": nrm(ks[15], (DEPTH, GMLP_HEADS, GMLP_CHUNK, GMLP_CHUNK), GMLP_CHUNK ** -0.5),
        "gmlp_bs": 1.0 + nrm(ks[16], (DEPTH, GMLP_HEADS, GMLP_CHUNK), 0.02),
        "w_out": nrm(ks[17], (DEPTH, D_MIX, D_MODEL), DN_BETA * D_MIX ** -0.5),
        "b_out": nrm(ks[18], (DEPTH, D_MODEL), 0.02),
        "ln1_g": 1.0 + nrm(ks[19], (DEPTH, D_MODEL), 0.02),
        "ln1_b": nrm(ks[20], (DEPTH, D_MODEL), 0.02),
        "router_w": nrm(ks[21], (DEPTH, D_MODEL, N_EXPERTS), D_MODEL ** -0.5),
        "router_b": nrm(ks[22], (DEPTH, N_EXPERTS), 0.01),
        "exp_w_gu": nrm(ks[23], (DEPTH, N_EXPERTS, D_MODEL, 2 * D_EXPERT), D_MODEL ** -0.5),
        "exp_b_gu": nrm(ks[24], (DEPTH, N_EXPERTS, 2 * D_EXPERT), 0.02),
        "exp_w_dn": nrm(ks[25], (DEPTH, N_EXPERTS, D_EXPERT, D_MODEL), DN_BETA * D_EXPERT ** -0.5),
        "exp_b_dn": nrm(ks[26], (DEPTH, N_EXPERTS, D_MODEL), 0.02),
        "ln2_g": 1.0 + nrm(ks[27], (DEPTH, D_MODEL), 0.02),
        "ln2_b": nrm(ks[28], (DEPTH, D_MODEL), 0.02),
    }


def reference(x_prompt, x_sample, emb_ln_g, emb_ln_b, w_in, conv_w, conv_b, dt_bias, a_log, d_skip,
              ssd_norm_g, fnet_w, fnet_b, gmlp_ln_g, gmlp_ln_b, gmlp_ws, gmlp_bs, w_out, b_out,
              ln1_g, ln1_b, router_w, router_b, exp_w_gu, exp_b_gu, exp_w_dn, exp_b_dn, ln2_g, ln2_b):
    y_prompt = trunk(x_prompt, emb_ln_g, emb_ln_b, w_in, conv_w, conv_b, dt_bias, a_log, d_skip,
                     ssd_norm_g, fnet_w, fnet_b, gmlp_ln_g, gmlp_ln_b, gmlp_ws, gmlp_bs, w_out, b_out,
                     ln1_g, ln1_b, router_w, router_b, exp_w_gu, exp_b_gu, exp_w_dn, exp_b_dn, ln2_g, ln2_b)
    y_sample = trunk(x_sample, emb_ln_g, emb_ln_b, w_in, conv_w, conv_b, dt_bias, a_log, d_skip,
                     ssd_norm_g, fnet_w, fnet_b, gmlp_ln_g, gmlp_ln_b, gmlp_ws, gmlp_bs, w_out, b_out,
                     ln1_g, ln1_b, router_w, router_b, exp_w_gu, exp_b_gu, exp_w_dn, exp_b_dn, ln2_g, ln2_b)
    return (y_prompt, y_sample)
```

```python
import functools
import math

import numpy as np
import jax
import jax.numpy as jnp
from jax import lax
from jax.experimental import pallas as pl
from jax.experimental.pallas import tpu as pltpu

F32 = jnp.float32
BF16 = jnp.bfloat16

D_MODEL = 1024
DEPTH = 4
SSD_WIDTH = 512
SSD_HEAD_DIM = 64
SSD_HEADS = 8
SSD_GROUPS = 2
SSD_STATE = 64
SSD_CONV = 5
SSD_GN = SSD_GROUPS * SSD_STATE
SSD_XBC = SSD_WIDTH + 2 * SSD_GN
SSD_IN = SSD_WIDTH + SSD_XBC + 2 * SSD_HEADS
FNET_WIDTH = 256
FNET_GROUPS = 4
FNET_GDIM = 64
GMLP_WIDTH = 256
GMLP_HEADS = 4
GMLP_HDIM = 64
N_EXPERTS = 32
TOP_K = 4
D_EXPERT = D_MODEL
SWIGLU_LIMIT = 7.0
SWIGLU_ALPHA = 1.702
LN_EPS = 1e-5
RMS_EPS = 1e-5
DN_ALPHA = (2 * DEPTH) ** 0.25

LANES = 128
CHUNK = 128
HALO = 16
VMEM_LIMIT = 56 * 1024 * 1024

TOK_TILE = 512
EXPERT_TILE = 512
GATHER_TILE = 512
COMBINE_TILE = 256
NEG_BIG = -1e30


def _dot(a, b):
    return jnp.dot(a, b, preferred_element_type=F32)


def _split2(v):
    hi = v.astype(BF16)
    lo = (v - hi.astype(F32)).astype(BF16)
    return hi, lo


def _split3(v):
    hi = v.astype(BF16)
    r = v - hi.astype(F32)
    mid = r.astype(BF16)
    lo = (r - mid.astype(F32)).astype(BF16)
    return hi, mid, lo


def _ln_rows(xf, g, b):
    mu = jnp.mean(xf, -1, keepdims=True)
    xc = xf - mu
    var = jnp.mean(xc * xc, -1, keepdims=True)
    return xc * lax.rsqrt(var + LN_EPS) * g + b


def _params(n_parallel=1, n_arbitrary=0):
    sem = ("parallel",) * n_parallel + ("arbitrary",) * n_arbitrary
    return pltpu.CompilerParams(dimension_semantics=sem, vmem_limit_bytes=VMEM_LIMIT)


def _full(shape):
    nd = len(shape)
    return pl.BlockSpec(shape, lambda *_: (0,) * nd)


def _embed_ln_kernel(x_ref, g_ref, b_ref, o_ref):
    o_ref[...] = _ln_rows(x_ref[...], g_ref[...], b_ref[...])


def _embed_ln(x, g, b):
    T, D = x.shape
    return pl.pallas_call(
        _embed_ln_kernel,
        grid=(T // TOK_TILE,),
        in_specs=[pl.BlockSpec((TOK_TILE, D), lambda i: (i, 0)), _full((1, D)), _full((1, D))],
        out_specs=pl.BlockSpec((TOK_TILE, D), lambda i: (i, 0)),
        out_shape=jax.ShapeDtypeStruct((T, D), F32),
        compiler_params=_params(1),
        name="embed_ln",
    )(x, g.reshape(1, D), b.reshape(1, D))


def _inproj_kernel(x_ref, wa_ref, wdt_ref, wf_ref, wuv_ref, mc_ref,
                   z_ref, xbc_ref, dt_ref, fr_ref, fi_ref, uv_ref):
    xb = x_ref[...].astype(BF16)
    a = _dot(xb, wa_ref[...])
    z_ref[...] = a[:, :SSD_WIDTH].astype(BF16)
    xbc_ref[...] = a[:, SSD_WIDTH:].astype(BF16)
    dt_ref[...] = _dot(xb, wdt_ref[...])
    f = _dot(xb, wf_ref[...]).astype(BF16)
    fri = _dot(f, mc_ref[...])
    fr_ref[...] = fri[:, :FNET_WIDTH].astype(BF16)
    fi_ref[...] = fri[:, FNET_WIDTH:].astype(BF16)
    uv_ref[...] = _dot(xb, wuv_ref[...]).astype(BF16)


def _inproj(x, wa, wdt, wf, wuv, mc):
    T, D = x.shape
    tm = TOK_TILE
    row = lambda w: pl.BlockSpec((tm, w), lambda i: (i, 0))
    return pl.pallas_call(
        _inproj_kernel,
        grid=(T // tm,),
        in_specs=[row(D), _full(wa.shape), _full(wdt.shape), _full(wf.shape), _full(wuv.shape),
                  _full(mc.shape)],
        out_specs=[row(SSD_WIDTH), row(SSD_XBC), row(LANES), row(FNET_WIDTH), row(FNET_WIDTH),
                   row(2 * GMLP_WIDTH)],
        out_shape=[jax.ShapeDtypeStruct((T, SSD_WIDTH), BF16),
                   jax.ShapeDtypeStruct((T, SSD_XBC), BF16),
                   jax.ShapeDtypeStruct((T, LANES), F32),
                   jax.ShapeDtypeStruct((T, FNET_WIDTH), BF16),
                   jax.ShapeDtypeStruct((T, FNET_WIDTH), BF16),
                   jax.ShapeDtypeStruct((T, 2 * GMLP_WIDTH), BF16)],
        compiler_params=_params(1),
        name="inproj",
    )(x, wa, wdt, wf, wuv, mc)


def _ssd_chunk(xc_ref, xp_ref, xn_ref, dt_ref, convw_ref, convb_ref, dtb_ref, a_ref,
               cum_ref, expand_ref, state_ref, *, backward, first, last):
    Q = CHUNK
    h_off = SSD_HEADS if backward else 0

    cur = xc_ref[...].astype(F32)
    prev = jnp.where(first, 0.0, xp_ref[...].astype(F32))
    nxt = jnp.where(last, 0.0, xn_ref[...].astype(F32))
    ext = jnp.concatenate([prev[HALO - 8:], cur, nxt[:8]], axis=0)
    n_ext = Q + 16
    pad = SSD_CONV // 2
    conv = convb_ref[...]
    for k in range(SSD_CONV):
        shift = (pad - k) % n_ext
        tap = ext if shift == 0 else pltpu.roll(ext, shift, axis=0)
        conv = conv + tap[8:8 + Q] * convw_ref[k:k + 1, :]
    act = conv * jax.nn.sigmoid(conv)
    xs = act[:, :SSD_WIDTH]
    bm = act[:, SSD_WIDTH:SSD_WIDTH + SSD_GN]
    cm = act[:, SSD_WIDTH + SSD_GN:]

    dt = jax.nn.softplus(dt_ref[...] + dtb_ref[...])
    adt = dt * a_ref[...]
    cum = cum_ref[...]
    h3 = _split3(adt)
    acum = _dot(cum, h3[0]) + _dot(cum, h3[1]) + _dot(cum, h3[2])
    acum_t = acum.T
    total = acum[0:1, :] if backward else acum[Q - 1:Q, :]
    dec_in = jnp.exp(acum)
    dec_end = jnp.exp(total - acum)

    expand = expand_ref[...]

    def widen(v):
        hi, lo = _split2(v)
        return _dot(hi, expand) + _dot(lo, expand)

    dt_w = widen(dt)
    dec_in_w = widen(dec_in)
    dec_end_w = widen(dec_end)
    tdec_w = dec_in_w[0:1, :] if backward else dec_in_w[Q - 1:Q, :]

    xdt = xs * dt_w
    xdt_b = xdt.astype(BF16)

    row = lax.broadcasted_iota(jnp.int32, (Q, Q), 0)
    col = lax.broadcasted_iota(jnp.int32, (Q, Q), 1)
    mask = (col > row) if backward else (col <= row)
    lane = lax.broadcasted_iota(jnp.int32, (Q, LANES), 1)
    lo_half = lane < SSD_STATE

    bm_b = bm.astype(BF16)
    cb = []
    for g in range(SSD_GROUPS):
        cm_g = jnp.where(lo_half if g == 0 else jnp.logical_not(lo_half), cm, 0.0).astype(BF16)
        cb.append(lax.dot_general(cm_g, bm_b, (((1,), (1,)), ((), ())),
                                  preferred_element_type=F32))

    y_parts = []
    for pair in range(SSD_HEADS // 2):
        gmats = []
        for r in range(2):
            h = 2 * pair + r
            hl = h_off + h
            diff = acum[:, hl:hl + 1] - acum_t[hl:hl + 1, :]
            dec = jnp.exp(jnp.where(mask, diff, NEG_BIG))
            gmats.append((cb[h // (SSD_HEADS // SSD_GROUPS)] * dec).astype(BF16))
        lhs = jnp.concatenate(gmats, axis=1)
        x2 = xdt[:, pair * LANES:(pair + 1) * LANES]
        rhs = jnp.concatenate([jnp.where(lo_half, x2, 0.0), jnp.where(lo_half, 0.0, x2)],
                              axis=0).astype(BF16)
        y_parts.append(_dot(lhs, rhs))
    y_diag = jnp.concatenate(y_parts, axis=1)

    srow = lax.broadcasted_iota(jnp.int32, (SSD_GN, SSD_WIDTH), 0)
    scol = lax.broadcasted_iota(jnp.int32, (SSD_GN, SSD_WIDTH), 1)
    on_group = (srow // SSD_STATE) == (scol // (SSD_WIDTH // SSD_GROUPS))
    st_new = _dot(bm.T.astype(BF16), (xdt * dec_end_w).astype(BF16))
    st_new = jnp.where(on_group, st_new, 0.0)
    st_prev = state_ref[...]
    y_off = _dot(cm.astype(BF16), st_prev.astype(BF16)) * dec_in_w
    state_ref[...] = st_prev * tdec_w + st_new
    del xdt_b
    return y_diag + y_off, xs


def _ssd_bwd_kernel(xc_ref, xp_ref, xn_ref, dt_ref, convw_ref, convb_ref, dtb_ref, a_ref,
                    cum_ref, expand_ref, yb_ref, state_ref, *, n_chunks):
    c = pl.program_id(1)
    chunk = n_chunks - 1 - c

    @pl.when(c == 0)
    def _():
        state_ref[...] = jnp.zeros_like(state_ref)

    y, _ = _ssd_chunk(xc_ref, xp_ref, xn_ref, dt_ref, convw_ref, convb_ref, dtb_ref, a_ref,
                      cum_ref, expand_ref, state_ref, backward=True,
                      first=chunk == 0, last=chunk == n_chunks - 1)
    yb_ref[...] = y


def _ssd_fwd_kernel(xc_ref, xp_ref, xn_ref, dt_ref, z_ref, yb_ref, convw_ref, convb_ref, dtb_ref,
                    a_ref, cum_ref, expand_ref, dskip_ref, normg_ref, o_ref, state_ref, *, n_chunks):
    c = pl.program_id(1)

    @pl.when(c == 0)
    def _():
        state_ref[...] = jnp.zeros_like(state_ref)

    y, xs = _ssd_chunk(xc_ref, xp_ref, xn_ref, dt_ref, convw_ref, convb_ref, dtb_ref, a_ref,
                       cum_ref, expand_ref, state_ref, backward=False,
                       first=c == 0, last=c == n_chunks - 1)
    z = z_ref[...].astype(F32)
    y = (y + yb_ref[...] + xs * dskip_ref[...]) * (z * jax.nn.sigmoid(z))
    gw = SSD_WIDTH // SSD_GROUPS
    outs = []
    for g in range(SSD_GROUPS):
        yg = y[:, g * gw:(g + 1) * gw]
        outs.append(yg * lax.rsqrt(jnp.mean(yg * yg, -1, keepdims=True) + RMS_EPS))
    o_ref[...] = (jnp.concatenate(outs, axis=1) * normg_ref[...]).astype(BF16)


def _ssd_mixer(z, xbc, dt, row0, B, L, convw, convb, dtb, a_neg, dskip_w, normg):
    C = L // CHUNK
    c0 = row0 // CHUNK
    hpc = CHUNK // HALO
    n_halo = z.shape[0] // HALO
    consts = _ssd_consts()

    def specs(chunk_of):
        cur = lambda w: pl.BlockSpec((CHUNK, w), lambda b, c: (c0 + b * C + chunk_of(c), 0))
        prv = pl.BlockSpec((HALO, SSD_XBC),
                           lambda b, c: (jnp.maximum((c0 + b * C + chunk_of(c)) * hpc - 1, 0), 0))
        nxt = pl.BlockSpec((HALO, SSD_XBC),
                           lambda b, c: (jnp.minimum((c0 + b * C + chunk_of(c) + 1) * hpc, n_halo - 1), 0))
        return cur, prv, nxt

    small = [_full(convw.shape), _full(convb.shape), _full(dtb.shape), _full(a_neg.shape)]
    out_row = lambda chunk_of: pl.BlockSpec((CHUNK, SSD_WIDTH), lambda b, c: (b * C + chunk_of(c), 0))

    rev = lambda c: C - 1 - c
    cur, prv, nxt = specs(rev)
    yb = pl.pallas_call(
        functools.partial(_ssd_bwd_kernel, n_chunks=C),
        grid=(B, C),
        in_specs=[cur(SSD_XBC), prv, nxt, cur(LANES)] + small
                 + [_full(consts["cum_b"].shape), _full(consts["expand_b"].shape)],
        out_specs=out_row(rev),
        out_shape=jax.ShapeDtypeStruct((B * L, SSD_WIDTH), F32),
        scratch_shapes=[pltpu.VMEM((SSD_GN, SSD_WIDTH), F32)],
        compiler_params=_params(1, 1),
        name="ssd_bwd",
    )(xbc, xbc, xbc, dt, convw, convb, dtb, a_neg, consts["cum_b"], consts["expand_b"])

    fwd = lambda c: c
    cur, prv, nxt = specs(fwd)
    return pl.pallas_call(
        functools.partial(_ssd_fwd_kernel, n_chunks=C),
        grid=(B, C),
        in_specs=[cur(SSD_XBC), prv, nxt, cur(LANES), cur(SSD_WIDTH), out_row(fwd)] + small
                 + [_full(consts["cum_f"].shape), _full(consts["expand_f"].shape),
                    _full(dskip_w.shape), _full(normg.shape)],
        out_specs=out_row(fwd),
        out_shape=jax.ShapeDtypeStruct((B * L, SSD_WIDTH), BF16),
        scratch_shapes=[pltpu.VMEM((SSD_GN, SSD_WIDTH), F32)],
        compiler_params=_params(1, 1),
        name="ssd_fwd",
    )(xbc, xbc, xbc, dt, z, yb, convw, convb, dtb, a_neg, consts["cum_f"], consts["expand_f"],
      dskip_w, normg)


@functools.lru_cache(maxsize=None)
def _ssd_consts_np():
    q = np.arange(CHUNK)
    cum_f = (q[None, :] <= q[:, None]).astype(np.float32)
    cum_b = (q[None, :] >= q[:, None]).astype(np.float32)
    out = {"cum_f": cum_f, "cum_b": cum_b}
    for name, off in (("expand_f", 0), ("expand_b", SSD_HEADS)):
        e = np.zeros((LANES, SSD_WIDTH), np.float32)
        for h in range(SSD_HEADS):
            e[off + h, h * SSD_HEAD_DIM:(h + 1) * SSD_HEAD_DIM] = 1.0
        out[name] = e
    return out


def _ssd_consts():
    return {k: jnp.asarray(v, BF16) for k, v in _ssd_consts_np().items()}


@functools.lru_cache(maxsize=None)
def _fnet_tables_np(L):
    L2 = CHUNK
    L1 = L // L2
    k1 = np.arange(L1)
    ang1 = 2.0 * np.pi * ((k1[:, None] * k1[None, :]) % L1) / L1
    f1 = np.concatenate([np.cos(ang1), np.sin(ang1)], axis=0) / math.sqrt(L1)
    k = k1[:, None, None] + L1 * np.arange(L2)[None, :, None]
    l2 = np.arange(L2)[None, None, :]
    ang2 = 2.0 * np.pi * ((k * l2) % L) / L
    gc = np.cos(ang2) / math.sqrt(L2)
    gs = np.sin(ang2) / math.sqrt(L2)
    return f1.astype(np.float32), gc.astype(np.float32), gs.astype(np.float32)


@functools.lru_cache(maxsize=None)
def _fnet_channel_np():
    c = np.arange(FNET_GDIM)
    ang = 2.0 * np.pi * ((c[:, None] * c[None, :]) % FNET_GDIM) / FNET_GDIM
    cc = np.cos(ang) / math.sqrt(FNET_GDIM)
    sc = np.sin(ang) / math.sqrt(FNET_GDIM)
    eye = np.eye(FNET_GROUPS)
    return np.concatenate([np.kron(eye, cc), np.kron(eye, -sc)], axis=1).astype(np.float32)


def _fnet_stage1_kernel(f_ref, xr_ref, xi_ref, yr_ref, yi_ref, *, L1):
    f = f_ref[...]
    pr = _dot(f, xr_ref[...])
    pi = _dot(f, xi_ref[...])
    yr_ref[...] = (pr[:L1] + pi[L1:]).astype(BF16)
    yi_ref[...] = (pi[:L1] - pr[L1:]).astype(BF16)


def _fnet_stage2_kernel(yr_ref, yi_ref, gc_ref, gs_ref, w_ref, b_ref, o_ref):
    zr = _dot(gc_ref[...], yr_ref[...]) + _dot(gs_ref[...], yi_ref[...])
    o_ref[...] = (_dot(zr.astype(BF16), w_ref[...]) + b_ref[...]).astype(BF16)


def _fourier_mixer(fr, fi, B, L, w_bd, b_f):
    L2 = CHUNK
    L1 = L // L2
    W = FNET_WIDTH
    f1, gc, gs = _fnet_tables_np(L)
    f1 = jnp.asarray(f1, BF16)
    gc = jnp.asarray(gc, BF16)
    gs = jnp.asarray(gs, BF16)
    n_cols = L2 * W
    tn = 4096
    xr = fr.reshape(B, L1, n_cols)
    xi = fi.reshape(B, L1, n_cols)
    blk = pl.BlockSpec((None, L1, tn), lambda b, j: (b, 0, j))
    yr, yi = pl.pallas_call(
        functools.partial(_fnet_stage1_kernel, L1=L1),
        grid=(B, n_cols // tn),
        in_specs=[_full(f1.shape), blk, blk],
        out_specs=[blk, blk],
        out_shape=[jax.ShapeDtypeStruct((B, L1, n_cols), BF16)] * 2,
        compiler_params=_params(2),
        name="fnet_stage1",
    )(f1, xr, xi)
    yr = yr.reshape(B, L1, L2, W)
    yi = yi.reshape(B, L1, L2, W)
    yblk = pl.BlockSpec((None, None, L2, W), lambda b, k: (b, k, 0, 0))
    gblk = pl.BlockSpec((None, L2, L2), lambda b, k: (k, 0, 0))
    out = pl.pallas_call(
        _fnet_stage2_kernel,
        grid=(B, L1),
        in_specs=[yblk, yblk, gblk, gblk, _full(w_bd.shape), _full(b_f.shape)],
        out_specs=pl.BlockSpec((None, L2, W), lambda b, k: (b, 0, k)),
        out_shape=jax.ShapeDtypeStruct((B, L2, L1 * W), BF16),
        compiler_params=_params(2),
        name="fnet_stage2",
    )(yr, yi, gc, gs, w_bd, b_f)
    return out.reshape(B * L, W)


def _gmlp_kernel(uv_ref, g_ref, b_ref, ws_ref, bs_ref, o_ref, *, n_chunks):
    uv = uv_ref[...].astype(F32)
    ge = 0.5 * uv * (1.0 + lax.erf(uv * (1.0 / math.sqrt(2.0))))
    u = ge[:, :GMLP_WIDTH]
    v = _ln_rows(ge[:, GMLP_WIDTH:], g_ref[...], b_ref[...])
    lane = lax.broadcasted_iota(jnp.int32, (CHUNK, GMLP_WIDTH), 1)
    ws = ws_ref[...]
    bs = bs_ref[...]
    for j in range(n_chunks):
        vj = v[j * CHUNK:(j + 1) * CHUNK]
        rhs = jnp.concatenate(
            [jnp.where(lane // GMLP_HDIM == h, vj, 0.0) for h in range(GMLP_HEADS)], axis=0)
        sv = _dot(ws, rhs.astype(BF16)) + bs
        o_ref[j * CHUNK:(j + 1) * CHUNK, :] = (u[j * CHUNK:(j + 1) * CHUNK] * sv).astype(BF16)


def _gmlp_mixer(uv, ln_g, ln_b, ws_cat, bs_wide):
    T = uv.shape[0]
    tm = TOK_TILE
    return pl.pallas_call(
        functools.partial(_gmlp_kernel, n_chunks=tm // CHUNK),
        grid=(T // tm,),
        in_specs=[pl.BlockSpec((tm, 2 * GMLP_WIDTH), lambda i: (i, 0)), _full(ln_g.shape),
                  _full(ln_b.shape), _full(ws_cat.shape), _full(bs_wide.shape)],
        out_specs=pl.BlockSpec((tm, GMLP_WIDTH), lambda i: (i, 0)),
        out_shape=jax.ShapeDtypeStruct((T, GMLP_WIDTH), BF16),
        compiler_params=_params(1),
        name="gmlp",
    )(uv, ln_g, ln_b, ws_cat, bs_wide)


def _outproj_kernel(x_ref, ssd_ref, fn_ref, gm_ref, wo_ref, bo_ref, g_ref, b_ref,
                    wrh_ref, wrl_ref, br_ref, x1_ref, idx_ref, gate_ref):
    acc = _dot(ssd_ref[...], wo_ref[0:SSD_WIDTH, :])
    acc = acc + _dot(fn_ref[...], wo_ref[SSD_WIDTH:SSD_WIDTH + FNET_WIDTH, :])
    acc = acc + _dot(gm_ref[...], wo_ref[SSD_WIDTH + FNET_WIDTH:, :])
    h = DN_ALPHA * x_ref[...] + acc + bo_ref[...]
    x1 = _ln_rows(h, g_ref[...], b_ref[...])
    x1_ref[...] = x1

    xh, xl = _split2(x1)
    wrh = wrh_ref[...]
    logits = _dot(xh, wrh) + _dot(xl, wrh) + _dot(xh, wrl_ref[...]) + br_ref[...]
    lane = lax.broadcasted_iota(jnp.int32, logits.shape, 1)
    work = logits
    vals, idxs = [], []
    for _ in range(TOP_K):
        m = jnp.max(work, axis=-1, keepdims=True)
        i = jnp.min(jnp.where(work == m, lane, LANES), axis=-1, keepdims=True)
        vals.append(m)
        idxs.append(i)
        work = jnp.where(lane == i, -jnp.inf, work)
    exps = [jnp.exp(v - vals[0]) for v in vals]
    denom = exps[0]
    for e in exps[1:]:
        denom = denom + e
    idx_out = jnp.zeros(logits.shape, jnp.int32)
    gate_out = jnp.zeros(logits.shape, F32)
    for k in range(TOP_K):
        idx_out = jnp.where(lane == k, idxs[k], idx_out)
        gate_out = jnp.where(lane == k, exps[k] / denom, gate_out)
    idx_ref[...] = idx_out
    gate_ref[...] = gate_out


def _outproj(x, ssd, fn, gm, wo, bo, g, b, wrh, wrl, br):
    T, D = x.shape
    tm = TOK_TILE
    row = lambda w: pl.BlockSpec((tm, w), lambda i: (i, 0))
    return pl.pallas_call(
        _outproj_kernel,
        grid=(T // tm,),
        in_specs=[row(D), row(SSD_WIDTH), row(FNET_WIDTH), row(GMLP_WIDTH), _full(wo.shape),
                  _full(bo.shape), _full(g.shape), _full(b.shape), _full(wrh.shape),
                  _full(wrl.shape), _full(br.shape)],
        out_specs=[row(D), row(LANES), row(LANES)],
        out_shape=[jax.ShapeDtypeStruct((T, D), F32),
                   jax.ShapeDtypeStruct((T, LANES), jnp.int32),
                   jax.ShapeDtypeStruct((T, LANES), F32)],
        compiler_params=_params(1),
        name="outproj_ln_router",
    )(x, ssd, fn, gm, wo, bo, g, b, wrh, wrl, br)


def _row_copy(src_hbm, dst, sem, src_row, dst_row):
    return pltpu.make_async_copy(src_hbm.at[pl.ds(src_row, 1), :], dst.at[pl.ds(dst_row, 1), :], sem)


def _gather_kernel(idx_ref, x_hbm, o_ref, sem, *, rows):
    def start(r, carry):
        _row_copy(x_hbm, o_ref, sem, idx_ref[r], r).start()
        return carry

    def wait(r, carry):
        _row_copy(x_hbm, o_ref, sem, idx_ref[r], r).wait()
        return carry

    lax.fori_loop(0, rows, start, 0)
    lax.fori_loop(0, rows, wait, 0)


def _dispatch_gather(x, tok_buf):
    P = tok_buf.shape[0]
    D = x.shape[1]
    R = GATHER_TILE
    return pl.pallas_call(
        functools.partial(_gather_kernel, rows=R),
        grid=(P // R,),
        in_specs=[pl.BlockSpec((R,), lambda i: (i,), memory_space=pltpu.SMEM),
                  pl.BlockSpec(memory_space=pl.ANY)],
        out_specs=pl.BlockSpec((R, D), lambda i: (i, 0)),
        out_shape=jax.ShapeDtypeStruct((P, D), F32),
        scratch_shapes=[pltpu.SemaphoreType.DMA(())],
        compiler_params=_params(1),
        name="moe_dispatch_gather",
    )(tok_buf, x)


def _expert_kernel(blk_e_ref, n_used_ref, x_ref, wgu_ref, bgu_ref, wdn_ref, bdn_ref, y_ref):
    i = pl.program_id(0)

    @pl.when(i < n_used_ref[0])
    def _():
        xb = x_ref[...].astype(BF16)
        gu = _dot(xb, wgu_ref[...]) + bgu_ref[...]
        g = jnp.minimum(gu[:, :D_EXPERT], SWIGLU_LIMIT)
        u = jnp.clip(gu[:, D_EXPERT:], -SWIGLU_LIMIT, SWIGLU_LIMIT)
        act = (u + 1.0) * (g * jax.nn.sigmoid(SWIGLU_ALPHA * g))
        y_ref[...] = _dot(act.astype(BF16), wdn_ref[...]) + bdn_ref[...]

    @pl.when(i >= n_used_ref[0])
    def _():
        y_ref[...] = jnp.zeros_like(y_ref)


def _expert_mlp(x_buf, blk_e, n_used, wgu, bgu, wdn, bdn):
    P, D = x_buf.shape
    tm = EXPERT_TILE
    grid_spec = pltpu.PrefetchScalarGridSpec(
        num_scalar_prefetch=2,
        grid=(P // tm,),
        in_specs=[pl.BlockSpec((tm, D), lambda i, be, nu: (i, 0)),
                  pl.BlockSpec((None, D, 2 * D_EXPERT), lambda i, be, nu: (be[i], 0, 0)),
                  pl.BlockSpec((None, 1, 2 * D_EXPERT), lambda i, be, nu: (be[i], 0, 0)),
                  pl.BlockSpec((None, D_EXPERT, D), lambda i, be, nu: (be[i], 0, 0)),
                  pl.BlockSpec((None, 1, D), lambda i, be, nu: (be[i], 0, 0))],
        out_specs=pl.BlockSpec((tm, D), lambda i, be, nu: (i, 0)),
    )
    return pl.pallas_call(
        _expert_kernel,
        grid_spec=grid_spec,
        out_shape=jax.ShapeDtypeStruct((P, D), F32),
        compiler_params=_params(1),
        name="moe_expert_mlp",
    )(blk_e, n_used, x_buf, wgu, bgu, wdn, bdn)


def _combine_kernel(pos_ref, x_ref, gate_ref, y_hbm, g_ref, b_ref, o_ref, buf, sem, *, rows):
    def each(r, carry, op):
        for k in range(TOP_K):
            cp = _row_copy(y_hbm, buf.at[k], sem, pos_ref[r * TOP_K + k], r)
            cp.start() if op == "start" else cp.wait()
        return carry

    lax.fori_loop(0, rows, functools.partial(each, op="start"), 0)
    lax.fori_loop(0, rows, functools.partial(each, op="wait"), 0)
    gates = gate_ref[...]
    y = gates[:, 0:1] * buf[0]
    for k in range(1, TOP_K):
        y = y + gates[:, k:k + 1] * buf[k]
    o_ref[...] = _ln_rows(DN_ALPHA * x_ref[...] + y, g_ref[...], b_ref[...])


def _combine(x, gates, pos_flat, y_buf, g, b):
    T, D = x.shape
    R = COMBINE_TILE
    return pl.pallas_call(
        functools.partial(_combine_kernel, rows=R),
        grid=(T // R,),
        in_specs=[pl.BlockSpec((R * TOP_K,), lambda i: (i,), memory_space=pltpu.SMEM),
                  pl.BlockSpec((R, D), lambda i: (i, 0)),
                  pl.BlockSpec((R, LANES), lambda i: (i, 0)),
                  pl.BlockSpec(memory_space=pl.ANY),
                  _full(g.shape), _full(b.shape)],
        out_specs=pl.BlockSpec((R, D), lambda i: (i, 0)),
        out_shape=jax.ShapeDtypeStruct((T, D), F32),
        scratch_shapes=[pltpu.VMEM((TOP_K, R, D), F32), pltpu.SemaphoreType.DMA(())],
        compiler_params=_params(1),
        name="moe_combine_ln",
    )(pos_flat, x, gates, y_buf, g, b)


def _routing_plan(idx, n_tokens):
    tm = EXPERT_TILE
    e = idx[:, :TOP_K]
    onehot = (e[:, :, None] == jnp.arange(N_EXPERTS, dtype=jnp.int32)).any(axis=1).astype(jnp.int32)
    csum = jnp.cumsum(onehot, axis=0)
    counts = csum[-1]
    rank = jnp.take_along_axis(csum - onehot, e, axis=1)
    padded = (counts + tm - 1) // tm * tm
    pend = jnp.cumsum(padded)
    pstart = pend - padded
    pos = (pstart[e] + rank).astype(jnp.int32)
    P = n_tokens * TOP_K + N_EXPERTS * tm
    tok = jnp.broadcast_to(jnp.arange(n_tokens, dtype=jnp.int32)[:, None], pos.shape)
    tok_buf = jnp.zeros((P,), jnp.int32).at[pos.reshape(-1)].set(tok.reshape(-1))
    blk_start = jnp.arange(P // tm, dtype=jnp.int32) * tm
    blk_e = jnp.minimum(jnp.searchsorted(pend, blk_start, side="right"), N_EXPERTS - 1).astype(jnp.int32)
    n_used = (pend[-1] // tm).astype(jnp.int32).reshape(1)
    return pos.reshape(-1), tok_buf, blk_e, n_used


def _trunk(xs, trunks, p):
    T, D = xs.shape
    x = _embed_ln(xs, p["emb_ln_g"], p["emb_ln_b"])
    mc = jnp.asarray(_fnet_channel_np(), BF16)
    for l in range(DEPTH):
        w_in = p["w_in"][l]
        wa = w_in[:, :SSD_WIDTH + SSD_XBC].astype(BF16)
        wdt = jnp.pad(w_in[:, SSD_WIDTH + SSD_XBC:SSD_IN], ((0, 0), (0, LANES - 2 * SSD_HEADS))).astype(BF16)
        wf = w_in[:, SSD_IN:SSD_IN + FNET_WIDTH].astype(BF16)
        wuv = w_in[:, SSD_IN + FNET_WIDTH:].astype(BF16)
        z, xbc, dt, fr, fi, uv = _inproj(x, wa, wdt, wf, wuv, mc)

        lane_pad = LANES - 2 * SSD_HEADS
        dtb = jnp.pad(p["dt_bias"][l].reshape(1, -1), ((0, 0), (0, lane_pad)))
        a_neg = jnp.pad(-jnp.exp(p["a_log"][l].astype(F32)).reshape(1, -1), ((0, 0), (0, lane_pad)))
        dskip_w = jnp.repeat(p["d_skip"][l], SSD_HEAD_DIM).reshape(1, SSD_WIDTH)
        normg = p["ssd_norm_g"][l].reshape(1, SSD_WIDTH)
        convw = p["conv_w"][l]
        convb = p["conv_b"][l].reshape(1, SSD_XBC)
        w_bd = jax.scipy.linalg.block_diag(*[p["fnet_w"][l, g] for g in range(FNET_GROUPS)]).astype(BF16)
        b_f = p["fnet_b"][l].reshape(1, FNET_WIDTH)

        ssd_parts, fn_parts = [], []
        row0 = 0
        for (B, L) in trunks:
            ssd_parts.append(_ssd_mixer(z, xbc, dt, row0, B, L, convw, convb, dtb, a_neg, dskip_w, normg))
            fn_parts.append(_fourier_mixer(fr[row0:row0 + B * L], fi[row0:row0 + B * L], B, L, w_bd, b_f))
            row0 += B * L
        ssd = jnp.concatenate(ssd_parts, axis=0)
        fn = jnp.concatenate(fn_parts, axis=0)

        ws_cat = jnp.concatenate([p["gmlp_ws"][l, h] for h in range(GMLP_HEADS)], axis=1).astype(BF16)
        bs_wide = jnp.repeat(p["gmlp_bs"][l].T, GMLP_HDIM, axis=1)
        gm = _gmlp_mixer(uv, p["gmlp_ln_g"][l].reshape(1, -1), p["gmlp_ln_b"][l].reshape(1, -1),
                         ws_cat, bs_wide)

        w_r = jnp.pad(p["router_w"][l], ((0, 0), (0, LANES - N_EXPERTS)))
        wrh = w_r.astype(BF16)
        wrl = (w_r - wrh.astype(F32)).astype(BF16)
        b_r = jnp.pad(p["router_b"][l].reshape(1, -1), ((0, 0), (0, LANES - N_EXPERTS)),
                      constant_values=NEG_BIG)
        x1, idx, gates = _outproj(x, ssd, fn, gm, p["w_out"][l].astype(BF16), p["b_out"][l].reshape(1, D),
                                  p["ln1_g"][l].reshape(1, D), p["ln1_b"][l].reshape(1, D), wrh, wrl, b_r)

        pos_flat, tok_buf, blk_e, n_used = _routing_plan(idx, T)
        x_buf = _dispatch_gather(x1, tok_buf)
        y_buf = _expert_mlp(x_buf, blk_e, n_used,
                            p["exp_w_gu"][l].astype(BF16), p["exp_b_gu"][l][:, None, :],
                            p["exp_w_dn"][l].astype(BF16), p["exp_b_dn"][l][:, None, :])
        x = _combine(x1, gates, pos_flat, y_buf, p["ln2_g"][l].reshape(1, D), p["ln2_b"][l].reshape(1, D))
    return x


def kernel(x_prompt, x_sample, emb_ln_g, emb_ln_b, w_in, conv_w, conv_b, dt_bias, a_log, d_skip, ssd_norm_g, fnet_w, fnet_b, gmlp_ln_g, gmlp_ln_b, gmlp_ws, gmlp_bs, w_out, b_out, ln1_g, ln1_b, router_w, router_b, exp_w_gu, exp_b_gu, exp_w_dn, exp_b_dn, ln2_g, ln2_b):
    p = dict(emb_ln_g=emb_ln_g, emb_ln_b=emb_ln_b, w_in=w_in, conv_w=conv_w, conv_b=conv_b,
             dt_bias=dt_bias, a_log=a_log, d_skip=d_skip, ssd_norm_g=ssd_norm_g, fnet_w=fnet_w,
             fnet_b=fnet_b, gmlp_ln_g=gmlp_ln_g, gmlp_ln_b=gmlp_ln_b, gmlp_ws=gmlp_ws, gmlp_bs=gmlp_bs,
             w_out=w_out, b_out=b_out, ln1_g=ln1_g, ln1_b=ln1_b, router_w=router_w, router_b=router_b,
             exp_w_gu=exp_w_gu, exp_b_gu=exp_b_gu, exp_w_dn=exp_w_dn, exp_b_dn=exp_b_dn,
             ln2_g=ln2_g, ln2_b=ln2_b)
    D = x_prompt.shape[-1]
    trunks = [x_prompt.shape[:2], x_sample.shape[:2]]
    xs = jnp.concatenate([x_prompt.reshape(-1, D), x_sample.reshape(-1, D)], axis=0)
    y = _trunk(xs, trunks, p)
    n_p = x_prompt.shape[0] * x_prompt.shape[1]
    return (y[:n_p].reshape(x_prompt.shape), y[n_p:].reshape(x_sample.shape))
```

```python
import functools
import math

import numpy as np
import jax
import jax.numpy as jnp
from jax import lax
from jax.experimental import pallas as pl
from jax.experimental.pallas import tpu as pltpu

F32 = jnp.float32
BF16 = jnp.bfloat16

D_MODEL = 1024
DEPTH = 4
SSD_WIDTH = 512
SSD_HEAD_DIM = 64
SSD_HEADS = 8
SSD_GROUPS = 2
SSD_STATE = 64
SSD_CONV = 5
SSD_GN = SSD_GROUPS * SSD_STATE
SSD_XBC = SSD_WIDTH + 2 * SSD_GN
SSD_IN = SSD_WIDTH + SSD_XBC + 2 * SSD_HEADS
FNET_WIDTH = 256
FNET_GROUPS = 4
FNET_GDIM = 64
GMLP_WIDTH = 256
GMLP_HEADS = 4
GMLP_HDIM = 64
N_EXPERTS = 32
TOP_K = 4
D_EXPERT = D_MODEL
SWIGLU_LIMIT = 7.0
SWIGLU_ALPHA = 1.702
LN_EPS = 1e-5
RMS_EPS = 1e-5
DN_ALPHA = (2 * DEPTH) ** 0.25

LANES = 128
CHUNK = 128
HALO = 16
VMEM_LIMIT = 56 * 1024 * 1024

TOK_TILE = 512
EXPERT_TILE = 512
EXPERT_COL_CHUNK = 512
DISPATCH_TILE = 512
COMBINE_TILE = 256
ROW_UNROLL = 8
NEG_BIG = -1e30


def _dot(a, b):
    return jnp.dot(a, b, preferred_element_type=F32)


def _split2(v):
    hi = v.astype(BF16)
    lo = (v - hi.astype(F32)).astype(BF16)
    return hi, lo


def _split3(v):
    hi = v.astype(BF16)
    r = v - hi.astype(F32)
    mid = r.astype(BF16)
    lo = (r - mid.astype(F32)).astype(BF16)
    return hi, mid, lo


def _ln_rows(xf, g, b):
    mu = jnp.mean(xf, -1, keepdims=True)
    xc = xf - mu
    var = jnp.mean(xc * xc, -1, keepdims=True)
    return xc * lax.rsqrt(var + LN_EPS) * g + b


def _params(n_parallel=1, n_arbitrary=0):
    sem = ("parallel",) * n_parallel + ("arbitrary",) * n_arbitrary
    return pltpu.CompilerParams(dimension_semantics=sem, vmem_limit_bytes=VMEM_LIMIT)


def _full(shape):
    nd = len(shape)
    return pl.BlockSpec(shape, lambda *_: (0,) * nd)


def _embed_ln_kernel(x_ref, g_ref, b_ref, o_ref):
    o_ref[...] = _ln_rows(x_ref[...], g_ref[...], b_ref[...])


def _embed_ln(x, g, b):
    T, D = x.shape
    return pl.pallas_call(
        _embed_ln_kernel,
        grid=(T // TOK_TILE,),
        in_specs=[pl.BlockSpec((TOK_TILE, D), lambda i: (i, 0)), _full((1, D)), _full((1, D))],
        out_specs=pl.BlockSpec((TOK_TILE, D), lambda i: (i, 0)),
        out_shape=jax.ShapeDtypeStruct((T, D), F32),
        compiler_params=_params(1),
        name="embed_ln",
    )(x, g.reshape(1, D), b.reshape(1, D))


def _inproj_kernel(x_ref, wa_ref, wdt_ref, wf_ref, wuv_ref, mc_ref,
                   z_ref, xbc_ref, dt_ref, fr_ref, fi_ref, uv_ref):
    xb = x_ref[...].astype(BF16)
    a = _dot(xb, wa_ref[...])
    z_ref[...] = a[:, :SSD_WIDTH].astype(BF16)
    xbc_ref[...] = a[:, SSD_WIDTH:].astype(BF16)
    dt_ref[...] = _dot(xb, wdt_ref[...])
    f = _dot(xb, wf_ref[...]).astype(BF16)
    fri = _dot(f, mc_ref[...])
    fr_ref[...] = fri[:, :FNET_WIDTH].astype(BF16)
    fi_ref[...] = fri[:, FNET_WIDTH:].astype(BF16)
    uv_ref[...] = _dot(xb, wuv_ref[...]).astype(BF16)


def _inproj(x, wa, wdt, wf, wuv, mc):
    T, D = x.shape
    tm = TOK_TILE
    row = lambda w: pl.BlockSpec((tm, w), lambda i: (i, 0))
    return pl.pallas_call(
        _inproj_kernel,
        grid=(T // tm,),
        in_specs=[row(D), _full(wa.shape), _full(wdt.shape), _full(wf.shape), _full(wuv.shape),
                  _full(mc.shape)],
        out_specs=[row(SSD_WIDTH), row(SSD_XBC), row(LANES), row(FNET_WIDTH), row(FNET_WIDTH),
                   row(2 * GMLP_WIDTH)],
        out_shape=[jax.ShapeDtypeStruct((T, SSD_WIDTH), BF16),
                   jax.ShapeDtypeStruct((T, SSD_XBC), BF16),
                   jax.ShapeDtypeStruct((T, LANES), F32),
                   jax.ShapeDtypeStruct((T, FNET_WIDTH), BF16),
                   jax.ShapeDtypeStruct((T, FNET_WIDTH), BF16),
                   jax.ShapeDtypeStruct((T, 2 * GMLP_WIDTH), BF16)],
        compiler_params=_params(1),
        name="inproj",
    )(x, wa, wdt, wf, wuv, mc)


def _ssd_chunk(xc_ref, xp_ref, xn_ref, dt_ref, convw_ref, convb_ref, dtb_ref, a_ref,
               cum_ref, expand_ref, state_ref, *, backward, first, last):
    Q = CHUNK
    h_off = SSD_HEADS if backward else 0

    cur = xc_ref[...].astype(F32)
    prev = jnp.where(first, 0.0, xp_ref[...].astype(F32))
    nxt = jnp.where(last, 0.0, xn_ref[...].astype(F32))
    ext = jnp.concatenate([prev[HALO - 8:], cur, nxt[:8]], axis=0)
    n_ext = Q + 16
    pad = SSD_CONV // 2
    conv = convb_ref[...]
    for k in range(SSD_CONV):
        shift = (pad - k) % n_ext
        tap = ext if shift == 0 else pltpu.roll(ext, shift, axis=0)
        conv = conv + tap[8:8 + Q] * convw_ref[k:k + 1, :]
    act = conv * jax.nn.sigmoid(conv)
    xs = act[:, :SSD_WIDTH]
    bm = act[:, SSD_WIDTH:SSD_WIDTH + SSD_GN]
    cm = act[:, SSD_WIDTH + SSD_GN:]

    dt = jax.nn.softplus(dt_ref[...] + dtb_ref[...])
    adt = dt * a_ref[...]
    cum = cum_ref[...]
    h3 = _split3(adt)
    acum = _dot(cum, h3[0]) + _dot(cum, h3[1]) + _dot(cum, h3[2])
    acum_t = acum.T
    total = acum[0:1, :] if backward else acum[Q - 1:Q, :]
    dec_in = jnp.exp(acum)
    dec_end = jnp.exp(total - acum)

    expand = expand_ref[...]

    def widen(v):
        hi, lo = _split2(v)
        return _dot(hi, expand) + _dot(lo, expand)

    dt_w = widen(dt)
    dec_in_w = widen(dec_in)
    dec_end_w = widen(dec_end)
    tdec_w = dec_in_w[0:1, :] if backward else dec_in_w[Q - 1:Q, :]

    xdt = xs * dt_w
    xdt_b = xdt.astype(BF16)

    row = lax.broadcasted_iota(jnp.int32, (Q, Q), 0)
    col = lax.broadcasted_iota(jnp.int32, (Q, Q), 1)
    mask = (col > row) if backward else (col <= row)
    lane = lax.broadcasted_iota(jnp.int32, (Q, LANES), 1)
    lo_half = lane < SSD_STATE

    bm_b = bm.astype(BF16)
    cb = []
    for g in range(SSD_GROUPS):
        cm_g = jnp.where(lo_half if g == 0 else jnp.logical_not(lo_half), cm, 0.0).astype(BF16)
        cb.append(lax.dot_general(cm_g, bm_b, (((1,), (1,)), ((), ())),
                                  preferred_element_type=F32))

    y_parts = []
    for pair in range(SSD_HEADS // 2):
        gmats = []
        for r in range(2):
            h = 2 * pair + r
            hl = h_off + h
            diff = acum[:, hl:hl + 1] - acum_t[hl:hl + 1, :]
            dec = jnp.exp(jnp.where(mask, diff, NEG_BIG))
            gmats.append((cb[h // (SSD_HEADS // SSD_GROUPS)] * dec).astype(BF16))
        lhs = jnp.concatenate(gmats, axis=1)
        x2 = xdt[:, pair * LANES:(pair + 1) * LANES]
        rhs = jnp.concatenate([jnp.where(lo_half, x2, 0.0), jnp.where(lo_half, 0.0, x2)],
                              axis=0).astype(BF16)
        y_parts.append(_dot(lhs, rhs))
    y_diag = jnp.concatenate(y_parts, axis=1)

    srow = lax.broadcasted_iota(jnp.int32, (SSD_GN, SSD_WIDTH), 0)
    scol = lax.broadcasted_iota(jnp.int32, (SSD_GN, SSD_WIDTH), 1)
    on_group = (srow // SSD_STATE) == (scol // (SSD_WIDTH // SSD_GROUPS))
    st_new = _dot(bm.T.astype(BF16), (xdt * dec_end_w).astype(BF16))
    st_new = jnp.where(on_group, st_new, 0.0)
    st_prev = state_ref[...]
    y_off = _dot(cm.astype(BF16), st_prev.astype(BF16)) * dec_in_w
    state_ref[...] = st_prev * tdec_w + st_new
    del xdt_b
    return y_diag + y_off, xs


def _ssd_bwd_kernel(xc_ref, xp_ref, xn_ref, dt_ref, convw_ref, convb_ref, dtb_ref, a_ref,
                    cum_ref, expand_ref, yb_ref, state_ref, *, n_chunks):
    c = pl.program_id(1)
    chunk = n_chunks - 1 - c

    @pl.when(c == 0)
    def _():
        state_ref[...] = jnp.zeros_like(state_ref)

    y, _ = _ssd_chunk(xc_ref, xp_ref, xn_ref, dt_ref, convw_ref, convb_ref, dtb_ref, a_ref,
                      cum_ref, expand_ref, state_ref, backward=True,
                      first=chunk == 0, last=chunk == n_chunks - 1)
    yb_ref[...] = y


def _ssd_fwd_kernel(xc_ref, xp_ref, xn_ref, dt_ref, z_ref, yb_ref, convw_ref, convb_ref, dtb_ref,
                    a_ref, cum_ref, expand_ref, dskip_ref, normg_ref, o_ref, state_ref, *, n_chunks):
    c = pl.program_id(1)

    @pl.when(c == 0)
    def _():
        state_ref[...] = jnp.zeros_like(state_ref)

    y, xs = _ssd_chunk(xc_ref, xp_ref, xn_ref, dt_ref, convw_ref, convb_ref, dtb_ref, a_ref,
                       cum_ref, expand_ref, state_ref, backward=False,
                       first=c == 0, last=c == n_chunks - 1)
    z = z_ref[...].astype(F32)
    y = (y + yb_ref[...] + xs * dskip_ref[...]) * (z * jax.nn.sigmoid(z))
    gw = SSD_WIDTH // SSD_GROUPS
    outs = []
    for g in range(SSD_GROUPS):
        yg = y[:, g * gw:(g + 1) * gw]
        outs.append(yg * lax.rsqrt(jnp.mean(yg * yg, -1, keepdims=True) + RMS_EPS))
    o_ref[...] = (jnp.concatenate(outs, axis=1) * normg_ref[...]).astype(BF16)


def _ssd_mixer(z, xbc, dt, row0, B, L, convw, convb, dtb, a_neg, dskip_w, normg):
    C = L // CHUNK
    c0 = row0 // CHUNK
    hpc = CHUNK // HALO
    n_halo = z.shape[0] // HALO
    consts = _ssd_consts()

    def specs(chunk_of):
        cur = lambda w: pl.BlockSpec((CHUNK, w), lambda b, c: (c0 + b * C + chunk_of(c), 0))
        prv = pl.BlockSpec((HALO, SSD_XBC),
                           lambda b, c: (jnp.maximum((c0 + b * C + chunk_of(c)) * hpc - 1, 0), 0))
        nxt = pl.BlockSpec((HALO, SSD_XBC),
                           lambda b, c: (jnp.minimum((c0 + b * C + chunk_of(c) + 1) * hpc, n_halo - 1), 0))
        return cur, prv, nxt

    small = [_full(convw.shape), _full(convb.shape), _full(dtb.shape), _full(a_neg.shape)]
    out_row = lambda chunk_of: pl.BlockSpec((CHUNK, SSD_WIDTH), lambda b, c: (b * C + chunk_of(c), 0))

    rev = lambda c: C - 1 - c
    cur, prv, nxt = specs(rev)
    yb = pl.pallas_call(
        functools.partial(_ssd_bwd_kernel, n_chunks=C),
        grid=(B, C),
        in_specs=[cur(SSD_XBC), prv, nxt, cur(LANES)] + small
                 + [_full(consts["cum_b"].shape), _full(consts["expand_b"].shape)],
        out_specs=out_row(rev),
        out_shape=jax.ShapeDtypeStruct((B * L, SSD_WIDTH), F32),
        scratch_shapes=[pltpu.VMEM((SSD_GN, SSD_WIDTH), F32)],
        compiler_params=_params(1, 1),
        name="ssd_bwd",
    )(xbc, xbc, xbc, dt, convw, convb, dtb, a_neg, consts["cum_b"], consts["expand_b"])

    fwd = lambda c: c
    cur, prv, nxt = specs(fwd)
    return pl.pallas_call(
        functools.partial(_ssd_fwd_kernel, n_chunks=C),
        grid=(B, C),
        in_specs=[cur(SSD_XBC), prv, nxt, cur(LANES), cur(SSD_WIDTH), out_row(fwd)] + small
                 + [_full(consts["cum_f"].shape), _full(consts["expand_f"].shape),
                    _full(dskip_w.shape), _full(normg.shape)],
        out_specs=out_row(fwd),
        out_shape=jax.ShapeDtypeStruct((B * L, SSD_WIDTH), BF16),
        scratch_shapes=[pltpu.VMEM((SSD_GN, SSD_WIDTH), F32)],
        compiler_params=_params(1, 1),
        name="ssd_fwd",
    )(xbc, xbc, xbc, dt, z, yb, convw, convb, dtb, a_neg, consts["cum_f"], consts["expand_f"],
      dskip_w, normg)


@functools.lru_cache(maxsize=None)
def _ssd_consts_np():
    q = np.arange(CHUNK)
    cum_f = (q[None, :] <= q[:, None]).astype(np.float32)
    cum_b = (q[None, :] >= q[:, None]).astype(np.float32)
    out = {"cum_f": cum_f, "cum_b": cum_b}
    for name, off in (("expand_f", 0), ("expand_b", SSD_HEADS)):
        e = np.zeros((LANES, SSD_WIDTH), np.float32)
        for h in range(SSD_HEADS):
            e[off + h, h * SSD_HEAD_DIM:(h + 1) * SSD_HEAD_DIM] = 1.0
        out[name] = e
    return out


def _ssd_consts():
    return {k: jnp.asarray(v, BF16) for k, v in _ssd_consts_np().items()}


@functools.lru_cache(maxsize=None)
def _fnet_tables_np(L):
    L2 = CHUNK
    L1 = L // L2
    k1 = np.arange(L1)
    ang1 = 2.0 * np.pi * ((k1[:, None] * k1[None, :]) % L1) / L1
    f1 = np.concatenate([np.cos(ang1), np.sin(ang1)], axis=0) / math.sqrt(L1)
    k = k1[:, None, None] + L1 * np.arange(L2)[None, :, None]
    l2 = np.arange(L2)[None, None, :]
    ang2 = 2.0 * np.pi * ((k * l2) % L) / L
    gc = np.cos(ang2) / math.sqrt(L2)
    gs = np.sin(ang2) / math.sqrt(L2)
    return f1.astype(np.float32), gc.astype(np.float32), gs.astype(np.float32)


@functools.lru_cache(maxsize=None)
def _fnet_channel_np():
    c = np.arange(FNET_GDIM)
    ang = 2.0 * np.pi * ((c[:, None] * c[None, :]) % FNET_GDIM) / FNET_GDIM
    cc = np.cos(ang) / math.sqrt(FNET_GDIM)
    sc = np.sin(ang) / math.sqrt(FNET_GDIM)
    eye = np.eye(FNET_GROUPS)
    return np.concatenate([np.kron(eye, cc), np.kron(eye, -sc)], axis=1).astype(np.float32)


def _fnet_stage1_kernel(f_ref, xr_ref, xi_ref, yr_ref, yi_ref, *, L1):
    f = f_ref[...]
    pr = _dot(f, xr_ref[...])
    pi = _dot(f, xi_ref[...])
    yr_ref[...] = (pr[:L1] + pi[L1:]).astype(BF16)
    yi_ref[...] = (pi[:L1] - pr[L1:]).astype(BF16)


def _fnet_stage2_kernel(yr_ref, yi_ref, gc_ref, gs_ref, w_ref, b_ref, o_ref):
    zr = _dot(gc_ref[...], yr_ref[...]) + _dot(gs_ref[...], yi_ref[...])
    o_ref[...] = (_dot(zr.astype(BF16), w_ref[...]) + b_ref[...]).astype(BF16)


def _fourier_mixer(fr, fi, B, L, w_bd, b_f):
    L2 = CHUNK
    L1 = L // L2
    W = FNET_WIDTH
    f1, gc, gs = _fnet_tables_np(L)
    f1 = jnp.asarray(f1, BF16)
    gc = jnp.asarray(gc, BF16)
    gs = jnp.asarray(gs, BF16)
    n_cols = L2 * W
    tn = 4096
    xr = fr.reshape(B, L1, n_cols)
    xi = fi.reshape(B, L1, n_cols)
    blk = pl.BlockSpec((None, L1, tn), lambda b, j: (b, 0, j))
    yr, yi = pl.pallas_call(
        functools.partial(_fnet_stage1_kernel, L1=L1),
        grid=(B, n_cols // tn),
        in_specs=[_full(f1.shape), blk, blk],
        out_specs=[blk, blk],
        out_shape=[jax.ShapeDtypeStruct((B, L1, n_cols), BF16)] * 2,
        compiler_params=_params(2),
        name="fnet_stage1",
    )(f1, xr, xi)
    yr = yr.reshape(B, L1, L2, W)
    yi = yi.reshape(B, L1, L2, W)
    yblk = pl.BlockSpec((None, None, L2, W), lambda b, k: (b, k, 0, 0))
    gblk = pl.BlockSpec((None, L2, L2), lambda b, k: (k, 0, 0))
    out = pl.pallas_call(
        _fnet_stage2_kernel,
        grid=(B, L1),
        in_specs=[yblk, yblk, gblk, gblk, _full(w_bd.shape), _full(b_f.shape)],
        out_specs=pl.BlockSpec((None, L2, W), lambda b, k: (b, 0, k)),
        out_shape=jax.ShapeDtypeStruct((B, L2, L1 * W), BF16),
        compiler_params=_params(2),
        name="fnet_stage2",
    )(yr, yi, gc, gs, w_bd, b_f)
    return out.reshape(B * L, W)


def _gmlp_kernel(uv_ref, g_ref, b_ref, ws_ref, bs_ref, o_ref, *, n_chunks):
    uv = uv_ref[...].astype(F32)
    ge = 0.5 * uv * (1.0 + lax.erf(uv * (1.0 / math.sqrt(2.0))))
    u = ge[:, :GMLP_WIDTH]
    v = _ln_rows(ge[:, GMLP_WIDTH:], g_ref[...], b_ref[...])
    lane = lax.broadcasted_iota(jnp.int32, (CHUNK, GMLP_WIDTH), 1)
    ws = ws_ref[...]
    bs = bs_ref[...]
    for j in range(n_chunks):
        vj = v[j * CHUNK:(j + 1) * CHUNK]
        rhs = jnp.concatenate(
            [jnp.where(lane // GMLP_HDIM == h, vj, 0.0) for h in range(GMLP_HEADS)], axis=0)
        sv = _dot(ws, rhs.astype(BF16)) + bs
        o_ref[j * CHUNK:(j + 1) * CHUNK, :] = (u[j * CHUNK:(j + 1) * CHUNK] * sv).astype(BF16)


def _gmlp_mixer(uv, ln_g, ln_b, ws_cat, bs_wide):
    T = uv.shape[0]
    tm = TOK_TILE
    return pl.pallas_call(
        functools.partial(_gmlp_kernel, n_chunks=tm // CHUNK),
        grid=(T // tm,),
        in_specs=[pl.BlockSpec((tm, 2 * GMLP_WIDTH), lambda i: (i, 0)), _full(ln_g.shape),
                  _full(ln_b.shape), _full(ws_cat.shape), _full(bs_wide.shape)],
        out_specs=pl.BlockSpec((tm, GMLP_WIDTH), lambda i: (i, 0)),
        out_shape=jax.ShapeDtypeStruct((T, GMLP_WIDTH), BF16),
        compiler_params=_params(1),
        name="gmlp",
    )(uv, ln_g, ln_b, ws_cat, bs_wide)


def _outproj_kernel(x_ref, ssd_ref, fn_ref, gm_ref, wo_ref, bo_ref, g_ref, b_ref,
                    wrh_ref, wrl_ref, br_ref, x1_ref, idx_ref, gate_ref):
    acc = _dot(ssd_ref[...], wo_ref[0:SSD_WIDTH, :])
    acc = acc + _dot(fn_ref[...], wo_ref[SSD_WIDTH:SSD_WIDTH + FNET_WIDTH, :])
    acc = acc + _dot(gm_ref[...], wo_ref[SSD_WIDTH + FNET_WIDTH:, :])
    h = DN_ALPHA * x_ref[...] + acc + bo_ref[...]
    x1 = _ln_rows(h, g_ref[...], b_ref[...])
    x1_ref[...] = x1

    xh, xl = _split2(x1)
    wrh = wrh_ref[...]
    logits = _dot(xh, wrh) + _dot(xl, wrh) + _dot(xh, wrl_ref[...]) + br_ref[...]
    lane = lax.broadcasted_iota(jnp.int32, logits.shape, 1)
    work = logits
    vals, idxs = [], []
    for _ in range(TOP_K):
        m = jnp.max(work, axis=-1, keepdims=True)
        i = jnp.min(jnp.where(work == m, lane, LANES), axis=-1, keepdims=True)
        vals.append(m)
        idxs.append(i)
        work = jnp.where(lane == i, -jnp.inf, work)
    exps = [jnp.exp(v - vals[0]) for v in vals]
    denom = exps[0]
    for e in exps[1:]:
        denom = denom + e
    idx_out = jnp.zeros(logits.shape, jnp.int32)
    gate_out = jnp.zeros(logits.shape, F32)
    for k in range(TOP_K):
        idx_out = jnp.where(lane == k, idxs[k], idx_out)
        gate_out = jnp.where(lane == k, exps[k] / denom, gate_out)
    idx_ref[...] = idx_out
    gate_ref[...] = gate_out


def _outproj(x, ssd, fn, gm, wo, bo, g, b, wrh, wrl, br):
    T, D = x.shape
    tm = TOK_TILE
    row = lambda w: pl.BlockSpec((tm, w), lambda i: (i, 0))
    return pl.pallas_call(
        _outproj_kernel,
        grid=(T // tm,),
        in_specs=[row(D), row(SSD_WIDTH), row(FNET_WIDTH), row(GMLP_WIDTH), _full(wo.shape),
                  _full(bo.shape), _full(g.shape), _full(b.shape), _full(wrh.shape),
                  _full(wrl.shape), _full(br.shape)],
        out_specs=[row(D), row(LANES), row(LANES)],
        out_shape=[jax.ShapeDtypeStruct((T, D), F32),
                   jax.ShapeDtypeStruct((T, LANES), jnp.int32),
                   jax.ShapeDtypeStruct((T, LANES), F32)],
        compiler_params=_params(1),
        name="outproj_ln_router",
    )(x, ssd, fn, gm, wo, bo, g, b, wrh, wrl, br)


def _row_copy(src, dst, sem, src_row, dst_row):
    return pltpu.make_async_copy(src.at[pl.ds(src_row, 1), :], dst.at[pl.ds(dst_row, 1), :], sem)


def _rows_wait(src, dst, sem, n_rows):
    pltpu.make_async_copy(src.at[pl.ds(0, n_rows), :], dst.at[pl.ds(0, n_rows), :], sem).wait()


def _dispatch_kernel(pos_ref, x_ref, xbuf_hbm, sem, *, rows):
    def body(j, carry):
        for u in range(ROW_UNROLL):
            r = j * ROW_UNROLL + u
            for k in range(TOP_K):
                _row_copy(x_ref, xbuf_hbm, sem, r, pos_ref[r * TOP_K + k]).start()
        return carry

    lax.fori_loop(0, rows // ROW_UNROLL, body, 0)
    for _ in range(TOP_K):
        _rows_wait(x_ref, xbuf_hbm, sem, rows)


def _dispatch(x, pos_flat, n_slots):
    T, D = x.shape
    R = DISPATCH_TILE
    return pl.pallas_call(
        functools.partial(_dispatch_kernel, rows=R),
        grid=(T // R,),
        in_specs=[pl.BlockSpec((R * TOP_K,), lambda i: (i,), memory_space=pltpu.SMEM),
                  pl.BlockSpec((R, D), lambda i: (i, 0))],
        out_specs=pl.BlockSpec(memory_space=pl.ANY),
        out_shape=jax.ShapeDtypeStruct((n_slots, D), F32),
        scratch_shapes=[pltpu.SemaphoreType.DMA(())],
        compiler_params=_params(1),
        name="moe_dispatch",
    )(pos_flat, x)


def _expert_kernel(blk_e_ref, valid_ref, x_ref, wgu_ref, bgu_ref, wdn_ref, bdn_ref, y_ref, wgu_b, wdn_b):
    i = pl.program_id(0)
    valid = valid_ref[i]
    prev_e = blk_e_ref[jnp.maximum(i - 1, 0)]
    new_expert = jnp.logical_or(i == 0, blk_e_ref[i] != prev_e)

    @pl.when(jnp.logical_and(valid > 0, new_expert))
    def _():
        wgu_b[...] = wgu_ref[...].astype(BF16)
        wdn_b[...] = wdn_ref[...].astype(BF16)

    @pl.when(valid > 0)
    def _():
        row = lax.broadcasted_iota(jnp.int32, x_ref.shape, 0)
        xb = jnp.where(row < valid, x_ref[...], 0.0).astype(BF16)
        acc = jnp.broadcast_to(bdn_ref[...], y_ref.shape)
        nc = EXPERT_COL_CHUNK
        for j in range(D_EXPERT // nc):
            g = _dot(xb, wgu_b[:, j * nc:(j + 1) * nc]) + bgu_ref[:, j * nc:(j + 1) * nc]
            u = (_dot(xb, wgu_b[:, D_EXPERT + j * nc:D_EXPERT + (j + 1) * nc])
                 + bgu_ref[:, D_EXPERT + j * nc:D_EXPERT + (j + 1) * nc])
            g = jnp.minimum(g, SWIGLU_LIMIT)
            u = jnp.clip(u, -SWIGLU_LIMIT, SWIGLU_LIMIT)
            act = (u + 1.0) * (g * jax.nn.sigmoid(SWIGLU_ALPHA * g))
            acc = acc + _dot(act.astype(BF16), wdn_b[j * nc:(j + 1) * nc, :])
        y_ref[...] = acc

    @pl.when(valid <= 0)
    def _():
        y_ref[...] = jnp.zeros_like(y_ref)


def _expert_mlp(x_buf, blk_e, blk_valid, wgu, bgu, wdn, bdn):
    P, D = x_buf.shape
    tm = EXPERT_TILE
    grid_spec = pltpu.PrefetchScalarGridSpec(
        num_scalar_prefetch=2,
        grid=(P // tm,),
        in_specs=[pl.BlockSpec((tm, D), lambda i, be, bv: (i, 0)),
                  pl.BlockSpec((None, D, 2 * D_EXPERT), lambda i, be, bv: (be[i], 0, 0)),
                  pl.BlockSpec((None, 1, 2 * D_EXPERT), lambda i, be, bv: (be[i], 0, 0)),
                  pl.BlockSpec((None, D_EXPERT, D), lambda i, be, bv: (be[i], 0, 0)),
                  pl.BlockSpec((None, 1, D), lambda i, be, bv: (be[i], 0, 0))],
        out_specs=pl.BlockSpec((tm, D), lambda i, be, bv: (i, 0)),
        scratch_shapes=[pltpu.VMEM((D, 2 * D_EXPERT), BF16), pltpu.VMEM((D_EXPERT, D), BF16)],
    )
    return pl.pallas_call(
        _expert_kernel,
        grid_spec=grid_spec,
        out_shape=jax.ShapeDtypeStruct((P, D), F32),
        compiler_params=_params(0, 1),
        name="moe_expert_mlp",
    )(blk_e, blk_valid, x_buf, wgu, bgu, wdn, bdn)


def _combine_kernel(pos_ref, pos_next_ref, x_ref, gate_ref, y_hbm, g_ref, b_ref, o_ref, buf, sem, *, rows):
    i = pl.program_id(0)
    n = pl.num_programs(0)
    slot = i % 2

    def issue(p_ref, s):
        def body(j, carry):
            for u in range(ROW_UNROLL):
                r = j * ROW_UNROLL + u
                for k in range(TOP_K):
                    _row_copy(y_hbm, buf.at[s, k], sem.at[s], p_ref[r * TOP_K + k], r).start()
            return carry

        lax.fori_loop(0, rows // ROW_UNROLL, body, 0)

    @pl.when(i == 0)
    def _():
        issue(pos_ref, 0)

    @pl.when(i + 1 < n)
    def _():
        issue(pos_next_ref, 1 - slot)

    for k in range(TOP_K):
        _rows_wait(y_hbm, buf.at[slot, k], sem.at[slot], rows)
    gates = gate_ref[...]
    y = gates[:, 0:1] * buf[slot, 0]
    for k in range(1, TOP_K):
        y = y + gates[:, k:k + 1] * buf[slot, k]
    o_ref[...] = _ln_rows(DN_ALPHA * x_ref[...] + y, g_ref[...], b_ref[...])


def _combine(x, gates, pos_flat, y_buf, g, b):
    T, D = x.shape
    R = COMBINE_TILE
    n = T // R
    return pl.pallas_call(
        functools.partial(_combine_kernel, rows=R),
        grid=(n,),
        in_specs=[pl.BlockSpec((R * TOP_K,), lambda i: (i,), memory_space=pltpu.SMEM),
                  pl.BlockSpec((R * TOP_K,), lambda i: (jnp.minimum(i + 1, n - 1),), memory_space=pltpu.SMEM),
                  pl.BlockSpec((R, D), lambda i: (i, 0)),
                  pl.BlockSpec((R, LANES), lambda i: (i, 0)),
                  pl.BlockSpec(memory_space=pl.ANY),
                  _full(g.shape), _full(b.shape)],
        out_specs=pl.BlockSpec((R, D), lambda i: (i, 0)),
        out_shape=jax.ShapeDtypeStruct((T, D), F32),
        scratch_shapes=[pltpu.VMEM((2, TOP_K, R, D), F32), pltpu.SemaphoreType.DMA((2,))],
        compiler_params=_params(0, 1),
        name="moe_combine_ln",
    )(pos_flat, pos_flat, x, gates, y_buf, g, b)


def _routing_plan(idx, n_tokens):
    tm = EXPERT_TILE
    e = idx[:, :TOP_K]
    onehot = (e[:, :, None] == jnp.arange(N_EXPERTS, dtype=jnp.int32)).any(axis=1).astype(jnp.int32)
    csum = jnp.cumsum(onehot, axis=0)
    counts = csum[-1]
    rank = jnp.take_along_axis(csum - onehot, e, axis=1)
    padded = (counts + tm - 1) // tm * tm
    pend = jnp.cumsum(padded)
    pstart = pend - padded
    pos = (pstart[e] + rank).astype(jnp.int32)
    n_slots = n_tokens * TOP_K + N_EXPERTS * tm
    blk_start = jnp.arange(n_slots // tm, dtype=jnp.int32) * tm
    blk_e = jnp.minimum(jnp.searchsorted(pend, blk_start, side="right"), N_EXPERTS - 1).astype(jnp.int32)
    blk_valid = jnp.clip(pstart[blk_e] + counts[blk_e] - blk_start, 0, tm).astype(jnp.int32)
    return pos.reshape(-1), blk_e, blk_valid, n_slots


def _moe(x1, idx, gates, wgu, bgu, wdn, bdn, ln_g, ln_b):
    T, D = x1.shape
    pos_flat, blk_e, blk_valid, n_slots = _routing_plan(idx, T)
    x_buf = _dispatch(x1, pos_flat, n_slots)
    y_buf = _expert_mlp(x_buf, blk_e, blk_valid, wgu, bgu[:, None, :], wdn, bdn[:, None, :])
    return _combine(x1, gates, pos_flat, y_buf, ln_g.reshape(1, D), ln_b.reshape(1, D))


def _trunk(xs, trunks, p):
    T, D = xs.shape
    x = _embed_ln(xs, p["emb_ln_g"], p["emb_ln_b"])
    mc = jnp.asarray(_fnet_channel_np(), BF16)
    for l in range(DEPTH):
        w_in = p["w_in"][l]
        wa = w_in[:, :SSD_WIDTH + SSD_XBC].astype(BF16)
        wdt = jnp.pad(w_in[:, SSD_WIDTH + SSD_XBC:SSD_IN], ((0, 0), (0, LANES - 2 * SSD_HEADS))).astype(BF16)
        wf = w_in[:, SSD_IN:SSD_IN + FNET_WIDTH].astype(BF16)
        wuv = w_in[:, SSD_IN + FNET_WIDTH:].astype(BF16)
        z, xbc, dt, fr, fi, uv = _inproj(x, wa, wdt, wf, wuv, mc)

        lane_pad = LANES - 2 * SSD_HEADS
        dtb = jnp.pad(p["dt_bias"][l].reshape(1, -1), ((0, 0), (0, lane_pad)))
        a_neg = jnp.pad(-jnp.exp(p["a_log"][l].astype(F32)).reshape(1, -1), ((0, 0), (0, lane_pad)))
        dskip_w = jnp.repeat(p["d_skip"][l], SSD_HEAD_DIM).reshape(1, SSD_WIDTH)
        normg = p["ssd_norm_g"][l].reshape(1, SSD_WIDTH)
        convw = p["conv_w"][l]
        convb = p["conv_b"][l].reshape(1, SSD_XBC)
        w_bd = jax.scipy.linalg.block_diag(*[p["fnet_w"][l, g] for g in range(FNET_GROUPS)]).astype(BF16)
        b_f = p["fnet_b"][l].reshape(1, FNET_WIDTH)

        ssd_parts, fn_parts = [], []
        row0 = 0
        for (B, L) in trunks:
            ssd_parts.append(_ssd_mixer(z, xbc, dt, row0, B, L, convw, convb, dtb, a_neg, dskip_w, normg))
            fn_parts.append(_fourier_mixer(fr[row0:row0 + B * L], fi[row0:row0 + B * L], B, L, w_bd, b_f))
            row0 += B * L
        ssd = jnp.concatenate(ssd_parts, axis=0)
        fn = jnp.concatenate(fn_parts, axis=0)

        ws_cat = jnp.concatenate([p["gmlp_ws"][l, h] for h in range(GMLP_HEADS)], axis=1).astype(BF16)
        bs_wide = jnp.repeat(p["gmlp_bs"][l].T, GMLP_HDIM, axis=1)
        gm = _gmlp_mixer(uv, p["gmlp_ln_g"][l].reshape(1, -1), p["gmlp_ln_b"][l].reshape(1, -1),
                         ws_cat, bs_wide)

        w_r = jnp.pad(p["router_w"][l], ((0, 0), (0, LANES - N_EXPERTS)))
        wrh = w_r.astype(BF16)
        wrl = (w_r - wrh.astype(F32)).astype(BF16)
        b_r = jnp.pad(p["router_b"][l].reshape(1, -1), ((0, 0), (0, LANES - N_EXPERTS)),
                      constant_values=NEG_BIG)
        x1, idx, gates = _outproj(x, ssd, fn, gm, p["w_out"][l].astype(BF16), p["b_out"][l].reshape(1, D),
                                  p["ln1_g"][l].reshape(1, D), p["ln1_b"][l].reshape(1, D), wrh, wrl, b_r)

        x = _moe(x1, idx, gates, p["exp_w_gu"][l], p["exp_b_gu"][l], p["exp_w_dn"][l], p["exp_b_dn"][l],
                 p["ln2_g"][l], p["ln2_b"][l])
    return x


def kernel(x_prompt, x_sample, emb_ln_g, emb_ln_b, w_in, conv_w, conv_b, dt_bias, a_log, d_skip, ssd_norm_g, fnet_w, fnet_b, gmlp_ln_g, gmlp_ln_b, gmlp_ws, gmlp_bs, w_out, b_out, ln1_g, ln1_b, router_w, router_b, exp_w_gu, exp_b_gu, exp_w_dn, exp_b_dn, ln2_g, ln2_b):
    p = dict(emb_ln_g=emb_ln_g, emb_ln_b=emb_ln_b, w_in=w_in, conv_w=conv_w, conv_b=conv_b,
             dt_bias=dt_bias, a_log=a_log, d_skip=d_skip, ssd_norm_g=ssd_norm_g, fnet_w=fnet_w,
             fnet_b=fnet_b, gmlp_ln_g=gmlp_ln_g, gmlp_ln_b=gmlp_ln_b, gmlp_ws=gmlp_ws, gmlp_bs=gmlp_bs,
             w_out=w_out, b_out=b_out, ln1_g=ln1_g, ln1_b=ln1_b, router_w=router_w, router_b=router_b,
             exp_w_gu=exp_w_gu, exp_b_gu=exp_b_gu, exp_w_dn=exp_w_dn, exp_b_dn=exp_b_dn,
             ln2_g=ln2_g, ln2_b=ln2_b)
    D = x_prompt.shape[-1]
    trunks = [x_prompt.shape[:2], x_sample.shape[:2]]
    xs = jnp.concatenate([x_prompt.reshape(-1, D), x_sample.reshape(-1, D)], axis=0)
    y = _trunk(xs, trunks, p)
    n_p = x_prompt.shape[0] * x_prompt.shape[1]
    return (y[:n_p].reshape(x_prompt.shape), y[n_p:].reshape(x_sample.shape))
```

```python
import functools
import math

import numpy as np
import jax
import jax.numpy as jnp
from jax import lax
from jax.experimental import pallas as pl
from jax.experimental.pallas import tpu as pltpu

F32 = jnp.float32
BF16 = jnp.bfloat16

D_MODEL = 1024
DEPTH = 4
SSD_WIDTH = 512
SSD_HEAD_DIM = 64
SSD_HEADS = 8
SSD_GROUPS = 2
SSD_STATE = 64
SSD_CONV = 5
SSD_GN = SSD_GROUPS * SSD_STATE
SSD_XBC = SSD_WIDTH + 2 * SSD_GN
SSD_IN = SSD_WIDTH + SSD_XBC + 2 * SSD_HEADS
FNET_WIDTH = 256
FNET_GROUPS = 4
FNET_GDIM = 64
GMLP_WIDTH = 256
GMLP_HEADS = 4
GMLP_HDIM = 64
N_EXPERTS = 32
TOP_K = 4
D_EXPERT = D_MODEL
SWIGLU_LIMIT = 7.0
SWIGLU_ALPHA = 1.702
LN_EPS = 1e-5
RMS_EPS = 1e-5
DN_ALPHA = (2 * DEPTH) ** 0.25

LANES = 128
CHUNK = 128
HALO = 16
VMEM_LIMIT = 56 * 1024 * 1024

TOK_TILE = 512
EXPERT_TILE = 512
EXPERT_COL_CHUNK = 512
DISPATCH_TILE = 512
COMBINE_TILE = 256
ROW_UNROLL = 8
NEG_BIG = -1e30


def _dot(a, b):
    return jnp.dot(a, b, preferred_element_type=F32)


def _split2(v):
    hi = v.astype(BF16)
    lo = (v - hi.astype(F32)).astype(BF16)
    return hi, lo


def _split3(v):
    hi = v.astype(BF16)
    r = v - hi.astype(F32)
    mid = r.astype(BF16)
    lo = (r - mid.astype(F32)).astype(BF16)
    return hi, mid, lo


def _ln_rows(xf, g, b):
    mu = jnp.mean(xf, -1, keepdims=True)
    xc = xf - mu
    var = jnp.mean(xc * xc, -1, keepdims=True)
    return xc * lax.rsqrt(var + LN_EPS) * g + b


def _params(n_parallel=1, n_arbitrary=0):
    sem = ("parallel",) * n_parallel + ("arbitrary",) * n_arbitrary
    return pltpu.CompilerParams(dimension_semantics=sem, vmem_limit_bytes=VMEM_LIMIT)


def _full(shape):
    nd = len(shape)
    return pl.BlockSpec(shape, lambda *_: (0,) * nd)


def _embed_ln_kernel(x_ref, g_ref, b_ref, o_ref):
    o_ref[...] = _ln_rows(x_ref[...], g_ref[...], b_ref[...])


def _embed_ln(x, g, b):
    T, D = x.shape
    return pl.pallas_call(
        _embed_ln_kernel,
        grid=(T // TOK_TILE,),
        in_specs=[pl.BlockSpec((TOK_TILE, D), lambda i: (i, 0)), _full((1, D)), _full((1, D))],
        out_specs=pl.BlockSpec((TOK_TILE, D), lambda i: (i, 0)),
        out_shape=jax.ShapeDtypeStruct((T, D), F32),
        compiler_params=_params(1),
        name="embed_ln",
    )(x, g.reshape(1, D), b.reshape(1, D))


def _inproj_kernel(x_ref, wa_ref, wdt_ref, wf_ref, wuv_ref, mc_ref,
                   z_ref, xbc_ref, dt_ref, fr_ref, fi_ref, uv_ref):
    xb = x_ref[...].astype(BF16)
    a = _dot(xb, wa_ref[...])
    z_ref[...] = a[:, :SSD_WIDTH].astype(BF16)
    xbc_ref[...] = a[:, SSD_WIDTH:].astype(BF16)
    dt_ref[...] = _dot(xb, wdt_ref[...])
    f = _dot(xb, wf_ref[...]).astype(BF16)
    fri = _dot(f, mc_ref[...])
    fr_ref[...] = fri[:, :FNET_WIDTH].astype(BF16)
    fi_ref[...] = fri[:, FNET_WIDTH:].astype(BF16)
    uv_ref[...] = _dot(xb, wuv_ref[...]).astype(BF16)


def _inproj(x, wa, wdt, wf, wuv, mc):
    T, D = x.shape
    tm = TOK_TILE
    row = lambda w: pl.BlockSpec((tm, w), lambda i: (i, 0))
    return pl.pallas_call(
        _inproj_kernel,
        grid=(T // tm,),
        in_specs=[row(D), _full(wa.shape), _full(wdt.shape), _full(wf.shape), _full(wuv.shape),
                  _full(mc.shape)],
        out_specs=[row(SSD_WIDTH), row(SSD_XBC), row(LANES), row(FNET_WIDTH), row(FNET_WIDTH),
                   row(2 * GMLP_WIDTH)],
        out_shape=[jax.ShapeDtypeStruct((T, SSD_WIDTH), BF16),
                   jax.ShapeDtypeStruct((T, SSD_XBC), BF16),
                   jax.ShapeDtypeStruct((T, LANES), F32),
                   jax.ShapeDtypeStruct((T, FNET_WIDTH), BF16),
                   jax.ShapeDtypeStruct((T, FNET_WIDTH), BF16),
                   jax.ShapeDtypeStruct((T, 2 * GMLP_WIDTH), BF16)],
        compiler_params=_params(1),
        name="inproj",
    )(x, wa, wdt, wf, wuv, mc)


def _conv_silu(xc_ref, xp_ref, xn_ref, convw_ref, convb_ref, first, last):
    Q = CHUNK
    cur = xc_ref[...].astype(F32)
    prev = jnp.where(first, 0.0, xp_ref[...].astype(F32))
    nxt = jnp.where(last, 0.0, xn_ref[...].astype(F32))
    ext = jnp.concatenate([prev[HALO - 8:], cur, nxt[:8]], axis=0)
    n_ext = Q + 16
    pad = SSD_CONV // 2
    conv = convb_ref[...]
    for k in range(SSD_CONV):
        shift = (pad - k) % n_ext
        tap = ext if shift == 0 else pltpu.roll(ext, shift, axis=0)
        conv = conv + tap[8:8 + Q] * convw_ref[k:k + 1, :]
    return conv * jax.nn.sigmoid(conv)


def _ssd_chunk(act, dt_ref, dtb_ref, a_ref, cum_ref, expand_ref, state_ref, *, backward):
    Q = CHUNK
    h_off = SSD_HEADS if backward else 0
    xs = act[:, :SSD_WIDTH]
    bm = act[:, SSD_WIDTH:SSD_WIDTH + SSD_GN]
    cm = act[:, SSD_WIDTH + SSD_GN:]

    dt = jax.nn.softplus(dt_ref[...] + dtb_ref[...])
    adt = dt * a_ref[...]
    cum = cum_ref[...]
    h3 = _split3(adt)
    acum = _dot(cum, h3[0]) + _dot(cum, h3[1]) + _dot(cum, h3[2])
    acum_t = acum.T
    total = acum[0:1, :] if backward else acum[Q - 1:Q, :]
    dec_in = jnp.exp(acum)
    dec_end = jnp.exp(total - acum)

    expand = expand_ref[...]

    def widen(v):
        hi, lo = _split2(v)
        return _dot(hi, expand) + _dot(lo, expand)

    dt_w = widen(dt)
    dec_in_w = widen(dec_in)
    dec_end_w = widen(dec_end)
    tdec_w = dec_in_w[0:1, :] if backward else dec_in_w[Q - 1:Q, :]

    xdt = xs * dt_w
    xdt_b = xdt.astype(BF16)

    row = lax.broadcasted_iota(jnp.int32, (Q, Q), 0)
    col = lax.broadcasted_iota(jnp.int32, (Q, Q), 1)
    mask = (col > row) if backward else (col <= row)
    lane = lax.broadcasted_iota(jnp.int32, (Q, LANES), 1)
    lo_half = lane < SSD_STATE

    bm_b = bm.astype(BF16)
    cb = []
    for g in range(SSD_GROUPS):
        cm_g = jnp.where(lo_half if g == 0 else jnp.logical_not(lo_half), cm, 0.0).astype(BF16)
        cb.append(lax.dot_general(cm_g, bm_b, (((1,), (1,)), ((), ())),
                                  preferred_element_type=F32))

    y_parts = []
    for pair in range(SSD_HEADS // 2):
        gmats = []
        for r in range(2):
            h = 2 * pair + r
            hl = h_off + h
            diff = acum[:, hl:hl + 1] - acum_t[hl:hl + 1, :]
            dec = jnp.exp(jnp.where(mask, diff, NEG_BIG))
            gmats.append((cb[h // (SSD_HEADS // SSD_GROUPS)] * dec).astype(BF16))
        lhs = jnp.concatenate(gmats, axis=1)
        x2 = xdt[:, pair * LANES:(pair + 1) * LANES]
        rhs = jnp.concatenate([jnp.where(lo_half, x2, 0.0), jnp.where(lo_half, 0.0, x2)],
                              axis=0).astype(BF16)
        y_parts.append(_dot(lhs, rhs))
    y_diag = jnp.concatenate(y_parts, axis=1)

    srow = lax.broadcasted_iota(jnp.int32, (SSD_GN, SSD_WIDTH), 0)
    scol = lax.broadcasted_iota(jnp.int32, (SSD_GN, SSD_WIDTH), 1)
    on_group = (srow // SSD_STATE) == (scol // (SSD_WIDTH // SSD_GROUPS))
    st_new = _dot(bm.T.astype(BF16), (xdt * dec_end_w).astype(BF16))
    st_new = jnp.where(on_group, st_new, 0.0)
    st_prev = state_ref[...]
    y_off = _dot(cm.astype(BF16), st_prev.astype(BF16)) * dec_in_w
    state_ref[...] = st_prev * tdec_w + st_new
    del xdt_b
    return y_diag + y_off, xs


def _ssd_bwd_kernel(xc_ref, xp_ref, xn_ref, dt_ref, convw_ref, convb_ref, dtb_ref, a_ref,
                    cum_ref, expand_ref, yb_ref, act_ref, state_ref, *, n_chunks):
    c = pl.program_id(1)
    chunk = n_chunks - 1 - c

    @pl.when(c == 0)
    def _():
        state_ref[...] = jnp.zeros_like(state_ref)

    act = _conv_silu(xc_ref, xp_ref, xn_ref, convw_ref, convb_ref,
                     first=chunk == 0, last=chunk == n_chunks - 1)
    act_ref[...] = act.astype(BF16)
    y, _ = _ssd_chunk(act, dt_ref, dtb_ref, a_ref, cum_ref, expand_ref, state_ref, backward=True)
    yb_ref[...] = y


def _ssd_fwd_kernel(act_ref, dt_ref, z_ref, yb_ref, dtb_ref, a_ref, cum_ref, expand_ref,
                    dskip_ref, normg_ref, o_ref, state_ref):
    c = pl.program_id(1)

    @pl.when(c == 0)
    def _():
        state_ref[...] = jnp.zeros_like(state_ref)

    y, xs = _ssd_chunk(act_ref[...].astype(F32), dt_ref, dtb_ref, a_ref, cum_ref, expand_ref,
                       state_ref, backward=False)
    z = z_ref[...].astype(F32)
    y = (y + yb_ref[...] + xs * dskip_ref[...]) * (z * jax.nn.sigmoid(z))
    gw = SSD_WIDTH // SSD_GROUPS
    outs = []
    for g in range(SSD_GROUPS):
        yg = y[:, g * gw:(g + 1) * gw]
        outs.append(yg * lax.rsqrt(jnp.mean(yg * yg, -1, keepdims=True) + RMS_EPS))
    o_ref[...] = (jnp.concatenate(outs, axis=1) * normg_ref[...]).astype(BF16)


def _ssd_mixer(z, xbc, dt, row0, B, L, convw, convb, dtb, a_neg, dskip_w, normg):
    C = L // CHUNK
    c0 = row0 // CHUNK
    hpc = CHUNK // HALO
    n_halo = z.shape[0] // HALO
    consts = _ssd_consts()

    def specs(chunk_of):
        cur = lambda w: pl.BlockSpec((CHUNK, w), lambda b, c: (c0 + b * C + chunk_of(c), 0))
        prv = pl.BlockSpec((HALO, SSD_XBC),
                           lambda b, c: (jnp.maximum((c0 + b * C + chunk_of(c)) * hpc - 1, 0), 0))
        nxt = pl.BlockSpec((HALO, SSD_XBC),
                           lambda b, c: (jnp.minimum((c0 + b * C + chunk_of(c) + 1) * hpc, n_halo - 1), 0))
        return cur, prv, nxt

    out_row = lambda chunk_of, w: pl.BlockSpec((CHUNK, w), lambda b, c: (b * C + chunk_of(c), 0))

    rev = lambda c: C - 1 - c
    cur, prv, nxt = specs(rev)
    yb, act = pl.pallas_call(
        functools.partial(_ssd_bwd_kernel, n_chunks=C),
        grid=(B, C),
        in_specs=[cur(SSD_XBC), prv, nxt, cur(LANES), _full(convw.shape), _full(convb.shape),
                  _full(dtb.shape), _full(a_neg.shape),
                  _full(consts["cum_b"].shape), _full(consts["expand_b"].shape)],
        out_specs=[out_row(rev, SSD_WIDTH), out_row(rev, SSD_XBC)],
        out_shape=[jax.ShapeDtypeStruct((B * L, SSD_WIDTH), F32),
                   jax.ShapeDtypeStruct((B * L, SSD_XBC), BF16)],
        scratch_shapes=[pltpu.VMEM((SSD_GN, SSD_WIDTH), F32)],
        compiler_params=_params(1, 1),
        name="ssd_bwd",
    )(xbc, xbc, xbc, dt, convw, convb, dtb, a_neg, consts["cum_b"], consts["expand_b"])

    fwd = lambda c: c
    cur, _, _ = specs(fwd)
    return pl.pallas_call(
        _ssd_fwd_kernel,
        grid=(B, C),
        in_specs=[out_row(fwd, SSD_XBC), cur(LANES), cur(SSD_WIDTH), out_row(fwd, SSD_WIDTH),
                  _full(dtb.shape), _full(a_neg.shape),
                  _full(consts["cum_f"].shape), _full(consts["expand_f"].shape),
                  _full(dskip_w.shape), _full(normg.shape)],
        out_specs=out_row(fwd, SSD_WIDTH),
        out_shape=jax.ShapeDtypeStruct((B * L, SSD_WIDTH), BF16),
        scratch_shapes=[pltpu.VMEM((SSD_GN, SSD_WIDTH), F32)],
        compiler_params=_params(1, 1),
        name="ssd_fwd",
    )(act, dt, z, yb, dtb, a_neg, consts["cum_f"], consts["expand_f"], dskip_w, normg)


@functools.lru_cache(maxsize=None)
def _ssd_consts_np():
    q = np.arange(CHUNK)
    cum_f = (q[None, :] <= q[:, None]).astype(np.float32)
    cum_b = (q[None, :] >= q[:, None]).astype(np.float32)
    out = {"cum_f": cum_f, "cum_b": cum_b}
    for name, off in (("expand_f", 0), ("expand_b", SSD_HEADS)):
        e = np.zeros((LANES, SSD_WIDTH), np.float32)
        for h in range(SSD_HEADS):
            e[off + h, h * SSD_HEAD_DIM:(h + 1) * SSD_HEAD_DIM] = 1.0
        out[name] = e
    return out


def _ssd_consts():
    return {k: jnp.asarray(v, BF16) for k, v in _ssd_consts_np().items()}


@functools.lru_cache(maxsize=None)
def _fnet_tables_np(L):
    L2 = CHUNK
    L1 = L // L2
    k1 = np.arange(L1)
    ang1 = 2.0 * np.pi * ((k1[:, None] * k1[None, :]) % L1) / L1
    f1 = np.concatenate([np.cos(ang1), np.sin(ang1)], axis=0) / math.sqrt(L1)
    k = k1[:, None, None] + L1 * np.arange(L2)[None, :, None]
    l2 = np.arange(L2)[None, None, :]
    ang2 = 2.0 * np.pi * ((k * l2) % L) / L
    gc = np.cos(ang2) / math.sqrt(L2)
    gs = np.sin(ang2) / math.sqrt(L2)
    return f1.astype(np.float32), gc.astype(np.float32), gs.astype(np.float32)


@functools.lru_cache(maxsize=None)
def _fnet_channel_np():
    c = np.arange(FNET_GDIM)
    ang = 2.0 * np.pi * ((c[:, None] * c[None, :]) % FNET_GDIM) / FNET_GDIM
    cc = np.cos(ang) / math.sqrt(FNET_GDIM)
    sc = np.sin(ang) / math.sqrt(FNET_GDIM)
    eye = np.eye(FNET_GROUPS)
    return np.concatenate([np.kron(eye, cc), np.kron(eye, -sc)], axis=1).astype(np.float32)


def _fnet_stage1_kernel(f_ref, xr_ref, xi_ref, yr_ref, yi_ref, *, L1):
    f = f_ref[...]
    pr = _dot(f, xr_ref[...])
    pi = _dot(f, xi_ref[...])
    yr_ref[...] = (pr[:L1] + pi[L1:]).astype(BF16)
    yi_ref[...] = (pi[:L1] - pr[L1:]).astype(BF16)


def _fnet_stage2_kernel(yr_ref, yi_ref, gc_ref, gs_ref, w_ref, b_ref, o_ref):
    zr = _dot(gc_ref[...], yr_ref[...]) + _dot(gs_ref[...], yi_ref[...])
    o_ref[...] = (_dot(zr.astype(BF16), w_ref[...]) + b_ref[...]).astype(BF16)


def _fourier_mixer(fr, fi, B, L, w_bd, b_f):
    L2 = CHUNK
    L1 = L // L2
    W = FNET_WIDTH
    f1, gc, gs = _fnet_tables_np(L)
    f1 = jnp.asarray(f1, BF16)
    gc = jnp.asarray(gc, BF16)
    gs = jnp.asarray(gs, BF16)
    n_cols = L2 * W
    tn = 4096
    xr = fr.reshape(B, L1, n_cols)
    xi = fi.reshape(B, L1, n_cols)
    blk = pl.BlockSpec((None, L1, tn), lambda b, j: (b, 0, j))
    yr, yi = pl.pallas_call(
        functools.partial(_fnet_stage1_kernel, L1=L1),
        grid=(B, n_cols // tn),
        in_specs=[_full(f1.shape), blk, blk],
        out_specs=[blk, blk],
        out_shape=[jax.ShapeDtypeStruct((B, L1, n_cols), BF16)] * 2,
        compiler_params=_params(2),
        name="fnet_stage1",
    )(f1, xr, xi)
    yr = yr.reshape(B, L1, L2, W)
    yi = yi.reshape(B, L1, L2, W)
    yblk = pl.BlockSpec((None, None, L2, W), lambda b, k: (b, k, 0, 0))
    gblk = pl.BlockSpec((None, L2, L2), lambda b, k: (k, 0, 0))
    out = pl.pallas_call(
        _fnet_stage2_kernel,
        grid=(B, L1),
        in_specs=[yblk, yblk, gblk, gblk, _full(w_bd.shape), _full(b_f.shape)],
        out_specs=pl.BlockSpec((None, L2, W), lambda b, k: (b, 0, k)),
        out_shape=jax.ShapeDtypeStruct((B, L2, L1 * W), BF16),
        compiler_params=_params(2),
        name="fnet_stage2",
    )(yr, yi, gc, gs, w_bd, b_f)
    return out.reshape(B * L, W)


def _gmlp_kernel(uv_ref, g_ref, b_ref, ws_ref, bs_ref, o_ref, *, n_chunks):
    uv = uv_ref[...].astype(F32)
    ge = 0.5 * uv * (1.0 + lax.erf(uv * (1.0 / math.sqrt(2.0))))
    u = ge[:, :GMLP_WIDTH]
    v = _ln_rows(ge[:, GMLP_WIDTH:], g_ref[...], b_ref[...])
    lane = lax.broadcasted_iota(jnp.int32, (CHUNK, GMLP_WIDTH), 1)
    ws = ws_ref[...]
    bs = bs_ref[...]
    for j in range(n_chunks):
        vj = v[j * CHUNK:(j + 1) * CHUNK]
        rhs = jnp.concatenate(
            [jnp.where(lane // GMLP_HDIM == h, vj, 0.0) for h in range(GMLP_HEADS)], axis=0)
        sv = _dot(ws, rhs.astype(BF16)) + bs
        o_ref[j * CHUNK:(j + 1) * CHUNK, :] = (u[j * CHUNK:(j + 1) * CHUNK] * sv).astype(BF16)


def _gmlp_mixer(uv, ln_g, ln_b, ws_cat, bs_wide):
    T = uv.shape[0]
    tm = TOK_TILE
    return pl.pallas_call(
        functools.partial(_gmlp_kernel, n_chunks=tm // CHUNK),
        grid=(T // tm,),
        in_specs=[pl.BlockSpec((tm, 2 * GMLP_WIDTH), lambda i: (i, 0)), _full(ln_g.shape),
                  _full(ln_b.shape), _full(ws_cat.shape), _full(bs_wide.shape)],
        out_specs=pl.BlockSpec((tm, GMLP_WIDTH), lambda i: (i, 0)),
        out_shape=jax.ShapeDtypeStruct((T, GMLP_WIDTH), BF16),
        compiler_params=_params(1),
        name="gmlp",
    )(uv, ln_g, ln_b, ws_cat, bs_wide)


def _outproj_kernel(x_ref, ssd_ref, fn_ref, gm_ref, wo_ref, bo_ref, g_ref, b_ref,
                    wrh_ref, wrl_ref, br_ref, tri_ref, x1_ref, idx_ref, gate_ref, cnt_ref, seen_ref):
    acc = _dot(ssd_ref[...], wo_ref[0:SSD_WIDTH, :])
    acc = acc + _dot(fn_ref[...], wo_ref[SSD_WIDTH:SSD_WIDTH + FNET_WIDTH, :])
    acc = acc + _dot(gm_ref[...], wo_ref[SSD_WIDTH + FNET_WIDTH:, :])
    h = DN_ALPHA * x_ref[...] + acc + bo_ref[...]
    x1 = _ln_rows(h, g_ref[...], b_ref[...])
    x1_ref[...] = x1

    xh, xl = _split2(x1)
    wrh = wrh_ref[...]
    logits = _dot(xh, wrh) + _dot(xl, wrh) + _dot(xh, wrl_ref[...]) + br_ref[...]
    lane = lax.broadcasted_iota(jnp.int32, logits.shape, 1)
    work = logits
    vals, idxs = [], []
    for _ in range(TOP_K):
        m = jnp.max(work, axis=-1, keepdims=True)
        i = jnp.min(jnp.where(work == m, lane, LANES), axis=-1, keepdims=True)
        vals.append(m)
        idxs.append(i)
        work = jnp.where(lane == i, -jnp.inf, work)
    exps = [jnp.exp(v - vals[0]) for v in vals]
    denom = exps[0]
    for e in exps[1:]:
        denom = denom + e
    @pl.when(pl.program_id(0) == 0)
    def _():
        seen_ref[...] = jnp.zeros_like(seen_ref)

    onehots = [lane == i for i in idxs]
    chosen = jnp.zeros(logits.shape, F32)
    for oh in onehots:
        chosen = jnp.where(oh, 1.0, chosen)
    before = _dot(tri_ref[...], chosen.astype(BF16)) + seen_ref[...]
    seen_ref[...] = seen_ref[...] + jnp.sum(chosen, axis=0, keepdims=True)
    cnt_ref[...] = seen_ref[...]

    idx_out = jnp.zeros(logits.shape, jnp.int32)
    gate_out = jnp.zeros(logits.shape, F32)
    for k in range(TOP_K):
        rank = jnp.sum(jnp.where(onehots[k], before, 0.0), axis=-1, keepdims=True).astype(jnp.int32)
        idx_out = jnp.where(lane == k, idxs[k], idx_out)
        idx_out = jnp.where(lane == TOP_K + k, rank, idx_out)
        gate_out = jnp.where(lane == k, exps[k] / denom, gate_out)
    idx_ref[...] = idx_out
    gate_ref[...] = gate_out


def _outproj(x, ssd, fn, gm, wo, bo, g, b, wrh, wrl, br):
    T, D = x.shape
    tm = TOK_TILE
    row = lambda w: pl.BlockSpec((tm, w), lambda i: (i, 0))
    q = np.arange(tm)
    tri = jnp.asarray((q[None, :] < q[:, None]).astype(np.float32), BF16)
    return pl.pallas_call(
        _outproj_kernel,
        grid=(T // tm,),
        in_specs=[row(D), row(SSD_WIDTH), row(FNET_WIDTH), row(GMLP_WIDTH), _full(wo.shape),
                  _full(bo.shape), _full(g.shape), _full(b.shape), _full(wrh.shape),
                  _full(wrl.shape), _full(br.shape), _full(tri.shape)],
        out_specs=[row(D), row(LANES), row(LANES), _full((1, LANES))],
        out_shape=[jax.ShapeDtypeStruct((T, D), F32),
                   jax.ShapeDtypeStruct((T, LANES), jnp.int32),
                   jax.ShapeDtypeStruct((T, LANES), F32),
                   jax.ShapeDtypeStruct((1, LANES), F32)],
        scratch_shapes=[pltpu.VMEM((1, LANES), F32)],
        compiler_params=_params(0, 1),
        name="outproj_ln_router",
    )(x, ssd, fn, gm, wo, bo, g, b, wrh, wrl, br, tri)


def _row_copy(src, dst, sem, src_row, dst_row):
    return pltpu.make_async_copy(src.at[pl.ds(src_row, 1), :], dst.at[pl.ds(dst_row, 1), :], sem)


def _rows_wait(src, dst, sem, n_rows):
    pltpu.make_async_copy(src.at[pl.ds(0, n_rows), :], dst.at[pl.ds(0, n_rows), :], sem).wait()


def _dispatch_kernel(pos_ref, x_ref, xbuf_hbm, sem, *, rows):
    def body(j, carry):
        for u in range(ROW_UNROLL):
            r = j * ROW_UNROLL + u
            for k in range(TOP_K):
                _row_copy(x_ref, xbuf_hbm, sem, r, pos_ref[r * TOP_K + k]).start(priority=k % 2)
        return carry

    lax.fori_loop(0, rows // ROW_UNROLL, body, 0)
    for _ in range(TOP_K):
        _rows_wait(x_ref, xbuf_hbm, sem, rows)


def _dispatch(x, pos_flat, n_slots):
    T, D = x.shape
    R = DISPATCH_TILE
    return pl.pallas_call(
        functools.partial(_dispatch_kernel, rows=R),
        grid=(T // R,),
        in_specs=[pl.BlockSpec((R * TOP_K,), lambda i: (i,), memory_space=pltpu.SMEM),
                  pl.BlockSpec((R, D), lambda i: (i, 0))],
        out_specs=pl.BlockSpec(memory_space=pl.ANY),
        out_shape=jax.ShapeDtypeStruct((n_slots, D), F32),
        scratch_shapes=[pltpu.SemaphoreType.DMA(())],
        compiler_params=_params(1),
        name="moe_dispatch",
    )(pos_flat, x)


def _expert_kernel(blk_e_ref, valid_ref, x_ref, wgu_ref, bgu_ref, wdn_ref, bdn_ref, y_ref, wgu_b, wdn_b):
    i = pl.program_id(0)
    valid = valid_ref[i]
    prev_e = blk_e_ref[jnp.maximum(i - 1, 0)]
    new_expert = jnp.logical_or(i == 0, blk_e_ref[i] != prev_e)

    @pl.when(jnp.logical_and(valid > 0, new_expert))
    def _():
        wgu_b[...] = wgu_ref[...].astype(BF16)
        wdn_b[...] = wdn_ref[...].astype(BF16)

    @pl.when(valid > 0)
    def _():
        row = lax.broadcasted_iota(jnp.int32, x_ref.shape, 0)
        xb = jnp.where(row < valid, x_ref[...], 0.0).astype(BF16)
        acc = jnp.broadcast_to(bdn_ref[...], y_ref.shape)
        nc = EXPERT_COL_CHUNK
        for j in range(D_EXPERT // nc):
            g = _dot(xb, wgu_b[:, j * nc:(j + 1) * nc]) + bgu_ref[:, j * nc:(j + 1) * nc]
            u = (_dot(xb, wgu_b[:, D_EXPERT + j * nc:D_EXPERT + (j + 1) * nc])
                 + bgu_ref[:, D_EXPERT + j * nc:D_EXPERT + (j + 1) * nc])
            g = jnp.minimum(g, SWIGLU_LIMIT)
            u = jnp.clip(u, -SWIGLU_LIMIT, SWIGLU_LIMIT)
            act = (u + 1.0) * (g * jax.nn.sigmoid(SWIGLU_ALPHA * g))
            acc = acc + _dot(act.astype(BF16), wdn_b[j * nc:(j + 1) * nc, :])
        y_ref[...] = acc

    @pl.when(valid <= 0)
    def _():
        y_ref[...] = jnp.zeros_like(y_ref)


def _expert_mlp(x_buf, blk_e, blk_valid, layer, wgu, bgu, wdn, bdn):
    P, D = x_buf.shape
    tm = EXPERT_TILE
    per_expert = lambda r, c: pl.BlockSpec((None, None, r, c), lambda i, be, bv: (layer, be[i], 0, 0))
    grid_spec = pltpu.PrefetchScalarGridSpec(
        num_scalar_prefetch=2,
        grid=(P // tm,),
        in_specs=[pl.BlockSpec((tm, D), lambda i, be, bv: (i, 0)),
                  per_expert(D, 2 * D_EXPERT), per_expert(1, 2 * D_EXPERT),
                  per_expert(D_EXPERT, D), per_expert(1, D)],
        out_specs=pl.BlockSpec((tm, D), lambda i, be, bv: (i, 0)),
        scratch_shapes=[pltpu.VMEM((D, 2 * D_EXPERT), BF16), pltpu.VMEM((D_EXPERT, D), BF16)],
    )
    return pl.pallas_call(
        _expert_kernel,
        grid_spec=grid_spec,
        out_shape=jax.ShapeDtypeStruct((P, D), F32),
        compiler_params=_params(0, 1),
        name="moe_expert_mlp",
    )(blk_e, blk_valid, x_buf, wgu, bgu, wdn, bdn)


def _combine_kernel(pos_ref, pos_next_ref, x_ref, gate_ref, y_hbm, g_ref, b_ref, o_ref, buf, sem, *, rows):
    i = pl.program_id(0)
    n = pl.num_programs(0)
    slot = i % 2

    def issue(p_ref, s):
        def body(j, carry):
            for u in range(ROW_UNROLL):
                r = j * ROW_UNROLL + u
                for k in range(TOP_K):
                    _row_copy(y_hbm, buf.at[s, k], sem.at[s], p_ref[r * TOP_K + k], r).start()
            return carry

        lax.fori_loop(0, rows // ROW_UNROLL, body, 0)

    @pl.when(i == 0)
    def _():
        issue(pos_ref, 0)

    @pl.when(i + 1 < n)
    def _():
        issue(pos_next_ref, 1 - slot)

    for k in range(TOP_K):
        _rows_wait(y_hbm, buf.at[slot, k], sem.at[slot], rows)
    gates = gate_ref[...]
    y = gates[:, 0:1] * buf[slot, 0]
    for k in range(1, TOP_K):
        y = y + gates[:, k:k + 1] * buf[slot, k]
    o_ref[...] = _ln_rows(DN_ALPHA * x_ref[...] + y, g_ref[...], b_ref[...])


def _combine(x, gates, pos_flat, y_buf, g, b):
    T, D = x.shape
    R = COMBINE_TILE
    n = T // R
    return pl.pallas_call(
        functools.partial(_combine_kernel, rows=R),
        grid=(n,),
        in_specs=[pl.BlockSpec((R * TOP_K,), lambda i: (i,), memory_space=pltpu.SMEM),
                  pl.BlockSpec((R * TOP_K,), lambda i: (jnp.minimum(i + 1, n - 1),), memory_space=pltpu.SMEM),
                  pl.BlockSpec((R, D), lambda i: (i, 0)),
                  pl.BlockSpec((R, LANES), lambda i: (i, 0)),
                  pl.BlockSpec(memory_space=pl.ANY),
                  _full(g.shape), _full(b.shape)],
        out_specs=pl.BlockSpec((R, D), lambda i: (i, 0)),
        out_shape=jax.ShapeDtypeStruct((T, D), F32),
        scratch_shapes=[pltpu.VMEM((2, TOP_K, R, D), F32), pltpu.SemaphoreType.DMA((2,))],
        compiler_params=_params(0, 1),
        name="moe_combine_ln",
    )(pos_flat, pos_flat, x, gates, y_buf, g, b)


def _routing_plan(idx, counts, n_tokens):
    tm = EXPERT_TILE
    e = idx[:, :TOP_K]
    rank = idx[:, TOP_K:2 * TOP_K]
    counts = counts[0, :N_EXPERTS].astype(jnp.int32)
    padded = (counts + tm - 1) // tm * tm
    pend = jnp.cumsum(padded)
    pstart = pend - padded
    pos = (pstart[e] + rank).astype(jnp.int32)
    n_slots = n_tokens * TOP_K + N_EXPERTS * tm
    blk_start = jnp.arange(n_slots // tm, dtype=jnp.int32) * tm
    blk_e = jnp.minimum(jnp.searchsorted(pend, blk_start, side="right"), N_EXPERTS - 1).astype(jnp.int32)
    blk_valid = jnp.clip(pstart[blk_e] + counts[blk_e] - blk_start, 0, tm).astype(jnp.int32)
    return pos.reshape(-1), blk_e, blk_valid, n_slots


def _moe(x1, idx, gates, counts, layer, wgu, bgu, wdn, bdn, ln_g, ln_b):
    T, D = x1.shape
    pos_flat, blk_e, blk_valid, n_slots = _routing_plan(idx, counts, T)
    x_buf = _dispatch(x1, pos_flat, n_slots)
    y_buf = _expert_mlp(x_buf, blk_e, blk_valid, layer, wgu, bgu[:, :, None, :], wdn, bdn[:, :, None, :])
    return _combine(x1, gates, pos_flat, y_buf, ln_g.reshape(1, D), ln_b.reshape(1, D))


def _trunk(xs, trunks, p):
    T, D = xs.shape
    x = _embed_ln(xs, p["emb_ln_g"], p["emb_ln_b"])
    mc = jnp.asarray(_fnet_channel_np(), BF16)
    for l in range(DEPTH):
        w_in = p["w_in"][l]
        wa = w_in[:, :SSD_WIDTH + SSD_XBC].astype(BF16)
        wdt = jnp.pad(w_in[:, SSD_WIDTH + SSD_XBC:SSD_IN], ((0, 0), (0, LANES - 2 * SSD_HEADS))).astype(BF16)
        wf = w_in[:, SSD_IN:SSD_IN + FNET_WIDTH].astype(BF16)
        wuv = w_in[:, SSD_IN + FNET_WIDTH:].astype(BF16)
        z, xbc, dt, fr, fi, uv = _inproj(x, wa, wdt, wf, wuv, mc)

        lane_pad = LANES - 2 * SSD_HEADS
        dtb = jnp.pad(p["dt_bias"][l].reshape(1, -1), ((0, 0), (0, lane_pad)))
        a_neg = jnp.pad(-jnp.exp(p["a_log"][l].astype(F32)).reshape(1, -1), ((0, 0), (0, lane_pad)))
        dskip_w = jnp.repeat(p["d_skip"][l], SSD_HEAD_DIM).reshape(1, SSD_WIDTH)
        normg = p["ssd_norm_g"][l].reshape(1, SSD_WIDTH)
        convw = p["conv_w"][l]
        convb = p["conv_b"][l].reshape(1, SSD_XBC)
        w_bd = jax.scipy.linalg.block_diag(*[p["fnet_w"][l, g] for g in range(FNET_GROUPS)]).astype(BF16)
        b_f = p["fnet_b"][l].reshape(1, FNET_WIDTH)

        ssd_parts, fn_parts = [], []
        row0 = 0
        for (B, L) in trunks:
            ssd_parts.append(_ssd_mixer(z, xbc, dt, row0, B, L, convw, convb, dtb, a_neg, dskip_w, normg))
            fn_parts.append(_fourier_mixer(fr[row0:row0 + B * L], fi[row0:row0 + B * L], B, L, w_bd, b_f))
            row0 += B * L
        ssd = jnp.concatenate(ssd_parts, axis=0)
        fn = jnp.concatenate(fn_parts, axis=0)

        ws_cat = jnp.concatenate([p["gmlp_ws"][l, h] for h in range(GMLP_HEADS)], axis=1).astype(BF16)
        bs_wide = jnp.repeat(p["gmlp_bs"][l].T, GMLP_HDIM, axis=1)
        gm = _gmlp_mixer(uv, p["gmlp_ln_g"][l].reshape(1, -1), p["gmlp_ln_b"][l].reshape(1, -1),
                         ws_cat, bs_wide)

        w_r = jnp.pad(p["router_w"][l], ((0, 0), (0, LANES - N_EXPERTS)))
        wrh = w_r.astype(BF16)
        wrl = (w_r - wrh.astype(F32)).astype(BF16)
        b_r = jnp.pad(p["router_b"][l].reshape(1, -1), ((0, 0), (0, LANES - N_EXPERTS)),
                      constant_values=NEG_BIG)
        x1, idx, gates, counts = _outproj(
            x, ssd, fn, gm, p["w_out"][l].astype(BF16), p["b_out"][l].reshape(1, D),
            p["ln1_g"][l].reshape(1, D), p["ln1_b"][l].reshape(1, D), wrh, wrl, b_r)

        x = _moe(x1, idx, gates, counts, l, p["exp_w_gu"], p["exp_b_gu"], p["exp_w_dn"], p["exp_b_dn"],
                 p["ln2_g"][l], p["ln2_b"][l])
    return x


def kernel(x_prompt, x_sample, emb_ln_g, emb_ln_b, w_in, conv_w, conv_b, dt_bias, a_log, d_skip, ssd_norm_g, fnet_w, fnet_b, gmlp_ln_g, gmlp_ln_b, gmlp_ws, gmlp_bs, w_out, b_out, ln1_g, ln1_b, router_w, router_b, exp_w_gu, exp_b_gu, exp_w_dn, exp_b_dn, ln2_g, ln2_b):
    p = dict(emb_ln_g=emb_ln_g, emb_ln_b=emb_ln_b, w_in=w_in, conv_w=conv_w, conv_b=conv_b,
             dt_bias=dt_bias, a_log=a_log, d_skip=d_skip, ssd_norm_g=ssd_norm_g, fnet_w=fnet_w,
             fnet_b=fnet_b, gmlp_ln_g=gmlp_ln_g, gmlp_ln_b=gmlp_ln_b, gmlp_ws=gmlp_ws, gmlp_bs=gmlp_bs,
             w_out=w_out, b_out=b_out, ln1_g=ln1_g, ln1_b=ln1_b, router_w=router_w, router_b=router_b,
             exp_w_gu=exp_w_gu, exp_b_gu=exp_b_gu, exp_w_dn=exp_w_dn, exp_b_dn=exp_b_dn,
             ln2_g=ln2_g, ln2_b=ln2_b)
    D = x_prompt.shape[-1]
    trunks = [x_prompt.shape[:2], x_sample.shape[:2]]
    xs = jnp.concatenate([x_prompt.reshape(-1, D), x_sample.reshape(-1, D)], axis=0)
    y = _trunk(xs, trunks, p)
    n_p = x_prompt.shape[0] * x_prompt.shape[1]
    return (y[:n_p].reshape(x_prompt.shape), y[n_p:].reshape(x_sample.shape))
```

```python
import functools
import math

import numpy as np
import jax
import jax.numpy as jnp
from jax import lax
from jax.experimental import pallas as pl
from jax.experimental.pallas import tpu as pltpu

F32 = jnp.float32
BF16 = jnp.bfloat16

D_MODEL = 1024
DEPTH = 4
SSD_WIDTH = 512
SSD_HEAD_DIM = 64
SSD_HEADS = 8
SSD_GROUPS = 2
SSD_STATE = 64
SSD_CONV = 5
SSD_GN = SSD_GROUPS * SSD_STATE
SSD_XBC = SSD_WIDTH + 2 * SSD_GN
SSD_IN = SSD_WIDTH + SSD_XBC + 2 * SSD_HEADS
FNET_WIDTH = 256
FNET_GROUPS = 4
FNET_GDIM = 64
GMLP_WIDTH = 256
GMLP_HEADS = 4
GMLP_HDIM = 64
N_EXPERTS = 32
TOP_K = 4
D_EXPERT = D_MODEL
SWIGLU_LIMIT = 7.0
SWIGLU_ALPHA = 1.702
LN_EPS = 1e-5
RMS_EPS = 1e-5
DN_ALPHA = (2 * DEPTH) ** 0.25

LANES = 128
CHUNK = 128
HALO = 16
VMEM_LIMIT = 56 * 1024 * 1024

TOK_TILE = 512
EXPERT_TILE = 512
EXPERT_COL_CHUNK = 512
DISPATCH_TILE = 512
COMBINE_TILE = 256
ROW_UNROLL = 8
SSD_STEP_CHUNKS = 4
FNET_ROW_GROUP = 16
NEG_BIG = -1e30


def _dot(a, b):
    return jnp.dot(a, b, preferred_element_type=F32)


def _split2(v):
    hi = v.astype(BF16)
    lo = (v - hi.astype(F32)).astype(BF16)
    return hi, lo


def _split3(v):
    hi = v.astype(BF16)
    r = v - hi.astype(F32)
    mid = r.astype(BF16)
    lo = (r - mid.astype(F32)).astype(BF16)
    return hi, mid, lo


def _ln_rows(xf, g, b):
    mu = jnp.mean(xf, -1, keepdims=True)
    xc = xf - mu
    var = jnp.mean(xc * xc, -1, keepdims=True)
    return xc * lax.rsqrt(var + LN_EPS) * g + b


def _params(n_parallel=1, n_arbitrary=0):
    sem = ("parallel",) * n_parallel + ("arbitrary",) * n_arbitrary
    return pltpu.CompilerParams(dimension_semantics=sem, vmem_limit_bytes=VMEM_LIMIT)


def _full(shape):
    nd = len(shape)
    return pl.BlockSpec(shape, lambda *_: (0,) * nd)


def _embed_ln_kernel(x_ref, g_ref, b_ref, o_ref):
    o_ref[...] = _ln_rows(x_ref[...], g_ref[...], b_ref[...])


def _embed_ln(x, g, b):
    T, D = x.shape
    return pl.pallas_call(
        _embed_ln_kernel,
        grid=(T // TOK_TILE,),
        in_specs=[pl.BlockSpec((TOK_TILE, D), lambda i: (i, 0)), _full((1, D)), _full((1, D))],
        out_specs=pl.BlockSpec((TOK_TILE, D), lambda i: (i, 0)),
        out_shape=jax.ShapeDtypeStruct((T, D), F32),
        compiler_params=_params(1),
        name="embed_ln",
    )(x, g.reshape(1, D), b.reshape(1, D))


def _inproj_kernel(x_ref, wa_ref, wdt_ref, wf_ref, wuv_ref, mc_ref,
                   z_ref, xbc_ref, dt_ref, fr_ref, fi_ref, uv_ref):
    xb = x_ref[...].astype(BF16)
    a = _dot(xb, wa_ref[...])
    z_ref[...] = a[:, :SSD_WIDTH].astype(BF16)
    xbc_ref[...] = a[:, SSD_WIDTH:].astype(BF16)
    dt_ref[...] = _dot(xb, wdt_ref[...])
    f = _dot(xb, wf_ref[...]).astype(BF16)
    fri = _dot(f, mc_ref[...])
    fr_ref[...] = fri[:, :FNET_WIDTH]
    fi_ref[...] = fri[:, FNET_WIDTH:]
    uv_ref[...] = _dot(xb, wuv_ref[...]).astype(BF16)


def _inproj(x, wa, wdt, wf, wuv, mc):
    T, D = x.shape
    tm = TOK_TILE
    row = lambda w: pl.BlockSpec((tm, w), lambda i: (i, 0))
    return pl.pallas_call(
        _inproj_kernel,
        grid=(T // tm,),
        in_specs=[row(D), _full(wa.shape), _full(wdt.shape), _full(wf.shape), _full(wuv.shape),
                  _full(mc.shape)],
        out_specs=[row(SSD_WIDTH), row(SSD_XBC), row(LANES), row(FNET_WIDTH), row(FNET_WIDTH),
                   row(2 * GMLP_WIDTH)],
        out_shape=[jax.ShapeDtypeStruct((T, SSD_WIDTH), BF16),
                   jax.ShapeDtypeStruct((T, SSD_XBC), BF16),
                   jax.ShapeDtypeStruct((T, LANES), F32),
                   jax.ShapeDtypeStruct((T, FNET_WIDTH), F32),
                   jax.ShapeDtypeStruct((T, FNET_WIDTH), F32),
                   jax.ShapeDtypeStruct((T, 2 * GMLP_WIDTH), BF16)],
        compiler_params=_params(1),
        name="inproj",
    )(x, wa, wdt, wf, wuv, mc)


def _conv_silu(xc_ref, xp_ref, xn_ref, convw_ref, convb_ref, first, last):
    Q = xc_ref.shape[0]
    cur = xc_ref[...].astype(F32)
    prev = jnp.where(first, 0.0, xp_ref[...].astype(F32))
    nxt = jnp.where(last, 0.0, xn_ref[...].astype(F32))
    ext = jnp.concatenate([prev[HALO - 8:], cur, nxt[:8]], axis=0)
    n_ext = Q + 16
    pad = SSD_CONV // 2
    conv = convb_ref[...]
    for k in range(SSD_CONV):
        shift = (pad - k) % n_ext
        tap = ext if shift == 0 else pltpu.roll(ext, shift, axis=0)
        conv = conv + tap[8:8 + Q] * convw_ref[k:k + 1, :]
    return conv * jax.nn.sigmoid(conv)


def _ssd_chunk(act, dt_raw, dtb_ref, a_ref, cum_ref, expand_ref, state_ref, *, backward):
    Q = CHUNK
    h_off = SSD_HEADS if backward else 0
    xs = act[:, :SSD_WIDTH]
    bm = act[:, SSD_WIDTH:SSD_WIDTH + SSD_GN]
    cm = act[:, SSD_WIDTH + SSD_GN:]

    dt = jax.nn.softplus(dt_raw + dtb_ref[...])
    adt = dt * a_ref[...]
    cum = cum_ref[...]
    h3 = _split3(adt)
    acum = _dot(cum, h3[0]) + _dot(cum, h3[1]) + _dot(cum, h3[2])
    acum_t = acum.T
    total = acum[0:1, :] if backward else acum[Q - 1:Q, :]
    dec_in = jnp.exp(acum)
    dec_end = jnp.exp(total - acum)

    expand = expand_ref[...]

    def widen(v):
        hi, lo = _split2(v)
        return _dot(hi, expand) + _dot(lo, expand)

    dt_w = widen(dt)
    dec_in_w = widen(dec_in)
    dec_end_w = widen(dec_end)
    tdec_w = dec_in_w[0:1, :] if backward else dec_in_w[Q - 1:Q, :]

    xdt = xs * dt_w

    row = lax.broadcasted_iota(jnp.int32, (Q, Q), 0)
    col = lax.broadcasted_iota(jnp.int32, (Q, Q), 1)
    mask = (col > row) if backward else (col <= row)
    lane = lax.broadcasted_iota(jnp.int32, (Q, LANES), 1)
    lo_half = lane < SSD_STATE

    bm_b = bm.astype(BF16)
    cb = []
    for g in range(SSD_GROUPS):
        cm_g = jnp.where(lo_half if g == 0 else jnp.logical_not(lo_half), cm, 0.0).astype(BF16)
        cb.append(lax.dot_general(cm_g, bm_b, (((1,), (1,)), ((), ())),
                                  preferred_element_type=F32))

    y_parts = []
    for pair in range(SSD_HEADS // 2):
        gmats = []
        for r in range(2):
            h = 2 * pair + r
            hl = h_off + h
            diff = acum[:, hl:hl + 1] - acum_t[hl:hl + 1, :]
            dec = jnp.exp(jnp.where(mask, diff, NEG_BIG))
            gmats.append((cb[h // (SSD_HEADS // SSD_GROUPS)] * dec).astype(BF16))
        lhs = jnp.concatenate(gmats, axis=1)
        x2 = xdt[:, pair * LANES:(pair + 1) * LANES]
        rhs = jnp.concatenate([jnp.where(lo_half, x2, 0.0), jnp.where(lo_half, 0.0, x2)],
                              axis=0).astype(BF16)
        y_parts.append(_dot(lhs, rhs))
    y_diag = jnp.concatenate(y_parts, axis=1)

    srow = lax.broadcasted_iota(jnp.int32, (SSD_GN, SSD_WIDTH), 0)
    scol = lax.broadcasted_iota(jnp.int32, (SSD_GN, SSD_WIDTH), 1)
    on_group = (srow // SSD_STATE) == (scol // (SSD_WIDTH // SSD_GROUPS))
    st_new = _dot(bm.T.astype(BF16), (xdt * dec_end_w).astype(BF16))
    st_new = jnp.where(on_group, st_new, 0.0)
    st_prev = state_ref[...]
    y_off = _dot(cm.astype(BF16), st_prev.astype(BF16)) * dec_in_w
    state_ref[...] = st_prev * tdec_w + st_new
    return y_diag + y_off, xs


def _ssd_bwd_kernel(xc_ref, xp_ref, xn_ref, dt_ref, convw_ref, convb_ref, dtb_ref, a_ref,
                    cum_ref, expand_ref, yb_ref, act_ref, state_ref, *, n_steps):
    c = pl.program_id(1)
    step = n_steps - 1 - c

    @pl.when(c == 0)
    def _():
        state_ref[...] = jnp.zeros_like(state_ref)

    act = _conv_silu(xc_ref, xp_ref, xn_ref, convw_ref, convb_ref,
                     first=step == 0, last=step == n_steps - 1)
    act_ref[...] = act.astype(BF16)
    for s in reversed(range(SSD_STEP_CHUNKS)):
        rows = slice(s * CHUNK, (s + 1) * CHUNK)
        y, _ = _ssd_chunk(act[rows], dt_ref[rows, :], dtb_ref, a_ref, cum_ref, expand_ref, state_ref,
                          backward=True)
        yb_ref[rows, :] = y


def _ssd_fwd_kernel(act_ref, dt_ref, z_ref, yb_ref, dtb_ref, a_ref, cum_ref, expand_ref,
                    dskip_ref, normg_ref, o_ref, state_ref):
    c = pl.program_id(1)

    @pl.when(c == 0)
    def _():
        state_ref[...] = jnp.zeros_like(state_ref)

    gw = SSD_WIDTH // SSD_GROUPS
    for s in range(SSD_STEP_CHUNKS):
        rows = slice(s * CHUNK, (s + 1) * CHUNK)
        y, xs = _ssd_chunk(act_ref[rows, :].astype(F32), dt_ref[rows, :], dtb_ref, a_ref, cum_ref,
                           expand_ref, state_ref, backward=False)
        z = z_ref[rows, :].astype(F32)
        y = (y + yb_ref[rows, :] + xs * dskip_ref[...]) * (z * jax.nn.sigmoid(z))
        outs = []
        for g in range(SSD_GROUPS):
            yg = y[:, g * gw:(g + 1) * gw]
            outs.append(yg * lax.rsqrt(jnp.mean(yg * yg, -1, keepdims=True) + RMS_EPS))
        o_ref[rows, :] = (jnp.concatenate(outs, axis=1) * normg_ref[...]).astype(BF16)


def _ssd_mixer(z, xbc, dt, row0, B, L, convw, convb, dtb, a_neg, dskip_w, normg):
    R = SSD_STEP_CHUNKS * CHUNK
    C = L // R
    c0 = row0 // R
    hpc = R // HALO
    n_halo = z.shape[0] // HALO
    consts = _ssd_consts()

    def specs(chunk_of):
        cur = lambda w: pl.BlockSpec((R, w), lambda b, c: (c0 + b * C + chunk_of(c), 0))
        prv = pl.BlockSpec((HALO, SSD_XBC),
                           lambda b, c: (jnp.maximum((c0 + b * C + chunk_of(c)) * hpc - 1, 0), 0))
        nxt = pl.BlockSpec((HALO, SSD_XBC),
                           lambda b, c: (jnp.minimum((c0 + b * C + chunk_of(c) + 1) * hpc, n_halo - 1), 0))
        return cur, prv, nxt

    out_row = lambda chunk_of, w: pl.BlockSpec((R, w), lambda b, c: (b * C + chunk_of(c), 0))

    rev = lambda c: C - 1 - c
    cur, prv, nxt = specs(rev)
    yb, act = pl.pallas_call(
        functools.partial(_ssd_bwd_kernel, n_steps=C),
        grid=(B, C),
        in_specs=[cur(SSD_XBC), prv, nxt, cur(LANES), _full(convw.shape), _full(convb.shape),
                  _full(dtb.shape), _full(a_neg.shape),
                  _full(consts["cum_b"].shape), _full(consts["expand_b"].shape)],
        out_specs=[out_row(rev, SSD_WIDTH), out_row(rev, SSD_XBC)],
        out_shape=[jax.ShapeDtypeStruct((B * L, SSD_WIDTH), F32),
                   jax.ShapeDtypeStruct((B * L, SSD_XBC), BF16)],
        scratch_shapes=[pltpu.VMEM((SSD_GN, SSD_WIDTH), F32)],
        compiler_params=_params(1, 1),
        name="ssd_bwd",
    )(xbc, xbc, xbc, dt, convw, convb, dtb, a_neg, consts["cum_b"], consts["expand_b"])

    fwd = lambda c: c
    cur, _, _ = specs(fwd)
    return pl.pallas_call(
        _ssd_fwd_kernel,
        grid=(B, C),
        in_specs=[out_row(fwd, SSD_XBC), cur(LANES), cur(SSD_WIDTH), out_row(fwd, SSD_WIDTH),
                  _full(dtb.shape), _full(a_neg.shape),
                  _full(consts["cum_f"].shape), _full(consts["expand_f"].shape),
                  _full(dskip_w.shape), _full(normg.shape)],
        out_specs=out_row(fwd, SSD_WIDTH),
        out_shape=jax.ShapeDtypeStruct((B * L, SSD_WIDTH), BF16),
        scratch_shapes=[pltpu.VMEM((SSD_GN, SSD_WIDTH), F32)],
        compiler_params=_params(1, 1),
        name="ssd_fwd",
    )(act, dt, z, yb, dtb, a_neg, consts["cum_f"], consts["expand_f"], dskip_w, normg)


@functools.lru_cache(maxsize=None)
def _ssd_consts_np():
    q = np.arange(CHUNK)
    cum_f = (q[None, :] <= q[:, None]).astype(np.float32)
    cum_b = (q[None, :] >= q[:, None]).astype(np.float32)
    out = {"cum_f": cum_f, "cum_b": cum_b}
    for name, off in (("expand_f", 0), ("expand_b", SSD_HEADS)):
        e = np.zeros((LANES, SSD_WIDTH), np.float32)
        for h in range(SSD_HEADS):
            e[off + h, h * SSD_HEAD_DIM:(h + 1) * SSD_HEAD_DIM] = 1.0
        out[name] = e
    return out


def _ssd_consts():
    return {k: jnp.asarray(v, BF16) for k, v in _ssd_consts_np().items()}


@functools.lru_cache(maxsize=None)
def _fnet_tables_np(L):
    L2 = CHUNK
    L1 = L // L2
    k1 = np.arange(L1)
    ang1 = 2.0 * np.pi * ((k1[:, None] * k1[None, :]) % L1) / L1
    f1 = np.concatenate([np.cos(ang1), np.sin(ang1)], axis=0) / math.sqrt(L1)
    k = k1[:, None, None] + L1 * np.arange(L2)[None, :, None]
    l2 = np.arange(L2)[None, None, :]
    ang2 = 2.0 * np.pi * ((k * l2) % L) / L
    gc = np.cos(ang2) / math.sqrt(L2)
    gs = np.sin(ang2) / math.sqrt(L2)
    return f1.astype(np.float32), gc.astype(np.float32), gs.astype(np.float32)


@functools.lru_cache(maxsize=None)
def _fnet_channel_np():
    c = np.arange(FNET_GDIM)
    ang = 2.0 * np.pi * ((c[:, None] * c[None, :]) % FNET_GDIM) / FNET_GDIM
    cc = np.cos(ang) / math.sqrt(FNET_GDIM)
    sc = np.sin(ang) / math.sqrt(FNET_GDIM)
    eye = np.eye(FNET_GROUPS)
    return np.concatenate([np.kron(eye, cc), np.kron(eye, -sc)], axis=1).astype(np.float32)


def _fnet_stage1_kernel(f_ref, xr_ref, xi_ref, yr_ref, yi_ref, *, L1, group):
    W = FNET_WIDTH
    f = f_ref[...]
    for j in range(group):
        x = jnp.concatenate([xr_ref[:, j, :], xi_ref[:, j, :]], axis=1).astype(BF16)
        p = _dot(f, x)
        yr_ref[:, j, :] = p[:L1, :W] + p[L1:, W:]
        yi_ref[:, j, :] = p[:L1, W:] - p[L1:, :W]


def _fnet_stage2_kernel(yr_ref, yi_ref, g_ref, w_ref, b_ref, o_ref, *, group):
    for j in range(group):
        y = jnp.concatenate([yr_ref[j], yi_ref[j]], axis=0).astype(BF16)
        zr = _dot(g_ref[j], y)
        o_ref[:, j, :] = _dot(zr.astype(BF16), w_ref[...]) + b_ref[...]


def _fourier_mixer(fr, fi, row0, B, L, w_bd, b_f):
    L2 = CHUNK
    L1 = L // L2
    W = FNET_WIDTH
    f1, gc, gs = _fnet_tables_np(L)
    f1 = jnp.asarray(f1, BF16)
    g_cat = jnp.asarray(np.concatenate([gc, gs], axis=2), BF16)
    xr = fr.reshape(-1, L2, W)
    xi = fi.reshape(-1, L2, W)
    blk0 = row0 // L2 // L1
    grp1 = FNET_ROW_GROUP
    xblk = pl.BlockSpec((L1, grp1, W), lambda b, j: (blk0 + b, j, 0))
    yblk = pl.BlockSpec((L1, grp1, W), lambda b, j: (b, j, 0))
    yr, yi = pl.pallas_call(
        functools.partial(_fnet_stage1_kernel, L1=L1, group=grp1),
        grid=(B, L2 // grp1),
        in_specs=[_full(f1.shape), xblk, xblk],
        out_specs=[yblk, yblk],
        out_shape=[jax.ShapeDtypeStruct((B * L1, L2, W), F32)] * 2,
        compiler_params=_params(2),
        name="fnet_stage1",
    )(f1, xr, xi)
    grp2 = FNET_ROW_GROUP // 2
    n2 = L1 // grp2
    yblk2 = pl.BlockSpec((grp2, L2, W), lambda b, k: (b * n2 + k, 0, 0))
    gblk = pl.BlockSpec((grp2, L2, 2 * L2), lambda b, k: (k, 0, 0))
    out = pl.pallas_call(
        functools.partial(_fnet_stage2_kernel, group=grp2),
        grid=(B, n2),
        in_specs=[yblk2, yblk2, gblk, _full(w_bd.shape), _full(b_f.shape)],
        out_specs=pl.BlockSpec((L2, grp2, W), lambda b, k: (b, k, 0)),
        out_shape=jax.ShapeDtypeStruct((B * L2, L1, W), F32),
        compiler_params=_params(2),
        name="fnet_stage2",
    )(yr, yi, g_cat, w_bd, b_f)
    return out.reshape(B * L, W)


def _gmlp_kernel(uv_ref, g_ref, b_ref, ws_ref, bs_ref, o_ref, *, n_chunks):
    uv = uv_ref[...].astype(F32)
    ge = 0.5 * uv * (1.0 + lax.erf(uv * (1.0 / math.sqrt(2.0))))
    u = ge[:, :GMLP_WIDTH]
    v = _ln_rows(ge[:, GMLP_WIDTH:], g_ref[...], b_ref[...])
    lane = lax.broadcasted_iota(jnp.int32, (CHUNK, GMLP_WIDTH), 1)
    ws = ws_ref[...]
    bs = bs_ref[...]
    for j in range(n_chunks):
        vj = v[j * CHUNK:(j + 1) * CHUNK]
        rhs = jnp.concatenate(
            [jnp.where(lane // GMLP_HDIM == h, vj, 0.0) for h in range(GMLP_HEADS)], axis=0)
        sv = _dot(ws, rhs.astype(BF16)) + bs
        o_ref[j * CHUNK:(j + 1) * CHUNK, :] = (u[j * CHUNK:(j + 1) * CHUNK] * sv).astype(BF16)


def _gmlp_mixer(uv, ln_g, ln_b, ws_cat, bs_wide):
    T = uv.shape[0]
    tm = TOK_TILE
    return pl.pallas_call(
        functools.partial(_gmlp_kernel, n_chunks=tm // CHUNK),
        grid=(T // tm,),
        in_specs=[pl.BlockSpec((tm, 2 * GMLP_WIDTH), lambda i: (i, 0)), _full(ln_g.shape),
                  _full(ln_b.shape), _full(ws_cat.shape), _full(bs_wide.shape)],
        out_specs=pl.BlockSpec((tm, GMLP_WIDTH), lambda i: (i, 0)),
        out_shape=jax.ShapeDtypeStruct((T, GMLP_WIDTH), BF16),
        compiler_params=_params(1),
        name="gmlp",
    )(uv, ln_g, ln_b, ws_cat, bs_wide)


def _outproj_kernel(x_ref, ssd_ref, fn_ref, gm_ref, wo_ref, bo_ref, g_ref, b_ref,
                    wrh_ref, wrl_ref, br_ref, tri_ref, x1_ref, idx_ref, gate_ref, cnt_ref, seen_ref):
    acc = _dot(ssd_ref[...], wo_ref[0:SSD_WIDTH, :])
    acc = acc + _dot(fn_ref[...].astype(BF16), wo_ref[SSD_WIDTH:SSD_WIDTH + FNET_WIDTH, :])
    acc = acc + _dot(gm_ref[...], wo_ref[SSD_WIDTH + FNET_WIDTH:, :])
    h = DN_ALPHA * x_ref[...] + acc + bo_ref[...]
    x1 = _ln_rows(h, g_ref[...], b_ref[...])
    x1_ref[...] = x1

    xh, xl = _split2(x1)
    wrh = wrh_ref[...]
    logits = _dot(xh, wrh) + _dot(xl, wrh) + _dot(xh, wrl_ref[...]) + br_ref[...]
    lane = lax.broadcasted_iota(jnp.int32, logits.shape, 1)
    work = logits
    vals, idxs = [], []
    for _ in range(TOP_K):
        m = jnp.max(work, axis=-1, keepdims=True)
        i = jnp.min(jnp.where(work == m, lane, LANES), axis=-1, keepdims=True)
        vals.append(m)
        idxs.append(i)
        work = jnp.where(lane == i, -jnp.inf, work)
    exps = [jnp.exp(v - vals[0]) for v in vals]
    denom = exps[0]
    for e in exps[1:]:
        denom = denom + e
    @pl.when(pl.program_id(0) == 0)
    def _():
        seen_ref[...] = jnp.zeros_like(seen_ref)

    onehots = [lane == i for i in idxs]
    chosen = jnp.zeros(logits.shape, F32)
    for oh in onehots:
        chosen = jnp.where(oh, 1.0, chosen)
    before = _dot(tri_ref[...], chosen.astype(BF16)) + seen_ref[...]
    seen_ref[...] = seen_ref[...] + jnp.sum(chosen, axis=0, keepdims=True)
    cnt_ref[...] = seen_ref[...]

    idx_out = jnp.zeros(logits.shape, jnp.int32)
    gate_out = jnp.zeros(logits.shape, F32)
    for k in range(TOP_K):
        rank = jnp.sum(jnp.where(onehots[k], before, 0.0), axis=-1, keepdims=True).astype(jnp.int32)
        idx_out = jnp.where(lane == k, idxs[k], idx_out)
        idx_out = jnp.where(lane == TOP_K + k, rank, idx_out)
        gate_out = jnp.where(lane == k, exps[k] / denom, gate_out)
    idx_ref[...] = idx_out
    gate_ref[...] = gate_out


def _outproj(x, ssd, fn, gm, wo, bo, g, b, wrh, wrl, br):
    T, D = x.shape
    tm = TOK_TILE
    row = lambda w: pl.BlockSpec((tm, w), lambda i: (i, 0))
    q = np.arange(tm)
    tri = jnp.asarray((q[None, :] < q[:, None]).astype(np.float32), BF16)
    return pl.pallas_call(
        _outproj_kernel,
        grid=(T // tm,),
        in_specs=[row(D), row(SSD_WIDTH), row(FNET_WIDTH), row(GMLP_WIDTH), _full(wo.shape),
                  _full(bo.shape), _full(g.shape), _full(b.shape), _full(wrh.shape),
                  _full(wrl.shape), _full(br.shape), _full(tri.shape)],
        out_specs=[row(D), row(LANES), row(LANES), _full((1, LANES))],
        out_shape=[jax.ShapeDtypeStruct((T, D), F32),
                   jax.ShapeDtypeStruct((T, LANES), jnp.int32),
                   jax.ShapeDtypeStruct((T, LANES), F32),
                   jax.ShapeDtypeStruct((1, LANES), F32)],
        scratch_shapes=[pltpu.VMEM((1, LANES), F32)],
        compiler_params=_params(0, 1),
        name="outproj_ln_router",
    )(x, ssd, fn, gm, wo, bo, g, b, wrh, wrl, br, tri)


def _row_copy(src, dst, sem, src_row, dst_row):
    return pltpu.make_async_copy(src.at[pl.ds(src_row, 1), :], dst.at[pl.ds(dst_row, 1), :], sem)


def _rows_wait(src, dst, sem, n_rows):
    pltpu.make_async_copy(src.at[pl.ds(0, n_rows), :], dst.at[pl.ds(0, n_rows), :], sem).wait()


def _dispatch_kernel(pos_ref, x_ref, xbuf_hbm, sem, *, rows):
    def body(j, carry):
        for u in range(ROW_UNROLL):
            r = j * ROW_UNROLL + u
            for k in range(TOP_K):
                _row_copy(x_ref, xbuf_hbm, sem, r, pos_ref[r * TOP_K + k]).start(priority=k % 2)
        return carry

    lax.fori_loop(0, rows // ROW_UNROLL, body, 0)
    for _ in range(TOP_K):
        _rows_wait(x_ref, xbuf_hbm, sem, rows)


def _dispatch(x, pos_flat, n_slots):
    T, D = x.shape
    R = DISPATCH_TILE
    return pl.pallas_call(
        functools.partial(_dispatch_kernel, rows=R),
        grid=(T // R,),
        in_specs=[pl.BlockSpec((R * TOP_K,), lambda i: (i,), memory_space=pltpu.SMEM),
                  pl.BlockSpec((R, D), lambda i: (i, 0))],
        out_specs=pl.BlockSpec(memory_space=pl.ANY),
        out_shape=jax.ShapeDtypeStruct((n_slots, D), F32),
        scratch_shapes=[pltpu.SemaphoreType.DMA(())],
        compiler_params=_params(1),
        name="moe_dispatch",
    )(pos_flat, x)


def _expert_kernel(blk_e_ref, valid_ref, x_ref, wgu_ref, bgu_ref, wdn_ref, bdn_ref, y_ref, wgu_b, wdn_b):
    i = pl.program_id(0)
    valid = valid_ref[i]
    prev_e = blk_e_ref[jnp.maximum(i - 1, 0)]
    new_expert = jnp.logical_or(i == 0, blk_e_ref[i] != prev_e)

    @pl.when(jnp.logical_and(valid > 0, new_expert))
    def _():
        wgu_b[...] = wgu_ref[...].astype(BF16)
        wdn_b[...] = wdn_ref[...].astype(BF16)

    @pl.when(valid > 0)
    def _():
        row = lax.broadcasted_iota(jnp.int32, x_ref.shape, 0)
        xb = jnp.where(row < valid, x_ref[...], 0.0).astype(BF16)
        acc = jnp.broadcast_to(bdn_ref[...], y_ref.shape)
        nc = EXPERT_COL_CHUNK
        for j in range(D_EXPERT // nc):
            g = _dot(xb, wgu_b[:, j * nc:(j + 1) * nc]) + bgu_ref[:, j * nc:(j + 1) * nc]
            u = (_dot(xb, wgu_b[:, D_EXPERT + j * nc:D_EXPERT + (j + 1) * nc])
                 + bgu_ref[:, D_EXPERT + j * nc:D_EXPERT + (j + 1) * nc])
            g = jnp.minimum(g, SWIGLU_LIMIT)
            u = jnp.clip(u, -SWIGLU_LIMIT, SWIGLU_LIMIT)
            act = (u + 1.0) * (g * jax.nn.sigmoid(SWIGLU_ALPHA * g))
            acc = acc + _dot(act.astype(BF16), wdn_b[j * nc:(j + 1) * nc, :])
        y_ref[...] = acc

    @pl.when(valid <= 0)
    def _():
        y_ref[...] = jnp.zeros_like(y_ref)


def _expert_mlp(x_buf, blk_e, blk_valid, layer, wgu, bgu, wdn, bdn):
    P, D = x_buf.shape
    tm = EXPERT_TILE
    per_expert = lambda r, c: pl.BlockSpec((None, None, r, c), lambda i, be, bv: (layer, be[i], 0, 0))
    grid_spec = pltpu.PrefetchScalarGridSpec(
        num_scalar_prefetch=2,
        grid=(P // tm,),
        in_specs=[pl.BlockSpec((tm, D), lambda i, be, bv: (i, 0)),
                  per_expert(D, 2 * D_EXPERT), per_expert(1, 2 * D_EXPERT),
                  per_expert(D_EXPERT, D), per_expert(1, D)],
        out_specs=pl.BlockSpec((tm, D), lambda i, be, bv: (i, 0)),
        scratch_shapes=[pltpu.VMEM((D, 2 * D_EXPERT), BF16), pltpu.VMEM((D_EXPERT, D), BF16)],
    )
    return pl.pallas_call(
        _expert_kernel,
        grid_spec=grid_spec,
        out_shape=jax.ShapeDtypeStruct((P, D), F32),
        compiler_params=_params(0, 1),
        name="moe_expert_mlp",
    )(blk_e, blk_valid, x_buf, wgu, bgu, wdn, bdn)


def _combine_kernel(pos_ref, pos_next_ref, x_ref, gate_ref, y_hbm, g_ref, b_ref, o_ref, buf, sem, *, rows):
    i = pl.program_id(0)
    n = pl.num_programs(0)
    slot = i % 2

    def issue(p_ref, s):
        def body(j, carry):
            for u in range(ROW_UNROLL):
                r = j * ROW_UNROLL + u
                for k in range(TOP_K):
                    _row_copy(y_hbm, buf.at[s, k], sem.at[s], p_ref[r * TOP_K + k], r).start(priority=k % 2)
            return carry

        lax.fori_loop(0, rows // ROW_UNROLL, body, 0)

    @pl.when(i == 0)
    def _():
        issue(pos_ref, 0)

    @pl.when(i + 1 < n)
    def _():
        issue(pos_next_ref, 1 - slot)

    for k in range(TOP_K):
        _rows_wait(y_hbm, buf.at[slot, k], sem.at[slot], rows)
    gates = gate_ref[...]
    y = gates[:, 0:1] * buf[slot, 0]
    for k in range(1, TOP_K):
        y = y + gates[:, k:k + 1] * buf[slot, k]
    o_ref[...] = _ln_rows(DN_ALPHA * x_ref[...] + y, g_ref[...], b_ref[...])


def _combine(x, gates, pos_flat, y_buf, g, b):
    T, D = x.shape
    R = COMBINE_TILE
    n = T // R
    return pl.pallas_call(
        functools.partial(_combine_kernel, rows=R),
        grid=(n,),
        in_specs=[pl.BlockSpec((R * TOP_K,), lambda i: (i,), memory_space=pltpu.SMEM),
                  pl.BlockSpec((R * TOP_K,), lambda i: (jnp.minimum(i + 1, n - 1),), memory_space=pltpu.SMEM),
                  pl.BlockSpec((R, D), lambda i: (i, 0)),
                  pl.BlockSpec((R, LANES), lambda i: (i, 0)),
                  pl.BlockSpec(memory_space=pl.ANY),
                  _full(g.shape), _full(b.shape)],
        out_specs=pl.BlockSpec((R, D), lambda i: (i, 0)),
        out_shape=jax.ShapeDtypeStruct((T, D), F32),
        scratch_shapes=[pltpu.VMEM((2, TOP_K, R, D), F32), pltpu.SemaphoreType.DMA((2,))],
        compiler_params=_params(0, 1),
        name="moe_combine_ln",
    )(pos_flat, pos_flat, x, gates, y_buf, g, b)


def _routing_plan(idx, counts, n_tokens):
    tm = EXPERT_TILE
    e = idx[:, :TOP_K]
    rank = idx[:, TOP_K:2 * TOP_K]
    counts = counts[0, :N_EXPERTS].astype(jnp.int32)
    padded = (counts + tm - 1) // tm * tm
    pend = jnp.cumsum(padded)
    pstart = pend - padded
    pos = (pstart[e] + rank).astype(jnp.int32)
    n_slots = n_tokens * TOP_K + N_EXPERTS * tm
    blk_start = jnp.arange(n_slots // tm, dtype=jnp.int32) * tm
    blk_e = jnp.minimum(jnp.sum(blk_start[:, None] >= pend[None, :], axis=1), N_EXPERTS - 1).astype(jnp.int32)
    blk_valid = jnp.clip(pstart[blk_e] + counts[blk_e] - blk_start, 0, tm).astype(jnp.int32)
    return pos.reshape(-1), blk_e, blk_valid, n_slots


def _moe(x1, idx, gates, counts, layer, wgu, bgu, wdn, bdn, ln_g, ln_b):
    T, D = x1.shape
    pos_flat, blk_e, blk_valid, n_slots = _routing_plan(idx, counts, T)
    x_buf = _dispatch(x1, pos_flat, n_slots)
    y_buf = _expert_mlp(x_buf, blk_e, blk_valid, layer, wgu, bgu[:, :, None, :], wdn, bdn[:, :, None, :])
    return _combine(x1, gates, pos_flat, y_buf, ln_g.reshape(1, D), ln_b.reshape(1, D))


def _trunk(xs, trunks, p):
    T, D = xs.shape
    x = _embed_ln(xs, p["emb_ln_g"], p["emb_ln_b"])
    mc = jnp.asarray(_fnet_channel_np(), BF16)
    for l in range(DEPTH):
        w_in = p["w_in"][l]
        wa = w_in[:, :SSD_WIDTH + SSD_XBC].astype(BF16)
        wdt = jnp.pad(w_in[:, SSD_WIDTH + SSD_XBC:SSD_IN], ((0, 0), (0, LANES - 2 * SSD_HEADS))).astype(BF16)
        wf = w_in[:, SSD_IN:SSD_IN + FNET_WIDTH].astype(BF16)
        wuv = w_in[:, SSD_IN + FNET_WIDTH:].astype(BF16)
        z, xbc, dt, fr, fi, uv = _inproj(x, wa, wdt, wf, wuv, mc)

        lane_pad = LANES - 2 * SSD_HEADS
        dtb = jnp.pad(p["dt_bias"][l].reshape(1, -1), ((0, 0), (0, lane_pad)))
        a_neg = jnp.pad(-jnp.exp(p["a_log"][l].astype(F32)).reshape(1, -1), ((0, 0), (0, lane_pad)))
        dskip_w = jnp.repeat(p["d_skip"][l], SSD_HEAD_DIM).reshape(1, SSD_WIDTH)
        normg = p["ssd_norm_g"][l].reshape(1, SSD_WIDTH)
        convw = p["conv_w"][l]
        convb = p["conv_b"][l].reshape(1, SSD_XBC)
        w_bd = jax.scipy.linalg.block_diag(*[p["fnet_w"][l, g] for g in range(FNET_GROUPS)]).astype(BF16)
        b_f = p["fnet_b"][l].reshape(1, FNET_WIDTH)

        ssd_parts, fn_parts = [], []
        row0 = 0
        for (B, L) in trunks:
            ssd_parts.append(_ssd_mixer(z, xbc, dt, row0, B, L, convw, convb, dtb, a_neg, dskip_w, normg))
            fn_parts.append(_fourier_mixer(fr, fi, row0, B, L, w_bd, b_f))
            row0 += B * L
        ssd = jnp.concatenate(ssd_parts, axis=0)
        fn = jnp.concatenate(fn_parts, axis=0)

        ws_cat = jnp.concatenate([p["gmlp_ws"][l, h] for h in range(GMLP_HEADS)], axis=1).astype(BF16)
        bs_wide = jnp.repeat(p["gmlp_bs"][l].T, GMLP_HDIM, axis=1)
        gm = _gmlp_mixer(uv, p["gmlp_ln_g"][l].reshape(1, -1), p["gmlp_ln_b"][l].reshape(1, -1),
                         ws_cat, bs_wide)

        w_r = jnp.pad(p["router_w"][l], ((0, 0), (0, LANES - N_EXPERTS)))
        wrh = w_r.astype(BF16)
        wrl = (w_r - wrh.astype(F32)).astype(BF16)
        b_r = jnp.pad(p["router_b"][l].reshape(1, -1), ((0, 0), (0, LANES - N_EXPERTS)),
                      constant_values=NEG_BIG)
        x1, idx, gates, counts = _outproj(
            x, ssd, fn, gm, p["w_out"][l].astype(BF16), p["b_out"][l].reshape(1, D),
            p["ln1_g"][l].reshape(1, D), p["ln1_b"][l].reshape(1, D), wrh, wrl, b_r)

        x = _moe(x1, idx, gates, counts, l, p["exp_w_gu"], p["exp_b_gu"], p["exp_w_dn"], p["exp_b_dn"],
                 p["ln2_g"][l], p["ln2_b"][l])
    return x


def kernel(x_prompt, x_sample, emb_ln_g, emb_ln_b, w_in, conv_w, conv_b, dt_bias, a_log, d_skip, ssd_norm_g, fnet_w, fnet_b, gmlp_ln_g, gmlp_ln_b, gmlp_ws, gmlp_bs, w_out, b_out, ln1_g, ln1_b, router_w, router_b, exp_w_gu, exp_b_gu, exp_w_dn, exp_b_dn, ln2_g, ln2_b):
    p = dict(emb_ln_g=emb_ln_g, emb_ln_b=emb_ln_b, w_in=w_in, conv_w=conv_w, conv_b=conv_b,
             dt_bias=dt_bias, a_log=a_log, d_skip=d_skip, ssd_norm_g=ssd_norm_g, fnet_w=fnet_w,
             fnet_b=fnet_b, gmlp_ln_g=gmlp_ln_g, gmlp_ln_b=gmlp_ln_b, gmlp_ws=gmlp_ws, gmlp_bs=gmlp_bs,
             w_out=w_out, b_out=b_out, ln1_g=ln1_g, ln1_b=ln1_b, router_w=router_w, router_b=router_b,
             exp_w_gu=exp_w_gu, exp_b_gu=exp_b_gu, exp_w_dn=exp_w_dn, exp_b_dn=exp_b_dn,
             ln2_g=ln2_g, ln2_b=ln2_b)
    D = x_prompt.shape[-1]
    trunks = [x_prompt.shape[:2], x_sample.shape[:2]]
    xs = jnp.concatenate([x_prompt.reshape(-1, D), x_sample.reshape(-1, D)], axis=0)
    y = _trunk(xs, trunks, p)
    n_p = x_prompt.shape[0] * x_prompt.shape[1]
    return (y[:n_p].reshape(x_prompt.shape), y[n_p:].reshape(x_sample.shape))
```

```python
import functools
import math

import numpy as np
import jax
import jax.numpy as jnp
from jax import lax
from jax.experimental import pallas as pl
from jax.experimental.pallas import tpu as pltpu

F32 = jnp.float32
BF16 = jnp.bfloat16

D_MODEL = 1024
DEPTH = 4
SSD_WIDTH = 512
SSD_HEAD_DIM = 64
SSD_HEADS = 8
SSD_GROUPS = 2
SSD_STATE = 64
SSD_CONV = 5
SSD_GN = SSD_GROUPS * SSD_STATE
SSD_XBC = SSD_WIDTH + 2 * SSD_GN
SSD_IN = SSD_WIDTH + SSD_XBC + 2 * SSD_HEADS
FNET_WIDTH = 256
FNET_GROUPS = 4
FNET_GDIM = 64
GMLP_WIDTH = 256
GMLP_HEADS = 4
GMLP_HDIM = 64
N_EXPERTS = 32
TOP_K = 4
D_EXPERT = D_MODEL
SWIGLU_LIMIT = 7.0
SWIGLU_ALPHA = 1.702
LN_EPS = 1e-5
RMS_EPS = 1e-5
DN_ALPHA = (2 * DEPTH) ** 0.25

LANES = 128
CHUNK = 128
HALO = 16
VMEM_LIMIT = 56 * 1024 * 1024

TOK_TILE = 512
EXPERT_TILE = 512
EXPERT_COL_CHUNK = 512
DISPATCH_TILE = 512
COMBINE_TILE = 256
COMBINE_SUB = 32
COMBINE_LAG = 2
ROW_UNROLL = 8
SSD_STEP_CHUNKS = 4
FNET_ROW_GROUP = 16
ROUTER_SUB = 512
NEG_BIG = -1e30


def _dot(a, b):
    return jnp.dot(a, b, preferred_element_type=F32)


def _split2(v):
    hi = v.astype(BF16)
    lo = (v - hi.astype(F32)).astype(BF16)
    return hi, lo


def _split3(v):
    hi = v.astype(BF16)
    r = v - hi.astype(F32)
    mid = r.astype(BF16)
    lo = (r - mid.astype(F32)).astype(BF16)
    return hi, mid, lo


def _ln_rows(xf, g, b):
    mu = jnp.mean(xf, -1, keepdims=True)
    xc = xf - mu
    var = jnp.mean(xc * xc, -1, keepdims=True)
    return xc * lax.rsqrt(var + LN_EPS) * g + b


def _params(n_parallel=1, n_arbitrary=0):
    sem = ("parallel",) * n_parallel + ("arbitrary",) * n_arbitrary
    return pltpu.CompilerParams(dimension_semantics=sem, vmem_limit_bytes=VMEM_LIMIT)


def _full(shape):
    nd = len(shape)
    return pl.BlockSpec(shape, lambda *_: (0,) * nd)


def _embed_ln_kernel(x_ref, g_ref, b_ref, o_ref):
    o_ref[...] = _ln_rows(x_ref[...], g_ref[...], b_ref[...])


def _embed_ln(x, g, b):
    T, D = x.shape
    return pl.pallas_call(
        _embed_ln_kernel,
        grid=(T // TOK_TILE,),
        in_specs=[pl.BlockSpec((TOK_TILE, D), lambda i: (i, 0)), _full((1, D)), _full((1, D))],
        out_specs=pl.BlockSpec((TOK_TILE, D), lambda i: (i, 0)),
        out_shape=jax.ShapeDtypeStruct((T, D), F32),
        compiler_params=_params(1),
        name="embed_ln",
    )(x, g.reshape(1, D), b.reshape(1, D))


def _inproj_kernel(x_ref, wa_ref, wdt_ref, wf_ref, wuv_ref, mc_ref,
                   z_ref, xbc_ref, dt_ref, fr_ref, fi_ref, uv_ref):
    xb = x_ref[...].astype(BF16)
    a = _dot(xb, wa_ref[...])
    z_ref[...] = a[:, :SSD_WIDTH].astype(BF16)
    xbc_ref[...] = a[:, SSD_WIDTH:].astype(BF16)
    dt_ref[...] = _dot(xb, wdt_ref[...])
    f = _dot(xb, wf_ref[...]).astype(BF16)
    fri = _dot(f, mc_ref[...])
    fr_ref[...] = fri[:, :FNET_WIDTH]
    fi_ref[...] = fri[:, FNET_WIDTH:]
    uv_ref[...] = _dot(xb, wuv_ref[...]).astype(BF16)


def _inproj(x, wa, wdt, wf, wuv, mc):
    T, D = x.shape
    tm = TOK_TILE
    row = lambda w: pl.BlockSpec((tm, w), lambda i: (i, 0))
    return pl.pallas_call(
        _inproj_kernel,
        grid=(T // tm,),
        in_specs=[row(D), _full(wa.shape), _full(wdt.shape), _full(wf.shape), _full(wuv.shape),
                  _full(mc.shape)],
        out_specs=[row(SSD_WIDTH), row(SSD_XBC), row(LANES), row(FNET_WIDTH), row(FNET_WIDTH),
                   row(2 * GMLP_WIDTH)],
        out_shape=[jax.ShapeDtypeStruct((T, SSD_WIDTH), BF16),
                   jax.ShapeDtypeStruct((T, SSD_XBC), BF16),
                   jax.ShapeDtypeStruct((T, LANES), F32),
                   jax.ShapeDtypeStruct((T, FNET_WIDTH), F32),
                   jax.ShapeDtypeStruct((T, FNET_WIDTH), F32),
                   jax.ShapeDtypeStruct((T, 2 * GMLP_WIDTH), BF16)],
        compiler_params=_params(1),
        name="inproj",
    )(x, wa, wdt, wf, wuv, mc)


def _conv_silu(xc_ref, xp_ref, xn_ref, convw_ref, convb_ref, first, last):
    Q = xc_ref.shape[0]
    cur = xc_ref[...].astype(F32)
    prev = jnp.where(first, 0.0, xp_ref[...].astype(F32))
    nxt = jnp.where(last, 0.0, xn_ref[...].astype(F32))
    ext = jnp.concatenate([prev[HALO - 8:], cur, nxt[:8]], axis=0)
    n_ext = Q + 16
    pad = SSD_CONV // 2
    conv = convb_ref[...]
    for k in range(SSD_CONV):
        shift = (pad - k) % n_ext
        tap = ext if shift == 0 else pltpu.roll(ext, shift, axis=0)
        conv = conv + tap[8:8 + Q] * convw_ref[k:k + 1, :]
    return conv * jax.nn.sigmoid(conv)


def _ssd_chunk(act, dt_raw, dtb_ref, a_ref, cum_ref, expand_ref, state_ref, *, backward):
    Q = CHUNK
    h_off = SSD_HEADS if backward else 0
    xs = act[:, :SSD_WIDTH]
    bm = act[:, SSD_WIDTH:SSD_WIDTH + SSD_GN]
    cm = act[:, SSD_WIDTH + SSD_GN:]

    dt = jax.nn.softplus(dt_raw + dtb_ref[...])
    adt = dt * a_ref[...]
    cum = cum_ref[...]
    h3 = _split3(adt)
    acum = _dot(cum, h3[0]) + _dot(cum, h3[1]) + _dot(cum, h3[2])
    acum_t = acum.T
    total = acum[0:1, :] if backward else acum[Q - 1:Q, :]
    dec_in = jnp.exp(acum)
    dec_end = jnp.exp(total - acum)

    expand = expand_ref[...]

    def widen(v):
        hi, lo = _split2(v)
        return _dot(hi, expand) + _dot(lo, expand)

    dt_w = widen(dt)
    dec_in_w = widen(dec_in)
    dec_end_w = widen(dec_end)
    tdec_w = dec_in_w[0:1, :] if backward else dec_in_w[Q - 1:Q, :]

    xdt = xs * dt_w

    row = lax.broadcasted_iota(jnp.int32, (Q, Q), 0)
    col = lax.broadcasted_iota(jnp.int32, (Q, Q), 1)
    mask = (col > row) if backward else (col <= row)
    lane = lax.broadcasted_iota(jnp.int32, (Q, LANES), 1)
    lo_half = lane < SSD_STATE

    bm_b = bm.astype(BF16)
    cb = []
    for g in range(SSD_GROUPS):
        cm_g = jnp.where(lo_half if g == 0 else jnp.logical_not(lo_half), cm, 0.0).astype(BF16)
        cb.append(lax.dot_general(cm_g, bm_b, (((1,), (1,)), ((), ())),
                                  preferred_element_type=F32))

    y_parts = []
    for pair in range(SSD_HEADS // 2):
        gmats = []
        for r in range(2):
            h = 2 * pair + r
            hl = h_off + h
            diff = acum[:, hl:hl + 1] - acum_t[hl:hl + 1, :]
            dec = jnp.exp(jnp.where(mask, diff, NEG_BIG))
            gmats.append((cb[h // (SSD_HEADS // SSD_GROUPS)] * dec).astype(BF16))
        lhs = jnp.concatenate(gmats, axis=1)
        x2 = xdt[:, pair * LANES:(pair + 1) * LANES]
        rhs = jnp.concatenate([jnp.where(lo_half, x2, 0.0), jnp.where(lo_half, 0.0, x2)],
                              axis=0).astype(BF16)
        y_parts.append(_dot(lhs, rhs))
    y_diag = jnp.concatenate(y_parts, axis=1)

    srow = lax.broadcasted_iota(jnp.int32, (SSD_GN, SSD_WIDTH), 0)
    scol = lax.broadcasted_iota(jnp.int32, (SSD_GN, SSD_WIDTH), 1)
    on_group = (srow // SSD_STATE) == (scol // (SSD_WIDTH // SSD_GROUPS))
    st_new = _dot(bm.T.astype(BF16), (xdt * dec_end_w).astype(BF16))
    st_new = jnp.where(on_group, st_new, 0.0)
    st_prev = state_ref[...]
    y_off = _dot(cm.astype(BF16), st_prev.astype(BF16)) * dec_in_w
    state_ref[...] = st_prev * tdec_w + st_new
    return y_diag + y_off, xs


def _ssd_bwd_kernel(xc_ref, xp_ref, xn_ref, dt_ref, convw_ref, convb_ref, dtb_ref, a_ref,
                    cum_ref, expand_ref, yb_ref, act_ref, state_ref, *, n_steps):
    c = pl.program_id(1)
    step = n_steps - 1 - c

    @pl.when(c == 0)
    def _():
        state_ref[...] = jnp.zeros_like(state_ref)

    act = _conv_silu(xc_ref, xp_ref, xn_ref, convw_ref, convb_ref,
                     first=step == 0, last=step == n_steps - 1)
    act_ref[...] = act.astype(BF16)
    for s in reversed(range(SSD_STEP_CHUNKS)):
        rows = slice(s * CHUNK, (s + 1) * CHUNK)
        y, _ = _ssd_chunk(act[rows], dt_ref[rows, :], dtb_ref, a_ref, cum_ref, expand_ref, state_ref,
                          backward=True)
        yb_ref[rows, :] = y


def _ssd_fwd_kernel(act_ref, dt_ref, z_ref, yb_ref, dtb_ref, a_ref, cum_ref, expand_ref,
                    dskip_ref, normg_ref, o_ref, state_ref):
    c = pl.program_id(1)

    @pl.when(c == 0)
    def _():
        state_ref[...] = jnp.zeros_like(state_ref)

    gw = SSD_WIDTH // SSD_GROUPS
    for s in range(SSD_STEP_CHUNKS):
        rows = slice(s * CHUNK, (s + 1) * CHUNK)
        y, xs = _ssd_chunk(act_ref[rows, :].astype(F32), dt_ref[rows, :], dtb_ref, a_ref, cum_ref,
                           expand_ref, state_ref, backward=False)
        z = z_ref[rows, :].astype(F32)
        y = (y + yb_ref[rows, :] + xs * dskip_ref[...]) * (z * jax.nn.sigmoid(z))
        outs = []
        for g in range(SSD_GROUPS):
            yg = y[:, g * gw:(g + 1) * gw]
            outs.append(yg * lax.rsqrt(jnp.mean(yg * yg, -1, keepdims=True) + RMS_EPS))
        o_ref[rows, :] = (jnp.concatenate(outs, axis=1) * normg_ref[...]).astype(BF16)


def _ssd_mixer(z, xbc, dt, row0, B, L, convw, convb, dtb, a_neg, dskip_w, normg):
    R = SSD_STEP_CHUNKS * CHUNK
    C = L // R
    c0 = row0 // R
    hpc = R // HALO
    n_halo = z.shape[0] // HALO
    consts = _ssd_consts()

    def specs(chunk_of):
        cur = lambda w: pl.BlockSpec((R, w), lambda b, c: (c0 + b * C + chunk_of(c), 0))
        prv = pl.BlockSpec((HALO, SSD_XBC),
                           lambda b, c: (jnp.maximum((c0 + b * C + chunk_of(c)) * hpc - 1, 0), 0))
        nxt = pl.BlockSpec((HALO, SSD_XBC),
                           lambda b, c: (jnp.minimum((c0 + b * C + chunk_of(c) + 1) * hpc, n_halo - 1), 0))
        return cur, prv, nxt

    out_row = lambda chunk_of, w: pl.BlockSpec((R, w), lambda b, c: (b * C + chunk_of(c), 0))

    rev = lambda c: C - 1 - c
    cur, prv, nxt = specs(rev)
    yb, act = pl.pallas_call(
        functools.partial(_ssd_bwd_kernel, n_steps=C),
        grid=(B, C),
        in_specs=[cur(SSD_XBC), prv, nxt, cur(LANES), _full(convw.shape), _full(convb.shape),
                  _full(dtb.shape), _full(a_neg.shape),
                  _full(consts["cum_b"].shape), _full(consts["expand_b"].shape)],
        out_specs=[out_row(rev, SSD_WIDTH), out_row(rev, SSD_XBC)],
        out_shape=[jax.ShapeDtypeStruct((B * L, SSD_WIDTH), F32),
                   jax.ShapeDtypeStruct((B * L, SSD_XBC), BF16)],
        scratch_shapes=[pltpu.VMEM((SSD_GN, SSD_WIDTH), F32)],
        compiler_params=_params(1, 1),
        name="ssd_bwd",
    )(xbc, xbc, xbc, dt, convw, convb, dtb, a_neg, consts["cum_b"], consts["expand_b"])

    fwd = lambda c: c
    cur, _, _ = specs(fwd)
    return pl.pallas_call(
        _ssd_fwd_kernel,
        grid=(B, C),
        in_specs=[out_row(fwd, SSD_XBC), cur(LANES), cur(SSD_WIDTH), out_row(fwd, SSD_WIDTH),
                  _full(dtb.shape), _full(a_neg.shape),
                  _full(consts["cum_f"].shape), _full(consts["expand_f"].shape),
                  _full(dskip_w.shape), _full(normg.shape)],
        out_specs=out_row(fwd, SSD_WIDTH),
        out_shape=jax.ShapeDtypeStruct((B * L, SSD_WIDTH), BF16),
        scratch_shapes=[pltpu.VMEM((SSD_GN, SSD_WIDTH), F32)],
        compiler_params=_params(1, 1),
        name="ssd_fwd",
    )(act, dt, z, yb, dtb, a_neg, consts["cum_f"], consts["expand_f"], dskip_w, normg)


@functools.lru_cache(maxsize=None)
def _ssd_consts_np():
    q = np.arange(CHUNK)
    cum_f = (q[None, :] <= q[:, None]).astype(np.float32)
    cum_b = (q[None, :] >= q[:, None]).astype(np.float32)
    out = {"cum_f": cum_f, "cum_b": cum_b}
    for name, off in (("expand_f", 0), ("expand_b", SSD_HEADS)):
        e = np.zeros((LANES, SSD_WIDTH), np.float32)
        for h in range(SSD_HEADS):
            e[off + h, h * SSD_HEAD_DIM:(h + 1) * SSD_HEAD_DIM] = 1.0
        out[name] = e
    return out


def _ssd_consts():
    return {k: jnp.asarray(v, BF16) for k, v in _ssd_consts_np().items()}


@functools.lru_cache(maxsize=None)
def _fnet_tables_np(L):
    L2 = CHUNK
    L1 = L // L2
    k1 = np.arange(L1)
    ang1 = 2.0 * np.pi * ((k1[:, None] * k1[None, :]) % L1) / L1
    f1 = np.concatenate([np.cos(ang1), np.sin(ang1)], axis=0) / math.sqrt(L1)
    k = k1[:, None, None] + L1 * np.arange(L2)[None, :, None]
    l2 = np.arange(L2)[None, None, :]
    ang2 = 2.0 * np.pi * ((k * l2) % L) / L
    gc = np.cos(ang2) / math.sqrt(L2)
    gs = np.sin(ang2) / math.sqrt(L2)
    return f1.astype(np.float32), gc.astype(np.float32), gs.astype(np.float32)


@functools.lru_cache(maxsize=None)
def _fnet_channel_np():
    c = np.arange(FNET_GDIM)
    ang = 2.0 * np.pi * ((c[:, None] * c[None, :]) % FNET_GDIM) / FNET_GDIM
    cc = np.cos(ang) / math.sqrt(FNET_GDIM)
    sc = np.sin(ang) / math.sqrt(FNET_GDIM)
    eye = np.eye(FNET_GROUPS)
    return np.concatenate([np.kron(eye, cc), np.kron(eye, -sc)], axis=1).astype(np.float32)


def _fnet_stage1_kernel(f_ref, xr_ref, xi_ref, yr_ref, yi_ref, *, L1, group):
    W = FNET_WIDTH
    f = f_ref[...]
    for j in range(group):
        x = jnp.concatenate([xr_ref[:, j, :], xi_ref[:, j, :]], axis=1).astype(BF16)
        p = _dot(f, x)
        yr_ref[:, j, :] = p[:L1, :W] + p[L1:, W:]
        yi_ref[:, j, :] = p[:L1, W:] - p[L1:, :W]


def _fnet_stage2_kernel(yr_ref, yi_ref, g_ref, w_ref, b_ref, o_ref, *, group):
    for j in range(group):
        y = jnp.concatenate([yr_ref[j], yi_ref[j]], axis=0).astype(BF16)
        zr = _dot(g_ref[j], y)
        o_ref[:, j, :] = _dot(zr.astype(BF16), w_ref[...]) + b_ref[...]


def _fourier_mixer(fr, fi, row0, B, L, w_bd, b_f):
    L2 = CHUNK
    L1 = L // L2
    W = FNET_WIDTH
    f1, gc, gs = _fnet_tables_np(L)
    f1 = jnp.asarray(f1, BF16)
    g_cat = jnp.asarray(np.concatenate([gc, gs], axis=2), BF16)
    xr = fr.reshape(-1, L2, W)
    xi = fi.reshape(-1, L2, W)
    blk0 = row0 // L2 // L1
    grp1 = FNET_ROW_GROUP
    xblk = pl.BlockSpec((L1, grp1, W), lambda b, j: (blk0 + b, j, 0))
    yblk = pl.BlockSpec((L1, grp1, W), lambda b, j: (b, j, 0))
    yr, yi = pl.pallas_call(
        functools.partial(_fnet_stage1_kernel, L1=L1, group=grp1),
        grid=(B, L2 // grp1),
        in_specs=[_full(f1.shape), xblk, xblk],
        out_specs=[yblk, yblk],
        out_shape=[jax.ShapeDtypeStruct((B * L1, L2, W), F32)] * 2,
        compiler_params=_params(2),
        name="fnet_stage1",
    )(f1, xr, xi)
    grp2 = FNET_ROW_GROUP // 2
    n2 = L1 // grp2
    yblk2 = pl.BlockSpec((grp2, L2, W), lambda b, k: (b * n2 + k, 0, 0))
    gblk = pl.BlockSpec((grp2, L2, 2 * L2), lambda b, k: (k, 0, 0))
    out = pl.pallas_call(
        functools.partial(_fnet_stage2_kernel, group=grp2),
        grid=(B, n2),
        in_specs=[yblk2, yblk2, gblk, _full(w_bd.shape), _full(b_f.shape)],
        out_specs=pl.BlockSpec((L2, grp2, W), lambda b, k: (b, k, 0)),
        out_shape=jax.ShapeDtypeStruct((B * L2, L1, W), F32),
        compiler_params=_params(2),
        name="fnet_stage2",
    )(yr, yi, g_cat, w_bd, b_f)
    return out.reshape(B * L, W)


def _gmlp_kernel(uv_ref, g_ref, b_ref, ws_ref, bs_ref, o_ref, *, n_chunks):
    uv = uv_ref[...].astype(F32)
    ge = 0.5 * uv * (1.0 + lax.erf(uv * (1.0 / math.sqrt(2.0))))
    u = ge[:, :GMLP_WIDTH]
    v = _ln_rows(ge[:, GMLP_WIDTH:], g_ref[...], b_ref[...])
    lane = lax.broadcasted_iota(jnp.int32, (CHUNK, GMLP_WIDTH), 1)
    ws = ws_ref[...]
    bs = bs_ref[...]
    for j in range(n_chunks):
        vj = v[j * CHUNK:(j + 1) * CHUNK]
        rhs = jnp.concatenate(
            [jnp.where(lane // GMLP_HDIM == h, vj, 0.0) for h in range(GMLP_HEADS)], axis=0)
        sv = _dot(ws, rhs.astype(BF16)) + bs
        o_ref[j * CHUNK:(j + 1) * CHUNK, :] = (u[j * CHUNK:(j + 1) * CHUNK] * sv).astype(BF16)


def _gmlp_mixer(uv, ln_g, ln_b, ws_cat, bs_wide):
    T = uv.shape[0]
    tm = TOK_TILE
    return pl.pallas_call(
        functools.partial(_gmlp_kernel, n_chunks=tm // CHUNK),
        grid=(T // tm,),
        in_specs=[pl.BlockSpec((tm, 2 * GMLP_WIDTH), lambda i: (i, 0)), _full(ln_g.shape),
                  _full(ln_b.shape), _full(ws_cat.shape), _full(bs_wide.shape)],
        out_specs=pl.BlockSpec((tm, GMLP_WIDTH), lambda i: (i, 0)),
        out_shape=jax.ShapeDtypeStruct((T, GMLP_WIDTH), BF16),
        compiler_params=_params(1),
        name="gmlp",
    )(uv, ln_g, ln_b, ws_cat, bs_wide)


def _outproj_kernel(x_ref, ssd_ref, fn_ref, gm_ref, wo_ref, bo_ref, g_ref, b_ref,
                    wrh_ref, wrl_ref, br_ref, tri_ref, x1_ref, idx_ref, gate_ref, cnt_ref, seen_ref):
    @pl.when(pl.program_id(0) == 0)
    def _():
        seen_ref[...] = jnp.zeros_like(seen_ref)

    seen = seen_ref[...]
    wrh = wrh_ref[...]
    for s in range(x_ref.shape[0] // ROUTER_SUB):
        rows = slice(s * ROUTER_SUB, (s + 1) * ROUTER_SUB)
        acc = _dot(ssd_ref[rows, :], wo_ref[0:SSD_WIDTH, :])
        acc = acc + _dot(fn_ref[rows, :].astype(BF16), wo_ref[SSD_WIDTH:SSD_WIDTH + FNET_WIDTH, :])
        acc = acc + _dot(gm_ref[rows, :], wo_ref[SSD_WIDTH + FNET_WIDTH:, :])
        h = DN_ALPHA * x_ref[rows, :] + acc + bo_ref[...]
        x1 = _ln_rows(h, g_ref[...], b_ref[...])
        x1_ref[rows, :] = x1

        xh, xl = _split2(x1)
        logits = _dot(xh, wrh) + _dot(xl, wrh) + _dot(xh, wrl_ref[...]) + br_ref[...]
        lane = lax.broadcasted_iota(jnp.int32, logits.shape, 1)
        work = logits
        vals, idxs = [], []
        for _ in range(TOP_K):
            m = jnp.max(work, axis=-1, keepdims=True)
            i = jnp.min(jnp.where(work == m, lane, LANES), axis=-1, keepdims=True)
            vals.append(m)
            idxs.append(i)
            work = jnp.where(lane == i, -jnp.inf, work)
        exps = [jnp.exp(v - vals[0]) for v in vals]
        denom = exps[0]
        for e in exps[1:]:
            denom = denom + e

        onehots = [lane == i for i in idxs]
        chosen = jnp.zeros(logits.shape, F32)
        for oh in onehots:
            chosen = jnp.where(oh, 1.0, chosen)
        before = _dot(tri_ref[...], chosen.astype(BF16)) + seen
        seen = seen + jnp.sum(chosen, axis=0, keepdims=True)

        idx_out = jnp.zeros(logits.shape, jnp.int32)
        gate_out = jnp.zeros(logits.shape, F32)
        for k in range(TOP_K):
            rank = jnp.sum(jnp.where(onehots[k], before, 0.0), axis=-1, keepdims=True).astype(jnp.int32)
            idx_out = jnp.where(lane == k, idxs[k], idx_out)
            idx_out = jnp.where(lane == TOP_K + k, rank, idx_out)
            gate_out = jnp.where(lane == k, exps[k] / denom, gate_out)
        idx_ref[rows, :] = idx_out
        gate_ref[rows, :] = gate_out
    seen_ref[...] = seen
    cnt_ref[...] = seen


def _outproj(x, ssd, fn, gm, wo, bo, g, b, wrh, wrl, br):
    T, D = x.shape
    tm = TOK_TILE
    row = lambda w: pl.BlockSpec((tm, w), lambda i: (i, 0))
    q = np.arange(ROUTER_SUB)
    tri = jnp.asarray((q[None, :] < q[:, None]).astype(np.float32), BF16)
    return pl.pallas_call(
        _outproj_kernel,
        grid=(T // tm,),
        in_specs=[row(D), row(SSD_WIDTH), row(FNET_WIDTH), row(GMLP_WIDTH), _full(wo.shape),
                  _full(bo.shape), _full(g.shape), _full(b.shape), _full(wrh.shape),
                  _full(wrl.shape), _full(br.shape), _full(tri.shape)],
        out_specs=[row(D), row(LANES), row(LANES), _full((1, LANES))],
        out_shape=[jax.ShapeDtypeStruct((T, D), F32),
                   jax.ShapeDtypeStruct((T, LANES), jnp.int32),
                   jax.ShapeDtypeStruct((T, LANES), F32),
                   jax.ShapeDtypeStruct((1, LANES), F32)],
        scratch_shapes=[pltpu.VMEM((1, LANES), F32)],
        compiler_params=_params(0, 1),
        name="outproj_ln_router",
    )(x, ssd, fn, gm, wo, bo, g, b, wrh, wrl, br, tri)


SUB = D_MODEL // LANES


def _to_slot_major(dst_ref, dst_row0, x, n):
    for c in range(SUB):
        dst_ref[pl.ds(dst_row0 + c, n, stride=SUB), :] = x[:, c * LANES:(c + 1) * LANES]


def _from_slot_major(src_ref, src_row0, n):
    return jnp.concatenate([src_ref[pl.ds(src_row0 + c, n, stride=SUB), :] for c in range(SUB)], axis=1)


def _slot_copy(src, dst, sem, src_slot, dst_slot):
    s0 = pl.multiple_of(src_slot * SUB, SUB)
    d0 = pl.multiple_of(dst_slot * SUB, SUB)
    return pltpu.make_async_copy(src.at[pl.ds(s0, SUB), :], dst.at[pl.ds(d0, SUB), :], sem)


def _slots_wait(src, dst, sem, n_slots):
    pltpu.make_async_copy(src.at[pl.ds(0, n_slots * SUB), :], dst.at[pl.ds(0, n_slots * SUB), :], sem).wait()


def _dispatch_kernel(pos_ref, x_ref, xbuf_hbm, xs_ref, sem, *, rows):
    _to_slot_major(xs_ref, 0, x_ref[...], rows)

    def body(j, carry):
        for u in range(ROW_UNROLL):
            r = j * ROW_UNROLL + u
            for k in range(TOP_K):
                _slot_copy(xs_ref, xbuf_hbm, sem, r, pos_ref[r * TOP_K + k]).start(priority=k % 2)
        return carry

    lax.fori_loop(0, rows // ROW_UNROLL, body, 0)
    for _ in range(TOP_K):
        _slots_wait(xs_ref, xbuf_hbm, sem, rows)


def _dispatch(x, pos_flat, n_slots):
    T, D = x.shape
    R = DISPATCH_TILE
    return pl.pallas_call(
        functools.partial(_dispatch_kernel, rows=R),
        grid=(T // R,),
        in_specs=[pl.BlockSpec((R * TOP_K,), lambda i: (i,), memory_space=pltpu.SMEM),
                  pl.BlockSpec((R, D), lambda i: (i, 0))],
        out_specs=pl.BlockSpec(memory_space=pl.ANY),
        out_shape=jax.ShapeDtypeStruct((n_slots * SUB, LANES), F32),
        scratch_shapes=[pltpu.VMEM((R * SUB, LANES), F32), pltpu.SemaphoreType.DMA(())],
        compiler_params=_params(1),
        name="moe_dispatch",
    )(pos_flat, x)


def _expert_kernel(blk_e_ref, valid_ref, x_ref, wgu_ref, bgu_ref, wdn_ref, bdn_ref, y_ref, wgu_b, wdn_b):
    i = pl.program_id(0)
    valid = valid_ref[i]
    prev_e = blk_e_ref[jnp.maximum(i - 1, 0)]
    new_expert = jnp.logical_or(i == 0, blk_e_ref[i] != prev_e)

    @pl.when(jnp.logical_and(valid > 0, new_expert))
    def _():
        wgu_b[...] = wgu_ref[...].astype(BF16)
        wdn_b[...] = wdn_ref[...].astype(BF16)

    @pl.when(valid > 0)
    def _():
        x = _from_slot_major(x_ref, 0, EXPERT_TILE)
        row = lax.broadcasted_iota(jnp.int32, x.shape, 0)
        xb = jnp.where(row < valid, x, 0.0).astype(BF16)
        acc = jnp.broadcast_to(bdn_ref[...], x.shape)
        nc = EXPERT_COL_CHUNK
        for j in range(D_EXPERT // nc):
            g = _dot(xb, wgu_b[:, j * nc:(j + 1) * nc]) + bgu_ref[:, j * nc:(j + 1) * nc]
            u = (_dot(xb, wgu_b[:, D_EXPERT + j * nc:D_EXPERT + (j + 1) * nc])
                 + bgu_ref[:, D_EXPERT + j * nc:D_EXPERT + (j + 1) * nc])
            g = jnp.minimum(g, SWIGLU_LIMIT)
            u = jnp.clip(u, -SWIGLU_LIMIT, SWIGLU_LIMIT)
            act = (u + 1.0) * (g * jax.nn.sigmoid(SWIGLU_ALPHA * g))
            acc = acc + _dot(act.astype(BF16), wdn_b[j * nc:(j + 1) * nc, :])
        _to_slot_major(y_ref, 0, acc, EXPERT_TILE)

    @pl.when(valid <= 0)
    def _():
        y_ref[...] = jnp.zeros_like(y_ref)


def _expert_mlp(x_buf, blk_e, blk_valid, layer, wgu, bgu, wdn, bdn):
    D = D_MODEL
    tm = EXPERT_TILE
    per_expert = lambda r, c: pl.BlockSpec((None, None, r, c), lambda i, be, bv: (layer, be[i], 0, 0))
    slots = pl.BlockSpec((tm * SUB, LANES), lambda i, be, bv: (i, 0))
    grid_spec = pltpu.PrefetchScalarGridSpec(
        num_scalar_prefetch=2,
        grid=(x_buf.shape[0] // (tm * SUB),),
        in_specs=[slots, per_expert(D, 2 * D_EXPERT), per_expert(1, 2 * D_EXPERT),
                  per_expert(D_EXPERT, D), per_expert(1, D)],
        out_specs=slots,
        scratch_shapes=[pltpu.VMEM((D, 2 * D_EXPERT), BF16), pltpu.VMEM((D_EXPERT, D), BF16)],
    )
    return pl.pallas_call(
        _expert_kernel,
        grid_spec=grid_spec,
        out_shape=jax.ShapeDtypeStruct(x_buf.shape, F32),
        compiler_params=_params(0, 1),
        name="moe_expert_mlp",
    )(blk_e, blk_valid, x_buf, wgu, bgu, wdn, bdn)


def _combine_kernel(pos_ref, x_ref, gate_ref, y_hbm, g_ref, b_ref, o_ref, buf, sem, *, rows, n_tiles):
    i = pl.program_id(0)
    n_slots = COMBINE_LAG + 1
    s_issue = i % n_slots
    s_done = (i + 1) % n_slots
    sub = COMBINE_SUB

    def issue(j):
        for u in range(sub):
            r = j * sub + u
            for k in range(TOP_K):
                _slot_copy(y_hbm, buf.at[s_issue, k], sem.at[s_issue], pos_ref[r * TOP_K + k], r).start(
                    priority=k % 2)

    def finish(j):
        rs = pl.ds(pl.multiple_of(j * sub, sub), sub)
        row0 = pl.multiple_of(j * (sub * SUB), sub * SUB)
        gates = gate_ref[rs, :]
        y = gates[:, 0:1] * _from_slot_major(buf.at[s_done, 0], row0, sub)
        for k in range(1, TOP_K):
            y = y + gates[:, k:k + 1] * _from_slot_major(buf.at[s_done, k], row0, sub)
        o_ref[rs, :] = _ln_rows(DN_ALPHA * x_ref[rs, :] + y, g_ref[...], b_ref[...])

    def wait_done():
        for k in range(TOP_K):
            _slots_wait(y_hbm, buf.at[s_done, k], sem.at[s_done], rows)

    def loop(do_issue, do_finish):
        def body(j, carry):
            if do_issue:
                issue(j)
            if do_finish:
                finish(j)
            return carry

        lax.fori_loop(0, rows // sub, body, 0)

    @pl.when(i < COMBINE_LAG)
    def _():
        loop(True, False)

    @pl.when(jnp.logical_and(i >= COMBINE_LAG, i < n_tiles))
    def _():
        wait_done()
        loop(True, True)

    @pl.when(i >= n_tiles)
    def _():
        wait_done()
        loop(False, True)


def _combine(x, gates, pos_flat, y_buf, g, b):
    T, D = x.shape
    R = COMBINE_TILE
    n = T // R
    lag = COMBINE_LAG
    done = lambda i: jnp.maximum(i - lag, 0)
    return pl.pallas_call(
        functools.partial(_combine_kernel, rows=R, n_tiles=n),
        grid=(n + lag,),
        in_specs=[pl.BlockSpec((R * TOP_K,), lambda i: (jnp.minimum(i, n - 1),), memory_space=pltpu.SMEM),
                  pl.BlockSpec((R, D), lambda i: (done(i), 0)),
                  pl.BlockSpec((R, LANES), lambda i: (done(i), 0)),
                  pl.BlockSpec(memory_space=pl.ANY),
                  _full(g.shape), _full(b.shape)],
        out_specs=pl.BlockSpec((R, D), lambda i: (done(i), 0)),
        out_shape=jax.ShapeDtypeStruct((T, D), F32),
        scratch_shapes=[pltpu.VMEM((lag + 1, TOP_K, R * SUB, LANES), F32),
                        pltpu.SemaphoreType.DMA((lag + 1,))],
        compiler_params=_params(0, 1),
        name="moe_combine_ln",
    )(pos_flat, x, gates, y_buf, g, b)


def _routing_plan(idx, counts, n_tokens):
    tm = EXPERT_TILE
    e = idx[:, :TOP_K]
    rank = idx[:, TOP_K:2 * TOP_K]
    counts = counts[0, :N_EXPERTS].astype(jnp.int32)
    padded = (counts + tm - 1) // tm * tm
    pend = jnp.cumsum(padded)
    pstart = pend - padded
    pos = (pstart[e] + rank).astype(jnp.int32)
    n_slots = n_tokens * TOP_K + N_EXPERTS * tm
    blk_start = jnp.arange(n_slots // tm, dtype=jnp.int32) * tm
    blk_e = jnp.minimum(jnp.sum(blk_start[:, None] >= pend[None, :], axis=1), N_EXPERTS - 1).astype(jnp.int32)
    blk_valid = jnp.clip(pstart[blk_e] + counts[blk_e] - blk_start, 0, tm).astype(jnp.int32)
    return pos.reshape(-1), blk_e, blk_valid, n_slots


def _moe(x1, idx, gates, counts, layer, wgu, bgu, wdn, bdn, ln_g, ln_b):
    T, D = x1.shape
    pos_flat, blk_e, blk_valid, n_slots = _routing_plan(idx, counts, T)
    x_buf = _dispatch(x1, pos_flat, n_slots)
    y_buf = _expert_mlp(x_buf, blk_e, blk_valid, layer, wgu, bgu[:, :, None, :], wdn, bdn[:, :, None, :])
    return _combine(x1, gates, pos_flat, y_buf, ln_g.reshape(1, D), ln_b.reshape(1, D))


def _trunk(xs, trunks, p):
    T, D = xs.shape
    x = _embed_ln(xs, p["emb_ln_g"], p["emb_ln_b"])
    mc = jnp.asarray(_fnet_channel_np(), BF16)
    for l in range(DEPTH):
        w_in = p["w_in"][l]
        wa = w_in[:, :SSD_WIDTH + SSD_XBC].astype(BF16)
        wdt = jnp.pad(w_in[:, SSD_WIDTH + SSD_XBC:SSD_IN], ((0, 0), (0, LANES - 2 * SSD_HEADS))).astype(BF16)
        wf = w_in[:, SSD_IN:SSD_IN + FNET_WIDTH].astype(BF16)
        wuv = w_in[:, SSD_IN + FNET_WIDTH:].astype(BF16)
        z, xbc, dt, fr, fi, uv = _inproj(x, wa, wdt, wf, wuv, mc)

        lane_pad = LANES - 2 * SSD_HEADS
        dtb = jnp.pad(p["dt_bias"][l].reshape(1, -1), ((0, 0), (0, lane_pad)))
        a_neg = jnp.pad(-jnp.exp(p["a_log"][l].astype(F32)).reshape(1, -1), ((0, 0), (0, lane_pad)))
        dskip_w = jnp.repeat(p["d_skip"][l], SSD_HEAD_DIM).reshape(1, SSD_WIDTH)
        normg = p["ssd_norm_g"][l].reshape(1, SSD_WIDTH)
        convw = p["conv_w"][l]
        convb = p["conv_b"][l].reshape(1, SSD_XBC)
        w_bd = jax.scipy.linalg.block_diag(*[p["fnet_w"][l, g] for g in range(FNET_GROUPS)]).astype(BF16)
        b_f = p["fnet_b"][l].reshape(1, FNET_WIDTH)

        ssd_parts, fn_parts = [], []
        row0 = 0
        for (B, L) in trunks:
            ssd_parts.append(_ssd_mixer(z, xbc, dt, row0, B, L, convw, convb, dtb, a_neg, dskip_w, normg))
            fn_parts.append(_fourier_mixer(fr, fi, row0, B, L, w_bd, b_f))
            row0 += B * L
        ssd = jnp.concatenate(ssd_parts, axis=0)
        fn = jnp.concatenate(fn_parts, axis=0)

        ws_cat = jnp.concatenate([p["gmlp_ws"][l, h] for h in range(GMLP_HEADS)], axis=1).astype(BF16)
        bs_wide = jnp.repeat(p["gmlp_bs"][l].T, GMLP_HDIM, axis=1)
        gm = _gmlp_mixer(uv, p["gmlp_ln_g"][l].reshape(1, -1), p["gmlp_ln_b"][l].reshape(1, -1),
                         ws_cat, bs_wide)

        w_r = jnp.pad(p["router_w"][l], ((0, 0), (0, LANES - N_EXPERTS)))
        wrh = w_r.astype(BF16)
        wrl = (w_r - wrh.astype(F32)).astype(BF16)
        b_r = jnp.pad(p["router_b"][l].reshape(1, -1), ((0, 0), (0, LANES - N_EXPERTS)),
                      constant_values=NEG_BIG)
        x1, idx, gates, counts = _outproj(
            x, ssd, fn, gm, p["w_out"][l].astype(BF16), p["b_out"][l].reshape(1, D),
            p["ln1_g"][l].reshape(1, D), p["ln1_b"][l].reshape(1, D), wrh, wrl, b_r)

        x = _moe(x1, idx, gates, counts, l, p["exp_w_gu"], p["exp_b_gu"], p["exp_w_dn"], p["exp_b_dn"],
                 p["ln2_g"][l], p["ln2_b"][l])
    return x


def kernel(x_prompt, x_sample, emb_ln_g, emb_ln_b, w_in, conv_w, conv_b, dt_bias, a_log, d_skip, ssd_norm_g, fnet_w, fnet_b, gmlp_ln_g, gmlp_ln_b, gmlp_ws, gmlp_bs, w_out, b_out, ln1_g, ln1_b, router_w, router_b, exp_w_gu, exp_b_gu, exp_w_dn, exp_b_dn, ln2_g, ln2_b):
    p = dict(emb_ln_g=emb_ln_g, emb_ln_b=emb_ln_b, w_in=w_in, conv_w=conv_w, conv_b=conv_b,
             dt_bias=dt_bias, a_log=a_log, d_skip=d_skip, ssd_norm_g=ssd_norm_g, fnet_w=fnet_w,
             fnet_b=fnet_b, gmlp_ln_g=gmlp_ln_g, gmlp_ln_b=gmlp_ln_b, gmlp_ws=gmlp_ws, gmlp_bs=gmlp_bs,
             w_out=w_out, b_out=b_out, ln1_g=ln1_g, ln1_b=ln1_b, router_w=router_w, router_b=router_b,
             exp_w_gu=exp_w_gu, exp_b_gu=exp_b_gu, exp_w_dn=exp_w_dn, exp_b_dn=exp_b_dn,
             ln2_g=ln2_g, ln2_b=ln2_b)
    D = x_prompt.shape[-1]
    trunks = [x_prompt.shape[:2], x_sample.shape[:2]]
    xs = jnp.concatenate([x_prompt.reshape(-1, D), x_sample.reshape(-1, D)], axis=0)
    y = _trunk(xs, trunks, p)
    n_p = x_prompt.shape[0] * x_prompt.shape[1]
    return (y[:n_p].reshape(x_prompt.shape), y[n_p:].reshape(x_sample.shape))
```

```python
import functools
import math

import numpy as np
import jax
import jax.numpy as jnp
from jax import lax
from jax.experimental import pallas as pl
from jax.experimental.pallas import tpu as pltpu

F32 = jnp.float32
BF16 = jnp.bfloat16

D_MODEL = 1024
DEPTH = 4
SSD_WIDTH = 512
SSD_HEAD_DIM = 64
SSD_HEADS = 8
SSD_GROUPS = 2
SSD_STATE = 64
SSD_CONV = 5
SSD_GN = SSD_GROUPS * SSD_STATE
SSD_XBC = SSD_WIDTH + 2 * SSD_GN
SSD_IN = SSD_WIDTH + SSD_XBC + 2 * SSD_HEADS
FNET_WIDTH = 256
FNET_GROUPS = 4
FNET_GDIM = 64
GMLP_WIDTH = 256
GMLP_HEADS = 4
GMLP_HDIM = 64
N_EXPERTS = 32
TOP_K = 4
D_EXPERT = D_MODEL
SWIGLU_LIMIT = 7.0
SWIGLU_ALPHA = 1.702
LN_EPS = 1e-5
RMS_EPS = 1e-5
DN_ALPHA = (2 * DEPTH) ** 0.25

LANES = 128
CHUNK = 128
HALO = 16
VMEM_LIMIT = 56 * 1024 * 1024

TOK_TILE = 512
EXPERT_TILE = 512
EXPERT_COL_CHUNK = 512
DISPATCH_TILE = 512
COMBINE_TILE = 256
COMBINE_SUB = 32
COMBINE_LAG = 2
ROW_UNROLL = 8
SSD_STEP_CHUNKS = 4
FNET_ROW_GROUP = 16
NEG_BIG = -1e30


def _dot(a, b):
    return jnp.dot(a, b, preferred_element_type=F32)


def _split2(v):
    hi = v.astype(BF16)
    lo = (v - hi.astype(F32)).astype(BF16)
    return hi, lo


def _split3(v):
    hi = v.astype(BF16)
    r = v - hi.astype(F32)
    mid = r.astype(BF16)
    lo = (r - mid.astype(F32)).astype(BF16)
    return hi, mid, lo


def _ln_rows(xf, g, b):
    mu = jnp.mean(xf, -1, keepdims=True)
    xc = xf - mu
    var = jnp.mean(xc * xc, -1, keepdims=True)
    return xc * lax.rsqrt(var + LN_EPS) * g + b


def _params(n_parallel=1, n_arbitrary=0):
    sem = ("parallel",) * n_parallel + ("arbitrary",) * n_arbitrary
    return pltpu.CompilerParams(dimension_semantics=sem, vmem_limit_bytes=VMEM_LIMIT)


def _full(shape):
    nd = len(shape)
    return pl.BlockSpec(shape, lambda *_: (0,) * nd)


def _embed_ln_kernel(xa_ref, xb_ref, g_ref, b_ref, o_ref, *, tiles_a):
    i = pl.program_id(0)

    @pl.when(i < tiles_a)
    def _():
        o_ref[...] = _ln_rows(xa_ref[...], g_ref[...], b_ref[...])

    @pl.when(i >= tiles_a)
    def _():
        o_ref[...] = _ln_rows(xb_ref[...], g_ref[...], b_ref[...])


def _embed_ln(xa, xb, g, b):
    D = xa.shape[1]
    na, nb = xa.shape[0] // TOK_TILE, xb.shape[0] // TOK_TILE
    return pl.pallas_call(
        functools.partial(_embed_ln_kernel, tiles_a=na),
        grid=(na + nb,),
        in_specs=[pl.BlockSpec((TOK_TILE, D), lambda i: (jnp.minimum(i, na - 1), 0)),
                  pl.BlockSpec((TOK_TILE, D), lambda i: (jnp.maximum(i - na, 0), 0)),
                  _full((1, D)), _full((1, D))],
        out_specs=pl.BlockSpec((TOK_TILE, D), lambda i: (i, 0)),
        out_shape=jax.ShapeDtypeStruct(((na + nb) * TOK_TILE, D), F32),
        compiler_params=_params(1),
        name="embed_ln",
    )(xa, xb, g.reshape(1, D), b.reshape(1, D))


def _inproj_kernel(x_ref, wa_ref, wdt_ref, wf_ref, wuv_ref, mc_ref,
                   z_ref, xbc_ref, dt_ref, fr_ref, fi_ref, uv_ref):
    xb = x_ref[...].astype(BF16)
    a = _dot(xb, wa_ref[...])
    z_ref[...] = a[:, :SSD_WIDTH].astype(BF16)
    xbc_ref[...] = a[:, SSD_WIDTH:].astype(BF16)
    dt_ref[...] = _dot(xb, wdt_ref[...])
    f = _dot(xb, wf_ref[...]).astype(BF16)
    fri = _dot(f, mc_ref[...])
    fr_ref[...] = fri[:, :FNET_WIDTH]
    fi_ref[...] = fri[:, FNET_WIDTH:]
    uv_ref[...] = _dot(xb, wuv_ref[...]).astype(BF16)


def _inproj(x, wa, wdt, wf, wuv, mc):
    T, D = x.shape
    tm = TOK_TILE
    row = lambda w: pl.BlockSpec((tm, w), lambda i: (i, 0))
    return pl.pallas_call(
        _inproj_kernel,
        grid=(T // tm,),
        in_specs=[row(D), _full(wa.shape), _full(wdt.shape), _full(wf.shape), _full(wuv.shape),
                  _full(mc.shape)],
        out_specs=[row(SSD_WIDTH), row(SSD_XBC), row(LANES), row(FNET_WIDTH), row(FNET_WIDTH),
                   row(2 * GMLP_WIDTH)],
        out_shape=[jax.ShapeDtypeStruct((T, SSD_WIDTH), BF16),
                   jax.ShapeDtypeStruct((T, SSD_XBC), BF16),
                   jax.ShapeDtypeStruct((T, LANES), F32),
                   jax.ShapeDtypeStruct((T, FNET_WIDTH), F32),
                   jax.ShapeDtypeStruct((T, FNET_WIDTH), F32),
                   jax.ShapeDtypeStruct((T, 2 * GMLP_WIDTH), BF16)],
        compiler_params=_params(1),
        name="inproj",
    )(x, wa, wdt, wf, wuv, mc)


def _conv_silu(xc_ref, xp_ref, xn_ref, convw_ref, convb_ref, first, last):
    Q = xc_ref.shape[0]
    cur = xc_ref[...].astype(F32)
    prev = jnp.where(first, 0.0, xp_ref[...].astype(F32))
    nxt = jnp.where(last, 0.0, xn_ref[...].astype(F32))
    ext = jnp.concatenate([prev[HALO - 8:], cur, nxt[:8]], axis=0)
    n_ext = Q + 16
    pad = SSD_CONV // 2
    conv = convb_ref[...]
    for k in range(SSD_CONV):
        shift = (pad - k) % n_ext
        tap = ext if shift == 0 else pltpu.roll(ext, shift, axis=0)
        conv = conv + tap[8:8 + Q] * convw_ref[k:k + 1, :]
    return conv * jax.nn.sigmoid(conv)


def _ssd_chunk(act, dt_raw, dtb_ref, a_ref, cum_ref, expand_ref, state_ref, *, backward):
    Q = CHUNK
    h_off = SSD_HEADS if backward else 0
    xs = act[:, :SSD_WIDTH]
    bm = act[:, SSD_WIDTH:SSD_WIDTH + SSD_GN]
    cm = act[:, SSD_WIDTH + SSD_GN:]

    dt = jax.nn.softplus(dt_raw + dtb_ref[...])
    adt = dt * a_ref[...]
    cum = cum_ref[...]
    h3 = _split3(adt)
    acum = _dot(cum, h3[0]) + _dot(cum, h3[1]) + _dot(cum, h3[2])
    acum_t = acum.T
    total = acum[0:1, :] if backward else acum[Q - 1:Q, :]
    dec_in = jnp.exp(acum)
    dec_end = jnp.exp(total - acum)

    expand = expand_ref[...]

    def widen(v):
        hi, lo = _split2(v)
        return _dot(hi, expand) + _dot(lo, expand)

    dt_w = widen(dt)
    dec_in_w = widen(dec_in)
    dec_end_w = widen(dec_end)
    tdec_w = dec_in_w[0:1, :] if backward else dec_in_w[Q - 1:Q, :]

    xdt = xs * dt_w

    row = lax.broadcasted_iota(jnp.int32, (Q, Q), 0)
    col = lax.broadcasted_iota(jnp.int32, (Q, Q), 1)
    mask = (col > row) if backward else (col <= row)
    lane = lax.broadcasted_iota(jnp.int32, (Q, LANES), 1)
    lo_half = lane < SSD_STATE

    bm_b = bm.astype(BF16)
    cb = []
    for g in range(SSD_GROUPS):
        cm_g = jnp.where(lo_half if g == 0 else jnp.logical_not(lo_half), cm, 0.0).astype(BF16)
        cb.append(lax.dot_general(cm_g, bm_b, (((1,), (1,)), ((), ())),
                                  preferred_element_type=F32))

    y_parts = []
    for pair in range(SSD_HEADS // 2):
        gmats = []
        for r in range(2):
            h = 2 * pair + r
            hl = h_off + h
            diff = acum[:, hl:hl + 1] - acum_t[hl:hl + 1, :]
            dec = jnp.exp(jnp.where(mask, diff, NEG_BIG))
            gmats.append((cb[h // (SSD_HEADS // SSD_GROUPS)] * dec).astype(BF16))
        lhs = jnp.concatenate(gmats, axis=1)
        x2 = xdt[:, pair * LANES:(pair + 1) * LANES]
        rhs = jnp.concatenate([jnp.where(lo_half, x2, 0.0), jnp.where(lo_half, 0.0, x2)],
                              axis=0).astype(BF16)
        y_parts.append(_dot(lhs, rhs))
    y_diag = jnp.concatenate(y_parts, axis=1)

    srow = lax.broadcasted_iota(jnp.int32, (SSD_GN, SSD_WIDTH), 0)
    scol = lax.broadcasted_iota(jnp.int32, (SSD_GN, SSD_WIDTH), 1)
    on_group = (srow // SSD_STATE) == (scol // (SSD_WIDTH // SSD_GROUPS))
    st_new = _dot(bm.T.astype(BF16), (xdt * dec_end_w).astype(BF16))
    st_new = jnp.where(on_group, st_new, 0.0)
    st_prev = state_ref[...]
    y_off = _dot(cm.astype(BF16), st_prev.astype(BF16)) * dec_in_w
    state_ref[...] = st_prev * tdec_w + st_new
    return y_diag + y_off, xs


def _ssd_bwd_kernel(xc_ref, xp_ref, xn_ref, dt_ref, convw_ref, convb_ref, dtb_ref, a_ref,
                    cum_ref, expand_ref, yb_ref, act_ref, state_ref, *, n_steps):
    c = pl.program_id(1)
    step = n_steps - 1 - c

    @pl.when(c == 0)
    def _():
        state_ref[...] = jnp.zeros_like(state_ref)

    act = _conv_silu(xc_ref, xp_ref, xn_ref, convw_ref, convb_ref,
                     first=step == 0, last=step == n_steps - 1)
    act_ref[...] = act.astype(BF16)
    for s in reversed(range(SSD_STEP_CHUNKS)):
        rows = slice(s * CHUNK, (s + 1) * CHUNK)
        y, _ = _ssd_chunk(act[rows], dt_ref[rows, :], dtb_ref, a_ref, cum_ref, expand_ref, state_ref,
                          backward=True)
        yb_ref[rows, :] = y


def _ssd_fwd_kernel(act_ref, dt_ref, z_ref, yb_ref, dtb_ref, a_ref, cum_ref, expand_ref,
                    dskip_ref, normg_ref, o_ref, state_ref):
    c = pl.program_id(1)

    @pl.when(c == 0)
    def _():
        state_ref[...] = jnp.zeros_like(state_ref)

    gw = SSD_WIDTH // SSD_GROUPS
    for s in range(SSD_STEP_CHUNKS):
        rows = slice(s * CHUNK, (s + 1) * CHUNK)
        y, xs = _ssd_chunk(act_ref[rows, :].astype(F32), dt_ref[rows, :], dtb_ref, a_ref, cum_ref,
                           expand_ref, state_ref, backward=False)
        z = z_ref[rows, :].astype(F32)
        y = (y + yb_ref[rows, :] + xs * dskip_ref[...]) * (z * jax.nn.sigmoid(z))
        outs = []
        for g in range(SSD_GROUPS):
            yg = y[:, g * gw:(g + 1) * gw]
            outs.append(yg * lax.rsqrt(jnp.mean(yg * yg, -1, keepdims=True) + RMS_EPS))
        o_ref[rows, :] = (jnp.concatenate(outs, axis=1) * normg_ref[...]).astype(BF16)


def _ssd_mixer(z, xbc, dt, row0, B, L, convw, convb, dtb, a_neg, dskip_w, normg):
    R = SSD_STEP_CHUNKS * CHUNK
    C = L // R
    c0 = row0 // R
    hpc = R // HALO
    n_halo = z.shape[0] // HALO
    consts = _ssd_consts()

    def specs(chunk_of):
        cur = lambda w: pl.BlockSpec((R, w), lambda b, c: (c0 + b * C + chunk_of(c), 0))
        prv = pl.BlockSpec((HALO, SSD_XBC),
                           lambda b, c: (jnp.maximum((c0 + b * C + chunk_of(c)) * hpc - 1, 0), 0))
        nxt = pl.BlockSpec((HALO, SSD_XBC),
                           lambda b, c: (jnp.minimum((c0 + b * C + chunk_of(c) + 1) * hpc, n_halo - 1), 0))
        return cur, prv, nxt

    out_row = lambda chunk_of, w: pl.BlockSpec((R, w), lambda b, c: (b * C + chunk_of(c), 0))

    rev = lambda c: C - 1 - c
    cur, prv, nxt = specs(rev)
    yb, act = pl.pallas_call(
        functools.partial(_ssd_bwd_kernel, n_steps=C),
        grid=(B, C),
        in_specs=[cur(SSD_XBC), prv, nxt, cur(LANES), _full(convw.shape), _full(convb.shape),
                  _full(dtb.shape), _full(a_neg.shape),
                  _full(consts["cum_b"].shape), _full(consts["expand_b"].shape)],
        out_specs=[out_row(rev, SSD_WIDTH), out_row(rev, SSD_XBC)],
        out_shape=[jax.ShapeDtypeStruct((B * L, SSD_WIDTH), F32),
                   jax.ShapeDtypeStruct((B * L, SSD_XBC), BF16)],
        scratch_shapes=[pltpu.VMEM((SSD_GN, SSD_WIDTH), F32)],
        compiler_params=_params(1, 1),
        name="ssd_bwd",
    )(xbc, xbc, xbc, dt, convw, convb, dtb, a_neg, consts["cum_b"], consts["expand_b"])

    fwd = lambda c: c
    cur, _, _ = specs(fwd)
    return pl.pallas_call(
        _ssd_fwd_kernel,
        grid=(B, C),
        in_specs=[out_row(fwd, SSD_XBC), cur(LANES), cur(SSD_WIDTH), out_row(fwd, SSD_WIDTH),
                  _full(dtb.shape), _full(a_neg.shape),
                  _full(consts["cum_f"].shape), _full(consts["expand_f"].shape),
                  _full(dskip_w.shape), _full(normg.shape)],
        out_specs=out_row(fwd, SSD_WIDTH),
        out_shape=jax.ShapeDtypeStruct((B * L, SSD_WIDTH), BF16),
        scratch_shapes=[pltpu.VMEM((SSD_GN, SSD_WIDTH), F32)],
        compiler_params=_params(1, 1),
        name="ssd_fwd",
    )(act, dt, z, yb, dtb, a_neg, consts["cum_f"], consts["expand_f"], dskip_w, normg)


@functools.lru_cache(maxsize=None)
def _ssd_consts_np():
    q = np.arange(CHUNK)
    cum_f = (q[None, :] <= q[:, None]).astype(np.float32)
    cum_b = (q[None, :] >= q[:, None]).astype(np.float32)
    out = {"cum_f": cum_f, "cum_b": cum_b}
    for name, off in (("expand_f", 0), ("expand_b", SSD_HEADS)):
        e = np.zeros((LANES, SSD_WIDTH), np.float32)
        for h in range(SSD_HEADS):
            e[off + h, h * SSD_HEAD_DIM:(h + 1) * SSD_HEAD_DIM] = 1.0
        out[name] = e
    return out


def _ssd_consts():
    return {k: jnp.asarray(v, BF16) for k, v in _ssd_consts_np().items()}


@functools.lru_cache(maxsize=None)
def _fnet_tables_np(L):
    L2 = CHUNK
    L1 = L // L2
    k1 = np.arange(L1)
    ang1 = 2.0 * np.pi * ((k1[:, None] * k1[None, :]) % L1) / L1
    f1 = np.concatenate([np.cos(ang1), np.sin(ang1)], axis=0) / math.sqrt(L1)
    k = k1[:, None, None] + L1 * np.arange(L2)[None, :, None]
    l2 = np.arange(L2)[None, None, :]
    ang2 = 2.0 * np.pi * ((k * l2) % L) / L
    gc = np.cos(ang2) / math.sqrt(L2)
    gs = np.sin(ang2) / math.sqrt(L2)
    return f1.astype(np.float32), gc.astype(np.float32), gs.astype(np.float32)


@functools.lru_cache(maxsize=None)
def _fnet_channel_np():
    c = np.arange(FNET_GDIM)
    ang = 2.0 * np.pi * ((c[:, None] * c[None, :]) % FNET_GDIM) / FNET_GDIM
    cc = np.cos(ang) / math.sqrt(FNET_GDIM)
    sc = np.sin(ang) / math.sqrt(FNET_GDIM)
    eye = np.eye(FNET_GROUPS)
    return np.concatenate([np.kron(eye, cc), np.kron(eye, -sc)], axis=1).astype(np.float32)


def _fnet_stage1_kernel(f_ref, xr_ref, xi_ref, yr_ref, yi_ref, *, L1, group):
    W = FNET_WIDTH
    f = f_ref[...]
    for j in range(group):
        x = jnp.concatenate([xr_ref[:, j, :], xi_ref[:, j, :]], axis=1).astype(BF16)
        p = _dot(f, x)
        yr_ref[:, j, :] = p[:L1, :W] + p[L1:, W:]
        yi_ref[:, j, :] = p[:L1, W:] - p[L1:, :W]


def _fnet_stage2_kernel(yr_ref, yi_ref, g_ref, w_ref, b_ref, o_ref, *, group):
    for j in range(group):
        y = jnp.concatenate([yr_ref[j], yi_ref[j]], axis=0).astype(BF16)
        zr = _dot(g_ref[j], y)
        o_ref[:, j, :] = _dot(zr.astype(BF16), w_ref[...]) + b_ref[...]


def _fourier_mixer(fr, fi, row0, B, L, w_bd, b_f):
    L2 = CHUNK
    L1 = L // L2
    W = FNET_WIDTH
    f1, gc, gs = _fnet_tables_np(L)
    f1 = jnp.asarray(f1, BF16)
    g_cat = jnp.asarray(np.concatenate([gc, gs], axis=2), BF16)
    xr = fr.reshape(-1, L2, W)
    xi = fi.reshape(-1, L2, W)
    blk0 = row0 // L2 // L1
    grp1 = FNET_ROW_GROUP
    xblk = pl.BlockSpec((L1, grp1, W), lambda b, j: (blk0 + b, j, 0))
    yblk = pl.BlockSpec((L1, grp1, W), lambda b, j: (b, j, 0))
    yr, yi = pl.pallas_call(
        functools.partial(_fnet_stage1_kernel, L1=L1, group=grp1),
        grid=(B, L2 // grp1),
        in_specs=[_full(f1.shape), xblk, xblk],
        out_specs=[yblk, yblk],
        out_shape=[jax.ShapeDtypeStruct((B * L1, L2, W), F32)] * 2,
        compiler_params=_params(2),
        name="fnet_stage1",
    )(f1, xr, xi)
    grp2 = FNET_ROW_GROUP // 2
    n2 = L1 // grp2
    yblk2 = pl.BlockSpec((grp2, L2, W), lambda b, k: (b * n2 + k, 0, 0))
    gblk = pl.BlockSpec((grp2, L2, 2 * L2), lambda b, k: (k, 0, 0))
    out = pl.pallas_call(
        functools.partial(_fnet_stage2_kernel, group=grp2),
        grid=(B, n2),
        in_specs=[yblk2, yblk2, gblk, _full(w_bd.shape), _full(b_f.shape)],
        out_specs=pl.BlockSpec((L2, grp2, W), lambda b, k: (b, k, 0)),
        out_shape=jax.ShapeDtypeStruct((B * L2, L1, W), F32),
        compiler_params=_params(2),
        name="fnet_stage2",
    )(yr, yi, g_cat, w_bd, b_f)
    return out.reshape(B * L, W)


def _gmlp_kernel(uv_ref, g_ref, b_ref, ws_ref, bs_ref, o_ref, *, n_chunks):
    uv = uv_ref[...].astype(F32)
    ge = 0.5 * uv * (1.0 + lax.erf(uv * (1.0 / math.sqrt(2.0))))
    u = ge[:, :GMLP_WIDTH]
    v = _ln_rows(ge[:, GMLP_WIDTH:], g_ref[...], b_ref[...])
    lane = lax.broadcasted_iota(jnp.int32, (CHUNK, GMLP_WIDTH), 1)
    ws = ws_ref[...]
    bs = bs_ref[...]
    for j in range(n_chunks):
        vj = v[j * CHUNK:(j + 1) * CHUNK]
        rhs = jnp.concatenate(
            [jnp.where(lane // GMLP_HDIM == h, vj, 0.0) for h in range(GMLP_HEADS)], axis=0)
        sv = _dot(ws, rhs.astype(BF16)) + bs
        o_ref[j * CHUNK:(j + 1) * CHUNK, :] = (u[j * CHUNK:(j + 1) * CHUNK] * sv).astype(BF16)


def _gmlp_mixer(uv, ln_g, ln_b, ws_cat, bs_wide):
    T = uv.shape[0]
    tm = TOK_TILE
    return pl.pallas_call(
        functools.partial(_gmlp_kernel, n_chunks=tm // CHUNK),
        grid=(T // tm,),
        in_specs=[pl.BlockSpec((tm, 2 * GMLP_WIDTH), lambda i: (i, 0)), _full(ln_g.shape),
                  _full(ln_b.shape), _full(ws_cat.shape), _full(bs_wide.shape)],
        out_specs=pl.BlockSpec((tm, GMLP_WIDTH), lambda i: (i, 0)),
        out_shape=jax.ShapeDtypeStruct((T, GMLP_WIDTH), BF16),
        compiler_params=_params(1),
        name="gmlp",
    )(uv, ln_g, ln_b, ws_cat, bs_wide)


def _outproj_kernel(x_ref, ssd_ref, fn_ref, gm_ref, wo_ref, bo_ref, g_ref, b_ref,
                    wrh_ref, wrl_ref, br_ref, tri_ref, x1_ref, route_ref, gate_ref, cnt_ref, seen_ref):
    @pl.when(pl.program_id(0) == 0)
    def _():
        seen_ref[...] = jnp.zeros_like(seen_ref)

    acc = _dot(ssd_ref[...], wo_ref[0:SSD_WIDTH, :])
    acc = acc + _dot(fn_ref[...].astype(BF16), wo_ref[SSD_WIDTH:SSD_WIDTH + FNET_WIDTH, :])
    acc = acc + _dot(gm_ref[...], wo_ref[SSD_WIDTH + FNET_WIDTH:, :])
    h = DN_ALPHA * x_ref[...] + acc + bo_ref[...]
    x1 = _ln_rows(h, g_ref[...], b_ref[...])
    x1_ref[...] = x1

    xh, xl = _split2(x1)
    wrh = wrh_ref[...]
    logits = _dot(xh, wrh) + _dot(xl, wrh) + _dot(xh, wrl_ref[...]) + br_ref[...]
    work = logits.T[:N_EXPERTS]
    n_tok = work.shape[1]
    row = lax.broadcasted_iota(jnp.int32, work.shape, 0)
    vals, idxs = [], []
    for _ in range(TOP_K):
        m = jnp.max(work, axis=0, keepdims=True)
        i = jnp.min(jnp.where(work == m, row, N_EXPERTS), axis=0, keepdims=True)
        vals.append(m)
        idxs.append(i)
        work = jnp.where(row == i, -jnp.inf, work)
    exps = [jnp.exp(v - vals[0]) for v in vals]
    denom = exps[0]
    for e in exps[1:]:
        denom = denom + e

    onehots = [row == i for i in idxs]
    chosen = jnp.zeros(work.shape, F32)
    for oh in onehots:
        chosen = jnp.where(oh, 1.0, chosen)
    seen = seen_ref[...]
    before = _dot(chosen.astype(BF16), tri_ref[...]) + seen[:, 0:1]
    seen = seen + jnp.sum(chosen, axis=1, keepdims=True)
    seen_ref[...] = seen
    cnt_ref[...] = seen

    row8 = lax.broadcasted_iota(jnp.int32, (2 * TOP_K, n_tok), 0)
    route = jnp.zeros((2 * TOP_K, n_tok), jnp.int32)
    gates_t = jnp.zeros((2 * TOP_K, n_tok), F32)
    for k in range(TOP_K):
        rank = jnp.sum(jnp.where(onehots[k], before, 0.0), axis=0, keepdims=True).astype(jnp.int32)
        route = jnp.where(row8 == k, idxs[k], route)
        route = jnp.where(row8 == TOP_K + k, rank, route)
        gates_t = jnp.where(row8 == k, exps[k] / denom, gates_t)
    route_ref[...] = route
    pad = jnp.zeros((LANES - 2 * TOP_K, n_tok), F32)
    gate_ref[...] = jnp.concatenate([gates_t, pad], axis=0).T


def _outproj(x, ssd, fn, gm, wo, bo, g, b, wrh, wrl, br):
    T, D = x.shape
    tm = TOK_TILE
    row = lambda w: pl.BlockSpec((tm, w), lambda i: (i, 0))
    q = np.arange(tm)
    tri = jnp.asarray((q[:, None] < q[None, :]).astype(np.float32), BF16)
    return pl.pallas_call(
        _outproj_kernel,
        grid=(T // tm,),
        in_specs=[row(D), row(SSD_WIDTH), row(FNET_WIDTH), row(GMLP_WIDTH), _full(wo.shape),
                  _full(bo.shape), _full(g.shape), _full(b.shape), _full(wrh.shape),
                  _full(wrl.shape), _full(br.shape), _full(tri.shape)],
        out_specs=[row(D), pl.BlockSpec((2 * TOP_K, tm), lambda i: (0, i)), row(LANES),
                   _full((N_EXPERTS, LANES))],
        out_shape=[jax.ShapeDtypeStruct((T, D), F32),
                   jax.ShapeDtypeStruct((2 * TOP_K, T), jnp.int32),
                   jax.ShapeDtypeStruct((T, LANES), F32),
                   jax.ShapeDtypeStruct((N_EXPERTS, LANES), F32)],
        scratch_shapes=[pltpu.VMEM((N_EXPERTS, LANES), F32)],
        compiler_params=_params(0, 1),
        name="outproj_ln_router",
    )(x, ssd, fn, gm, wo, bo, g, b, wrh, wrl, br, tri)


SUB = D_MODEL // LANES


def _to_slot_major(dst_ref, dst_row0, x, n):
    for c in range(SUB):
        dst_ref[pl.ds(dst_row0 + c, n, stride=SUB), :] = x[:, c * LANES:(c + 1) * LANES]


def _from_slot_major(src_ref, src_row0, n):
    return jnp.concatenate([src_ref[pl.ds(src_row0 + c, n, stride=SUB), :] for c in range(SUB)], axis=1)


def _slot_copy(src, dst, sem, src_slot, dst_slot):
    s0 = pl.multiple_of(src_slot * SUB, SUB)
    d0 = pl.multiple_of(dst_slot * SUB, SUB)
    return pltpu.make_async_copy(src.at[pl.ds(s0, SUB), :], dst.at[pl.ds(d0, SUB), :], sem)


def _slots_wait(src, dst, sem, n_slots):
    pltpu.make_async_copy(src.at[pl.ds(0, n_slots * SUB), :], dst.at[pl.ds(0, n_slots * SUB), :], sem).wait()


def _dispatch_kernel(pos_ref, x_ref, xbuf_hbm, xs_ref, sem, *, rows):
    _to_slot_major(xs_ref, 0, x_ref[...], rows)

    def body(j, carry):
        for u in range(ROW_UNROLL):
            r = j * ROW_UNROLL + u
            for k in range(TOP_K):
                _slot_copy(xs_ref, xbuf_hbm, sem, r, pos_ref[k * rows + r]).start(priority=k % 2)
        return carry

    lax.fori_loop(0, rows // ROW_UNROLL, body, 0)
    for _ in range(TOP_K):
        _slots_wait(xs_ref, xbuf_hbm, sem, rows)


def _dispatch(x, pos_flat, n_slots):
    T, D = x.shape
    R = DISPATCH_TILE
    return pl.pallas_call(
        functools.partial(_dispatch_kernel, rows=R),
        grid=(T // R,),
        in_specs=[pl.BlockSpec((R * TOP_K,), lambda i: (i,), memory_space=pltpu.SMEM),
                  pl.BlockSpec((R, D), lambda i: (i, 0))],
        out_specs=pl.BlockSpec(memory_space=pl.ANY),
        out_shape=jax.ShapeDtypeStruct((n_slots * SUB, LANES), F32),
        scratch_shapes=[pltpu.VMEM((R * SUB, LANES), F32), pltpu.SemaphoreType.DMA(())],
        compiler_params=_params(1),
        name="moe_dispatch",
    )(pos_flat, x)


def _expert_kernel(blk_e_ref, valid_ref, x_ref, wgu_ref, bgu_ref, wdn_ref, bdn_ref, y_ref, wgu_b, wdn_b):
    i = pl.program_id(0)
    valid = valid_ref[i]
    prev_e = blk_e_ref[jnp.maximum(i - 1, 0)]
    new_expert = jnp.logical_or(i == 0, blk_e_ref[i] != prev_e)

    @pl.when(jnp.logical_and(valid > 0, new_expert))
    def _():
        wgu_b[...] = wgu_ref[...].astype(BF16)
        wdn_b[...] = wdn_ref[...].astype(BF16)

    @pl.when(valid > 0)
    def _():
        x = _from_slot_major(x_ref, 0, EXPERT_TILE)
        row = lax.broadcasted_iota(jnp.int32, x.shape, 0)
        xb = jnp.where(row < valid, x, 0.0).astype(BF16)
        acc = jnp.broadcast_to(bdn_ref[...], x.shape)
        nc = EXPERT_COL_CHUNK
        for j in range(D_EXPERT // nc):
            g = _dot(xb, wgu_b[:, j * nc:(j + 1) * nc]) + bgu_ref[:, j * nc:(j + 1) * nc]
            u = (_dot(xb, wgu_b[:, D_EXPERT + j * nc:D_EXPERT + (j + 1) * nc])
                 + bgu_ref[:, D_EXPERT + j * nc:D_EXPERT + (j + 1) * nc])
            g = jnp.minimum(g, SWIGLU_LIMIT)
            u = jnp.clip(u, -SWIGLU_LIMIT, SWIGLU_LIMIT)
            act = (u + 1.0) * (g * jax.nn.sigmoid(SWIGLU_ALPHA * g))
            acc = acc + _dot(act.astype(BF16), wdn_b[j * nc:(j + 1) * nc, :])
        _to_slot_major(y_ref, 0, acc, EXPERT_TILE)

    @pl.when(valid <= 0)
    def _():
        y_ref[...] = jnp.zeros_like(y_ref)


def _expert_mlp(x_buf, blk_e, blk_valid, layer, wgu, bgu, wdn, bdn):
    D = D_MODEL
    tm = EXPERT_TILE
    per_expert = lambda r, c: pl.BlockSpec((None, None, r, c), lambda i, be, bv: (layer, be[i], 0, 0))
    slots = pl.BlockSpec((tm * SUB, LANES), lambda i, be, bv: (i, 0))
    grid_spec = pltpu.PrefetchScalarGridSpec(
        num_scalar_prefetch=2,
        grid=(x_buf.shape[0] // (tm * SUB),),
        in_specs=[slots, per_expert(D, 2 * D_EXPERT), per_expert(1, 2 * D_EXPERT),
                  per_expert(D_EXPERT, D), per_expert(1, D)],
        out_specs=slots,
        scratch_shapes=[pltpu.VMEM((D, 2 * D_EXPERT), BF16), pltpu.VMEM((D_EXPERT, D), BF16)],
    )
    return pl.pallas_call(
        _expert_kernel,
        grid_spec=grid_spec,
        out_shape=jax.ShapeDtypeStruct(x_buf.shape, F32),
        compiler_params=_params(0, 1),
        name="moe_expert_mlp",
    )(blk_e, blk_valid, x_buf, wgu, bgu, wdn, bdn)


def _combine_kernel(pos_ref, x_ref, gate_ref, y_hbm, g_ref, b_ref, o_ref, *scratch, rows, n_tiles):
    n_slots = COMBINE_LAG + 1
    bufs, sem = scratch[:n_slots], scratch[n_slots]
    i = pl.program_id(0)
    sub = COMBINE_SUB

    def run(s_issue, s_done):
        def issue(j):
            for u in range(sub):
                r = j * sub + u
                for k in range(TOP_K):
                    _slot_copy(y_hbm, bufs[s_issue].at[k], sem.at[s_issue], pos_ref[k * rows + r], r).start(
                        priority=k % 2)

        def finish(j):
            rs = pl.ds(pl.multiple_of(j * sub, sub), sub)
            row0 = pl.multiple_of(j * (sub * SUB), sub * SUB)
            gates = gate_ref[rs, :]
            y = gates[:, 0:1] * _from_slot_major(bufs[s_done].at[0], row0, sub)
            for k in range(1, TOP_K):
                y = y + gates[:, k:k + 1] * _from_slot_major(bufs[s_done].at[k], row0, sub)
            o_ref[rs, :] = _ln_rows(DN_ALPHA * x_ref[rs, :] + y, g_ref[...], b_ref[...])

        def wait_done():
            for k in range(TOP_K):
                _slots_wait(y_hbm, bufs[s_done].at[k], sem.at[s_done], rows)

        def loop(do_issue, do_finish):
            def body(j, carry):
                if do_issue:
                    issue(j)
                if do_finish:
                    finish(j)
                return carry

            lax.fori_loop(0, rows // sub, body, 0)

        @pl.when(i < COMBINE_LAG)
        def _():
            loop(True, False)

        @pl.when(jnp.logical_and(i >= COMBINE_LAG, i < n_tiles))
        def _():
            wait_done()
            loop(True, True)

        @pl.when(i >= n_tiles)
        def _():
            wait_done()
            loop(False, True)

    for s in range(n_slots):
        pl.when(i % n_slots == s)(functools.partial(run, s, (s + 1) % n_slots))


def _combine(x, gates, pos_flat, y_buf, g, b):
    T, D = x.shape
    R = COMBINE_TILE
    n = T // R
    lag = COMBINE_LAG
    done = lambda i: jnp.maximum(i - lag, 0)
    return pl.pallas_call(
        functools.partial(_combine_kernel, rows=R, n_tiles=n),
        grid=(n + lag,),
        in_specs=[pl.BlockSpec((R * TOP_K,), lambda i: (jnp.minimum(i, n - 1),), memory_space=pltpu.SMEM),
                  pl.BlockSpec((R, D), lambda i: (done(i), 0)),
                  pl.BlockSpec((R, LANES), lambda i: (done(i), 0)),
                  pl.BlockSpec(memory_space=pl.ANY),
                  _full(g.shape), _full(b.shape)],
        out_specs=pl.BlockSpec((R, D), lambda i: (done(i), 0)),
        out_shape=jax.ShapeDtypeStruct((T, D), F32),
        scratch_shapes=[pltpu.VMEM((TOP_K, R * SUB, LANES), F32) for _ in range(lag + 1)]
                       + [pltpu.SemaphoreType.DMA((lag + 1,))],
        compiler_params=_params(0, 1),
        name="moe_combine_ln",
    )(pos_flat, x, gates, y_buf, g, b)


def _tile_major(pos, tile):
    K, T = pos.shape
    return pos.reshape(K, T // tile, tile).transpose(1, 0, 2).reshape(-1)


def _routing_plan(route, counts, n_tokens):
    tm = EXPERT_TILE
    e = route[:TOP_K]
    rank = route[TOP_K:]
    counts = counts[:, 0].astype(jnp.int32)
    padded = (counts + tm - 1) // tm * tm
    pend = jnp.cumsum(padded)
    pstart = pend - padded
    pos = (pstart[e] + rank).astype(jnp.int32)
    n_slots = n_tokens * TOP_K + N_EXPERTS * tm
    blk_start = jnp.arange(n_slots // tm, dtype=jnp.int32) * tm
    blk_e = jnp.minimum(jnp.sum(blk_start[:, None] >= pend[None, :], axis=1), N_EXPERTS - 1).astype(jnp.int32)
    blk_valid = jnp.clip(pstart[blk_e] + counts[blk_e] - blk_start, 0, tm).astype(jnp.int32)
    return pos, blk_e, blk_valid, n_slots


def _moe(x1, route, gates, counts, layer, wgu, bgu, wdn, bdn, ln_g, ln_b):
    T, D = x1.shape
    pos, blk_e, blk_valid, n_slots = _routing_plan(route, counts, T)
    x_buf = _dispatch(x1, _tile_major(pos, DISPATCH_TILE), n_slots)
    y_buf = _expert_mlp(x_buf, blk_e, blk_valid, layer, wgu, bgu[:, :, None, :], wdn, bdn[:, :, None, :])
    return _combine(x1, gates, _tile_major(pos, COMBINE_TILE), y_buf, ln_g.reshape(1, D), ln_b.reshape(1, D))


def _trunk(x_parts, trunks, p):
    x = _embed_ln(x_parts[0], x_parts[1], p["emb_ln_g"], p["emb_ln_b"])
    T, D = x.shape
    mc = jnp.asarray(_fnet_channel_np(), BF16)
    for l in range(DEPTH):
        w_in = p["w_in"][l]
        wa = w_in[:, :SSD_WIDTH + SSD_XBC].astype(BF16)
        wdt = jnp.pad(w_in[:, SSD_WIDTH + SSD_XBC:SSD_IN], ((0, 0), (0, LANES - 2 * SSD_HEADS))).astype(BF16)
        wf = w_in[:, SSD_IN:SSD_IN + FNET_WIDTH].astype(BF16)
        wuv = w_in[:, SSD_IN + FNET_WIDTH:].astype(BF16)
        z, xbc, dt, fr, fi, uv = _inproj(x, wa, wdt, wf, wuv, mc)

        lane_pad = LANES - 2 * SSD_HEADS
        dtb = jnp.pad(p["dt_bias"][l].reshape(1, -1), ((0, 0), (0, lane_pad)))
        a_neg = jnp.pad(-jnp.exp(p["a_log"][l].astype(F32)).reshape(1, -1), ((0, 0), (0, lane_pad)))
        dskip_w = jnp.repeat(p["d_skip"][l], SSD_HEAD_DIM).reshape(1, SSD_WIDTH)
        normg = p["ssd_norm_g"][l].reshape(1, SSD_WIDTH)
        convw = p["conv_w"][l]
        convb = p["conv_b"][l].reshape(1, SSD_XBC)
        w_bd = jax.scipy.linalg.block_diag(*[p["fnet_w"][l, g] for g in range(FNET_GROUPS)]).astype(BF16)
        b_f = p["fnet_b"][l].reshape(1, FNET_WIDTH)

        ssd_parts, fn_parts = [], []
        row0 = 0
        for (B, L) in trunks:
            ssd_parts.append(_ssd_mixer(z, xbc, dt, row0, B, L, convw, convb, dtb, a_neg, dskip_w, normg))
            fn_parts.append(_fourier_mixer(fr, fi, row0, B, L, w_bd, b_f))
            row0 += B * L
        ssd = jnp.concatenate(ssd_parts, axis=0)
        fn = jnp.concatenate(fn_parts, axis=0)

        ws_cat = jnp.concatenate([p["gmlp_ws"][l, h] for h in range(GMLP_HEADS)], axis=1).astype(BF16)
        bs_wide = jnp.repeat(p["gmlp_bs"][l].T, GMLP_HDIM, axis=1)
        gm = _gmlp_mixer(uv, p["gmlp_ln_g"][l].reshape(1, -1), p["gmlp_ln_b"][l].reshape(1, -1),
                         ws_cat, bs_wide)

        w_r = jnp.pad(p["router_w"][l], ((0, 0), (0, LANES - N_EXPERTS)))
        wrh = w_r.astype(BF16)
        wrl = (w_r - wrh.astype(F32)).astype(BF16)
        b_r = jnp.pad(p["router_b"][l].reshape(1, -1), ((0, 0), (0, LANES - N_EXPERTS)),
                      constant_values=NEG_BIG)
        x1, idx, gates, counts = _outproj(
            x, ssd, fn, gm, p["w_out"][l].astype(BF16), p["b_out"][l].reshape(1, D),
            p["ln1_g"][l].reshape(1, D), p["ln1_b"][l].reshape(1, D), wrh, wrl, b_r)

        x = _moe(x1, idx, gates, counts, l, p["exp_w_gu"], p["exp_b_gu"], p["exp_w_dn"], p["exp_b_dn"],
                 p["ln2_g"][l], p["ln2_b"][l])
    return x


def kernel(x_prompt, x_sample, emb_ln_g, emb_ln_b, w_in, conv_w, conv_b, dt_bias, a_log, d_skip, ssd_norm_g, fnet_w, fnet_b, gmlp_ln_g, gmlp_ln_b, gmlp_ws, gmlp_bs, w_out, b_out, ln1_g, ln1_b, router_w, router_b, exp_w_gu, exp_b_gu, exp_w_dn, exp_b_dn, ln2_g, ln2_b):
    p = dict(emb_ln_g=emb_ln_g, emb_ln_b=emb_ln_b, w_in=w_in, conv_w=conv_w, conv_b=conv_b,
             dt_bias=dt_bias, a_log=a_log, d_skip=d_skip, ssd_norm_g=ssd_norm_g, fnet_w=fnet_w,
             fnet_b=fnet_b, gmlp_ln_g=gmlp_ln_g, gmlp_ln_b=gmlp_ln_b, gmlp_ws=gmlp_ws, gmlp_bs=gmlp_bs,
             w_out=w_out, b_out=b_out, ln1_g=ln1_g, ln1_b=ln1_b, router_w=router_w, router_b=router_b,
             exp_w_gu=exp_w_gu, exp_b_gu=exp_b_gu, exp_w_dn=exp_w_dn, exp_b_dn=exp_b_dn,
             ln2_g=ln2_g, ln2_b=ln2_b)
    D = x_prompt.shape[-1]
    trunks = [x_prompt.shape[:2], x_sample.shape[:2]]
    y = _trunk([x_prompt.reshape(-1, D), x_sample.reshape(-1, D)], trunks, p)
    n_p = x_prompt.shape[0] * x_prompt.shape[1]
    return (y[:n_p].reshape(x_prompt.shape), y[n_p:].reshape(x_sample.shape))
```

```python
import functools
import math

import numpy as np
import jax
import jax.numpy as jnp
from jax import lax
from jax.experimental import pallas as pl
from jax.experimental.pallas import tpu as pltpu

F32 = jnp.float32
BF16 = jnp.bfloat16

D_MODEL = 1024
DEPTH = 4
SSD_WIDTH = 512
SSD_HEAD_DIM = 64
SSD_HEADS = 8
SSD_GROUPS = 2
SSD_STATE = 64
SSD_CONV = 5
SSD_GN = SSD_GROUPS * SSD_STATE
SSD_XBC = SSD_WIDTH + 2 * SSD_GN
SSD_IN = SSD_WIDTH + SSD_XBC + 2 * SSD_HEADS
FNET_WIDTH = 256
FNET_GROUPS = 4
FNET_GDIM = 64
GMLP_WIDTH = 256
GMLP_HEADS = 4
GMLP_HDIM = 64
N_EXPERTS = 32
TOP_K = 4
D_EXPERT = D_MODEL
SWIGLU_LIMIT = 7.0
SWIGLU_ALPHA = 1.702
LN_EPS = 1e-5
RMS_EPS = 1e-5
DN_ALPHA = (2 * DEPTH) ** 0.25

LANES = 128
CHUNK = 128
HALO = 16
VMEM_LIMIT = 56 * 1024 * 1024

TOK_TILE = 512
EXPERT_TILE = 512
EXPERT_COL_CHUNK = 512
DISPATCH_TILE = 512
COMBINE_TILE = 256
COMBINE_SUB = 32
COMBINE_LAG = 2
ROW_UNROLL = 8
SSD_STEP_CHUNKS = 4
FNET_ROW_GROUP = 16
NEG_BIG = -1e30


def _dot(a, b):
    return jnp.dot(a, b, preferred_element_type=F32)


def _split2(v):
    hi = v.astype(BF16)
    lo = (v - hi.astype(F32)).astype(BF16)
    return hi, lo


def _split3(v):
    hi = v.astype(BF16)
    r = v - hi.astype(F32)
    mid = r.astype(BF16)
    lo = (r - mid.astype(F32)).astype(BF16)
    return hi, mid, lo


def _ln_rows(xf, g, b):
    mu = jnp.mean(xf, -1, keepdims=True)
    xc = xf - mu
    var = jnp.mean(xc * xc, -1, keepdims=True)
    return xc * lax.rsqrt(var + LN_EPS) * g + b


def _params(n_parallel=1, n_arbitrary=0):
    sem = ("parallel",) * n_parallel + ("arbitrary",) * n_arbitrary
    return pltpu.CompilerParams(dimension_semantics=sem, vmem_limit_bytes=VMEM_LIMIT)


def _full(shape):
    nd = len(shape)
    return pl.BlockSpec(shape, lambda *_: (0,) * nd)


def _embed_ln_kernel(xa_ref, xb_ref, g_ref, b_ref, o_ref, *, tiles_a):
    i = pl.program_id(0)

    @pl.when(i < tiles_a)
    def _():
        o_ref[...] = _ln_rows(xa_ref[...], g_ref[...], b_ref[...])

    @pl.when(i >= tiles_a)
    def _():
        o_ref[...] = _ln_rows(xb_ref[...], g_ref[...], b_ref[...])


def _embed_ln(xa, xb, g, b):
    D = xa.shape[1]
    na, nb = xa.shape[0] // TOK_TILE, xb.shape[0] // TOK_TILE
    return pl.pallas_call(
        functools.partial(_embed_ln_kernel, tiles_a=na),
        grid=(na + nb,),
        in_specs=[pl.BlockSpec((TOK_TILE, D), lambda i: (jnp.minimum(i, na - 1), 0)),
                  pl.BlockSpec((TOK_TILE, D), lambda i: (jnp.maximum(i - na, 0), 0)),
                  _full((1, D)), _full((1, D))],
        out_specs=pl.BlockSpec((TOK_TILE, D), lambda i: (i, 0)),
        out_shape=jax.ShapeDtypeStruct(((na + nb) * TOK_TILE, D), F32),
        compiler_params=_params(1),
        name="embed_ln",
    )(xa, xb, g.reshape(1, D), b.reshape(1, D))


def _inproj_kernel(x_ref, wa_ref, wdt_ref, wf_ref, wuv_ref, mc_ref,
                   z_ref, xbc_ref, dt_ref, fr_ref, fi_ref, uv_ref):
    xb = x_ref[...].astype(BF16)
    a = _dot(xb, wa_ref[...])
    z_ref[...] = a[:, :SSD_WIDTH].astype(BF16)
    xbc_ref[...] = a[:, SSD_WIDTH:].astype(BF16)
    dt_ref[...] = _dot(xb, wdt_ref[...])
    f = _dot(xb, wf_ref[...]).astype(BF16)
    fri = _dot(f, mc_ref[...])
    fr_ref[...] = fri[:, :FNET_WIDTH]
    fi_ref[...] = fri[:, FNET_WIDTH:]
    uv_ref[...] = _dot(xb, wuv_ref[...]).astype(BF16)


def _inproj(x, wa, wdt, wf, wuv, mc):
    T, D = x.shape
    tm = TOK_TILE
    row = lambda w: pl.BlockSpec((tm, w), lambda i: (i, 0))
    return pl.pallas_call(
        _inproj_kernel,
        grid=(T // tm,),
        in_specs=[row(D), _full(wa.shape), _full(wdt.shape), _full(wf.shape), _full(wuv.shape),
                  _full(mc.shape)],
        out_specs=[row(SSD_WIDTH), row(SSD_XBC), row(LANES), row(FNET_WIDTH), row(FNET_WIDTH),
                   row(2 * GMLP_WIDTH)],
        out_shape=[jax.ShapeDtypeStruct((T, SSD_WIDTH), BF16),
                   jax.ShapeDtypeStruct((T, SSD_XBC), BF16),
                   jax.ShapeDtypeStruct((T, LANES), F32),
                   jax.ShapeDtypeStruct((T, FNET_WIDTH), F32),
                   jax.ShapeDtypeStruct((T, FNET_WIDTH), F32),
                   jax.ShapeDtypeStruct((T, 2 * GMLP_WIDTH), BF16)],
        compiler_params=_params(1),
        name="inproj",
    )(x, wa, wdt, wf, wuv, mc)


def _conv_silu(xc_ref, xp_ref, xn_ref, convw_ref, convb_ref, first, last):
    Q = xc_ref.shape[0]
    cur = xc_ref[...].astype(F32)
    prev = jnp.where(first, 0.0, xp_ref[...].astype(F32))
    nxt = jnp.where(last, 0.0, xn_ref[...].astype(F32))
    ext = jnp.concatenate([prev[HALO - 8:], cur, nxt[:8]], axis=0)
    n_ext = Q + 16
    pad = SSD_CONV // 2
    conv = convb_ref[...]
    for k in range(SSD_CONV):
        shift = (pad - k) % n_ext
        tap = ext if shift == 0 else pltpu.roll(ext, shift, axis=0)
        conv = conv + tap[8:8 + Q] * convw_ref[k:k + 1, :]
    return conv * jax.nn.sigmoid(conv)


def _ssd_chunk(act, dt_raw, dtb_ref, a_ref, cum_ref, expand_ref, state_ref, *, backward):
    Q = CHUNK
    h_off = SSD_HEADS if backward else 0
    xs = act[:, :SSD_WIDTH]
    bm = act[:, SSD_WIDTH:SSD_WIDTH + SSD_GN]
    cm = act[:, SSD_WIDTH + SSD_GN:]

    dt = jax.nn.softplus(dt_raw + dtb_ref[...])
    adt = dt * a_ref[...]
    cum = cum_ref[...]
    h3 = _split3(adt)
    acum = _dot(cum, h3[0]) + _dot(cum, h3[1]) + _dot(cum, h3[2])
    acum_t = acum.T
    total = acum[0:1, :] if backward else acum[Q - 1:Q, :]
    dec_in = jnp.exp(acum)
    dec_end = jnp.exp(total - acum)

    expand = expand_ref[...]

    def widen(v):
        hi, lo = _split2(v)
        return _dot(hi, expand) + _dot(lo, expand)

    dt_w = widen(dt)
    dec_in_w = widen(dec_in)
    dec_end_w = widen(dec_end)
    tdec_w = dec_in_w[0:1, :] if backward else dec_in_w[Q - 1:Q, :]

    xdt = xs * dt_w

    row = lax.broadcasted_iota(jnp.int32, (Q, Q), 0)
    col = lax.broadcasted_iota(jnp.int32, (Q, Q), 1)
    mask = (col > row) if backward else (col <= row)
    lane = lax.broadcasted_iota(jnp.int32, (Q, LANES), 1)
    lo_half = lane < SSD_STATE

    bm_b = bm.astype(BF16)
    cb = []
    for g in range(SSD_GROUPS):
        cm_g = jnp.where(lo_half if g == 0 else jnp.logical_not(lo_half), cm, 0.0).astype(BF16)
        cb.append(lax.dot_general(cm_g, bm_b, (((1,), (1,)), ((), ())),
                                  preferred_element_type=F32))

    y_parts = []
    for pair in range(SSD_HEADS // 2):
        gmats = []
        for r in range(2):
            h = 2 * pair + r
            hl = h_off + h
            diff = acum[:, hl:hl + 1] - acum_t[hl:hl + 1, :]
            dec = jnp.exp(jnp.where(mask, diff, NEG_BIG))
            gmats.append((cb[h // (SSD_HEADS // SSD_GROUPS)] * dec).astype(BF16))
        lhs = jnp.concatenate(gmats, axis=1)
        x2 = xdt[:, pair * LANES:(pair + 1) * LANES]
        rhs = jnp.concatenate([jnp.where(lo_half, x2, 0.0), jnp.where(lo_half, 0.0, x2)],
                              axis=0).astype(BF16)
        y_parts.append(_dot(lhs, rhs))
    y_diag = jnp.concatenate(y_parts, axis=1)

    srow = lax.broadcasted_iota(jnp.int32, (SSD_GN, SSD_WIDTH), 0)
    scol = lax.broadcasted_iota(jnp.int32, (SSD_GN, SSD_WIDTH), 1)
    on_group = (srow // SSD_STATE) == (scol // (SSD_WIDTH // SSD_GROUPS))
    st_new = _dot(bm.T.astype(BF16), (xdt * dec_end_w).astype(BF16))
    st_new = jnp.where(on_group, st_new, 0.0)
    st_prev = state_ref[...]
    y_off = _dot(cm.astype(BF16), st_prev.astype(BF16)) * dec_in_w
    state_ref[...] = st_prev * tdec_w + st_new
    return y_diag + y_off, xs


def _ssd_bwd_kernel(xc_ref, xp_ref, xn_ref, dt_ref, convw_ref, convb_ref, dtb_ref, a_ref,
                    cum_ref, expand_ref, yb_ref, act_ref, state_ref, *, n_steps):
    c = pl.program_id(1)
    step = n_steps - 1 - c

    @pl.when(c == 0)
    def _():
        state_ref[...] = jnp.zeros_like(state_ref)

    act = _conv_silu(xc_ref, xp_ref, xn_ref, convw_ref, convb_ref,
                     first=step == 0, last=step == n_steps - 1)
    act_ref[...] = act.astype(BF16)
    for s in reversed(range(SSD_STEP_CHUNKS)):
        rows = slice(s * CHUNK, (s + 1) * CHUNK)
        y, _ = _ssd_chunk(act[rows], dt_ref[rows, :], dtb_ref, a_ref, cum_ref, expand_ref, state_ref,
                          backward=True)
        yb_ref[rows, :] = y


def _ssd_fwd_kernel(act_ref, dt_ref, z_ref, yb_ref, dtb_ref, a_ref, cum_ref, expand_ref,
                    dskip_ref, normg_ref, o_ref, state_ref):
    c = pl.program_id(1)

    @pl.when(c == 0)
    def _():
        state_ref[...] = jnp.zeros_like(state_ref)

    gw = SSD_WIDTH // SSD_GROUPS
    for s in range(SSD_STEP_CHUNKS):
        rows = slice(s * CHUNK, (s + 1) * CHUNK)
        y, xs = _ssd_chunk(act_ref[rows, :].astype(F32), dt_ref[rows, :], dtb_ref, a_ref, cum_ref,
                           expand_ref, state_ref, backward=False)
        z = z_ref[rows, :].astype(F32)
        y = (y + yb_ref[rows, :] + xs * dskip_ref[...]) * (z * jax.nn.sigmoid(z))
        outs = []
        for g in range(SSD_GROUPS):
            yg = y[:, g * gw:(g + 1) * gw]
            outs.append(yg * lax.rsqrt(jnp.mean(yg * yg, -1, keepdims=True) + RMS_EPS))
        o_ref[rows, :] = (jnp.concatenate(outs, axis=1) * normg_ref[...]).astype(BF16)


def _ssd_mixer(z, xbc, dt, row0, B, L, convw, convb, dtb, a_neg, dskip_w, normg):
    R = SSD_STEP_CHUNKS * CHUNK
    C = L // R
    c0 = row0 // R
    hpc = R // HALO
    n_halo = z.shape[0] // HALO
    consts = _ssd_consts()

    def specs(chunk_of):
        cur = lambda w: pl.BlockSpec((R, w), lambda b, c: (c0 + b * C + chunk_of(c), 0))
        prv = pl.BlockSpec((HALO, SSD_XBC),
                           lambda b, c: (jnp.maximum((c0 + b * C + chunk_of(c)) * hpc - 1, 0), 0))
        nxt = pl.BlockSpec((HALO, SSD_XBC),
                           lambda b, c: (jnp.minimum((c0 + b * C + chunk_of(c) + 1) * hpc, n_halo - 1), 0))
        return cur, prv, nxt

    out_row = lambda chunk_of, w: pl.BlockSpec((R, w), lambda b, c: (b * C + chunk_of(c), 0))

    rev = lambda c: C - 1 - c
    cur, prv, nxt = specs(rev)
    yb, act = pl.pallas_call(
        functools.partial(_ssd_bwd_kernel, n_steps=C),
        grid=(B, C),
        in_specs=[cur(SSD_XBC), prv, nxt, cur(LANES), _full(convw.shape), _full(convb.shape),
                  _full(dtb.shape), _full(a_neg.shape),
                  _full(consts["cum_b"].shape), _full(consts["expand_b"].shape)],
        out_specs=[out_row(rev, SSD_WIDTH), out_row(rev, SSD_XBC)],
        out_shape=[jax.ShapeDtypeStruct((B * L, SSD_WIDTH), F32),
                   jax.ShapeDtypeStruct((B * L, SSD_XBC), BF16)],
        scratch_shapes=[pltpu.VMEM((SSD_GN, SSD_WIDTH), F32)],
        compiler_params=_params(1, 1),
        name="ssd_bwd",
    )(xbc, xbc, xbc, dt, convw, convb, dtb, a_neg, consts["cum_b"], consts["expand_b"])

    fwd = lambda c: c
    cur, _, _ = specs(fwd)
    return pl.pallas_call(
        _ssd_fwd_kernel,
        grid=(B, C),
        in_specs=[out_row(fwd, SSD_XBC), cur(LANES), cur(SSD_WIDTH), out_row(fwd, SSD_WIDTH),
                  _full(dtb.shape), _full(a_neg.shape),
                  _full(consts["cum_f"].shape), _full(consts["expand_f"].shape),
                  _full(dskip_w.shape), _full(normg.shape)],
        out_specs=out_row(fwd, SSD_WIDTH),
        out_shape=jax.ShapeDtypeStruct((B * L, SSD_WIDTH), BF16),
        scratch_shapes=[pltpu.VMEM((SSD_GN, SSD_WIDTH), F32)],
        compiler_params=_params(1, 1),
        name="ssd_fwd",
    )(act, dt, z, yb, dtb, a_neg, consts["cum_f"], consts["expand_f"], dskip_w, normg)


@functools.lru_cache(maxsize=None)
def _ssd_consts_np():
    q = np.arange(CHUNK)
    cum_f = (q[None, :] <= q[:, None]).astype(np.float32)
    cum_b = (q[None, :] >= q[:, None]).astype(np.float32)
    out = {"cum_f": cum_f, "cum_b": cum_b}
    for name, off in (("expand_f", 0), ("expand_b", SSD_HEADS)):
        e = np.zeros((LANES, SSD_WIDTH), np.float32)
        for h in range(SSD_HEADS):
            e[off + h, h * SSD_HEAD_DIM:(h + 1) * SSD_HEAD_DIM] = 1.0
        out[name] = e
    return out


def _ssd_consts():
    return {k: jnp.asarray(v, BF16) for k, v in _ssd_consts_np().items()}


@functools.lru_cache(maxsize=None)
def _fnet_tables_np(L):
    L2 = CHUNK
    L1 = L // L2
    k1 = np.arange(L1)
    ang1 = 2.0 * np.pi * ((k1[:, None] * k1[None, :]) % L1) / L1
    f1 = np.concatenate([np.cos(ang1), np.sin(ang1)], axis=0) / math.sqrt(L1)
    k = k1[:, None, None] + L1 * np.arange(L2)[None, :, None]
    l2 = np.arange(L2)[None, None, :]
    ang2 = 2.0 * np.pi * ((k * l2) % L) / L
    gc = np.cos(ang2) / math.sqrt(L2)
    gs = np.sin(ang2) / math.sqrt(L2)
    return f1.astype(np.float32), gc.astype(np.float32), gs.astype(np.float32)


@functools.lru_cache(maxsize=None)
def _fnet_channel_np():
    c = np.arange(FNET_GDIM)
    ang = 2.0 * np.pi * ((c[:, None] * c[None, :]) % FNET_GDIM) / FNET_GDIM
    cc = np.cos(ang) / math.sqrt(FNET_GDIM)
    sc = np.sin(ang) / math.sqrt(FNET_GDIM)
    eye = np.eye(FNET_GROUPS)
    return np.concatenate([np.kron(eye, cc), np.kron(eye, -sc)], axis=1).astype(np.float32)


def _fnet_stage1_kernel(f_ref, xr_ref, xi_ref, yr_ref, yi_ref, *, L1, group):
    W = FNET_WIDTH
    f = f_ref[...]
    for j in range(group):
        x = jnp.concatenate([xr_ref[:, j, :], xi_ref[:, j, :]], axis=1).astype(BF16)
        p = _dot(f, x)
        yr_ref[:, j, :] = p[:L1, :W] + p[L1:, W:]
        yi_ref[:, j, :] = p[:L1, W:] - p[L1:, :W]


def _fnet_stage2_kernel(yr_ref, yi_ref, g_ref, w_ref, b_ref, o_ref, *, group):
    for j in range(group):
        y = jnp.concatenate([yr_ref[j], yi_ref[j]], axis=0).astype(BF16)
        zr = _dot(g_ref[j], y)
        o_ref[:, j, :] = _dot(zr.astype(BF16), w_ref[...]) + b_ref[...]


def _fourier_mixer(fr, fi, row0, B, L, w_bd, b_f):
    L2 = CHUNK
    L1 = L // L2
    W = FNET_WIDTH
    f1, gc, gs = _fnet_tables_np(L)
    f1 = jnp.asarray(f1, BF16)
    g_cat = jnp.asarray(np.concatenate([gc, gs], axis=2), BF16)
    xr = fr.reshape(-1, L2, W)
    xi = fi.reshape(-1, L2, W)
    blk0 = row0 // L2 // L1
    grp1 = FNET_ROW_GROUP
    xblk = pl.BlockSpec((L1, grp1, W), lambda b, j: (blk0 + b, j, 0))
    yblk = pl.BlockSpec((L1, grp1, W), lambda b, j: (b, j, 0))
    yr, yi = pl.pallas_call(
        functools.partial(_fnet_stage1_kernel, L1=L1, group=grp1),
        grid=(B, L2 // grp1),
        in_specs=[_full(f1.shape), xblk, xblk],
        out_specs=[yblk, yblk],
        out_shape=[jax.ShapeDtypeStruct((B * L1, L2, W), F32)] * 2,
        compiler_params=_params(2),
        name="fnet_stage1",
    )(f1, xr, xi)
    grp2 = FNET_ROW_GROUP // 2
    n2 = L1 // grp2
    yblk2 = pl.BlockSpec((grp2, L2, W), lambda b, k: (b * n2 + k, 0, 0))
    gblk = pl.BlockSpec((grp2, L2, 2 * L2), lambda b, k: (k, 0, 0))
    out = pl.pallas_call(
        functools.partial(_fnet_stage2_kernel, group=grp2),
        grid=(B, n2),
        in_specs=[yblk2, yblk2, gblk, _full(w_bd.shape), _full(b_f.shape)],
        out_specs=pl.BlockSpec((L2, grp2, W), lambda b, k: (b, k, 0)),
        out_shape=jax.ShapeDtypeStruct((B * L2, L1, W), F32),
        compiler_params=_params(2),
        name="fnet_stage2",
    )(yr, yi, g_cat, w_bd, b_f)
    return out.reshape(B * L, W)


def _gmlp_kernel(uv_ref, g_ref, b_ref, ws_ref, bs_ref, o_ref, *, n_chunks):
    uv = uv_ref[...].astype(F32)
    ge = 0.5 * uv * (1.0 + lax.erf(uv * (1.0 / math.sqrt(2.0))))
    u = ge[:, :GMLP_WIDTH]
    v = _ln_rows(ge[:, GMLP_WIDTH:], g_ref[...], b_ref[...])
    lane = lax.broadcasted_iota(jnp.int32, (CHUNK, GMLP_WIDTH), 1)
    ws = ws_ref[...]
    bs = bs_ref[...]
    for j in range(n_chunks):
        vj = v[j * CHUNK:(j + 1) * CHUNK]
        rhs = jnp.concatenate(
            [jnp.where(lane // GMLP_HDIM == h, vj, 0.0) for h in range(GMLP_HEADS)], axis=0)
        sv = _dot(ws, rhs.astype(BF16)) + bs
        o_ref[j * CHUNK:(j + 1) * CHUNK, :] = (u[j * CHUNK:(j + 1) * CHUNK] * sv).astype(BF16)


def _gmlp_mixer(uv, ln_g, ln_b, ws_cat, bs_wide):
    T = uv.shape[0]
    tm = TOK_TILE
    return pl.pallas_call(
        functools.partial(_gmlp_kernel, n_chunks=tm // CHUNK),
        grid=(T // tm,),
        in_specs=[pl.BlockSpec((tm, 2 * GMLP_WIDTH), lambda i: (i, 0)), _full(ln_g.shape),
                  _full(ln_b.shape), _full(ws_cat.shape), _full(bs_wide.shape)],
        out_specs=pl.BlockSpec((tm, GMLP_WIDTH), lambda i: (i, 0)),
        out_shape=jax.ShapeDtypeStruct((T, GMLP_WIDTH), BF16),
        compiler_params=_params(1),
        name="gmlp",
    )(uv, ln_g, ln_b, ws_cat, bs_wide)


def _outproj_kernel(x_ref, ssd_ref, fn_ref, gm_ref, wo_ref, bo_ref, g_ref, b_ref,
                    wrh_ref, wrl_ref, br_ref, tri_ref, x1_ref, route_ref, gate_ref, cnt_ref, seen_ref):
    @pl.when(pl.program_id(0) == 0)
    def _():
        seen_ref[...] = jnp.zeros_like(seen_ref)

    acc = _dot(ssd_ref[...], wo_ref[0:SSD_WIDTH, :])
    acc = acc + _dot(fn_ref[...].astype(BF16), wo_ref[SSD_WIDTH:SSD_WIDTH + FNET_WIDTH, :])
    acc = acc + _dot(gm_ref[...], wo_ref[SSD_WIDTH + FNET_WIDTH:, :])
    h = DN_ALPHA * x_ref[...] + acc + bo_ref[...]
    x1 = _ln_rows(h, g_ref[...], b_ref[...])
    x1_ref[...] = x1

    xh, xl = _split2(x1)
    wrh = wrh_ref[...]
    logits = _dot(xh, wrh) + _dot(xl, wrh) + _dot(xh, wrl_ref[...]) + br_ref[...]
    work = logits.T[:N_EXPERTS]
    n_tok = work.shape[1]
    row = lax.broadcasted_iota(jnp.int32, work.shape, 0)
    vals, idxs = [], []
    for _ in range(TOP_K):
        m = jnp.max(work, axis=0, keepdims=True)
        i = jnp.min(jnp.where(work == m, row, N_EXPERTS), axis=0, keepdims=True)
        vals.append(m)
        idxs.append(i)
        work = jnp.where(row == i, -jnp.inf, work)
    exps = [jnp.exp(v - vals[0]) for v in vals]
    denom = exps[0]
    for e in exps[1:]:
        denom = denom + e

    onehots = [row == i for i in idxs]
    chosen = jnp.zeros(work.shape, F32)
    for oh in onehots:
        chosen = jnp.where(oh, 1.0, chosen)
    seen = seen_ref[...]
    before = _dot(chosen.astype(BF16), tri_ref[...]) + seen[:, 0:1]
    seen = seen + jnp.sum(chosen, axis=1, keepdims=True)
    seen_ref[...] = seen
    cnt_ref[...] = seen

    row8 = lax.broadcasted_iota(jnp.int32, (2 * TOP_K, n_tok), 0)
    route = jnp.zeros((2 * TOP_K, n_tok), jnp.int32)
    gates_t = jnp.zeros((2 * TOP_K, n_tok), F32)
    for k in range(TOP_K):
        rank = jnp.sum(jnp.where(onehots[k], before, 0.0), axis=0, keepdims=True).astype(jnp.int32)
        route = jnp.where(row8 == k, idxs[k], route)
        route = jnp.where(row8 == TOP_K + k, rank, route)
        gates_t = jnp.where(row8 == k, exps[k] / denom, gates_t)
    route_ref[...] = route
    pad = jnp.zeros((LANES - 2 * TOP_K, n_tok), F32)
    gate_ref[...] = jnp.concatenate([gates_t, pad], axis=0).T


def _outproj(x, ssd, fn, gm, wo, bo, g, b, wrh, wrl, br):
    T, D = x.shape
    tm = TOK_TILE
    row = lambda w: pl.BlockSpec((tm, w), lambda i: (i, 0))
    q = np.arange(tm)
    tri = jnp.asarray((q[:, None] < q[None, :]).astype(np.float32), BF16)
    return pl.pallas_call(
        _outproj_kernel,
        grid=(T // tm,),
        in_specs=[row(D), row(SSD_WIDTH), row(FNET_WIDTH), row(GMLP_WIDTH), _full(wo.shape),
                  _full(bo.shape), _full(g.shape), _full(b.shape), _full(wrh.shape),
                  _full(wrl.shape), _full(br.shape), _full(tri.shape)],
        out_specs=[row(D), pl.BlockSpec((2 * TOP_K, tm), lambda i: (0, i)), row(LANES),
                   _full((N_EXPERTS, LANES))],
        out_shape=[jax.ShapeDtypeStruct((T, D), F32),
                   jax.ShapeDtypeStruct((2 * TOP_K, T), jnp.int32),
                   jax.ShapeDtypeStruct((T, LANES), F32),
                   jax.ShapeDtypeStruct((N_EXPERTS, LANES), F32)],
        scratch_shapes=[pltpu.VMEM((N_EXPERTS, LANES), F32)],
        compiler_params=_params(0, 1),
        name="outproj_ln_router",
    )(x, ssd, fn, gm, wo, bo, g, b, wrh, wrl, br, tri)


SUB = D_MODEL // LANES


def _to_slot_major(dst_ref, dst_row0, x, n):
    for c in range(SUB):
        dst_ref[pl.ds(dst_row0 + c, n, stride=SUB), :] = x[:, c * LANES:(c + 1) * LANES]


def _from_slot_major(src_ref, src_row0, n):
    return jnp.concatenate([src_ref[pl.ds(src_row0 + c, n, stride=SUB), :] for c in range(SUB)], axis=1)


def _slot_copy(src, dst, sem, src_slot, dst_slot):
    s0 = pl.multiple_of(src_slot * SUB, SUB)
    d0 = pl.multiple_of(dst_slot * SUB, SUB)
    return pltpu.make_async_copy(src.at[pl.ds(s0, SUB), :], dst.at[pl.ds(d0, SUB), :], sem)


def _slots_wait(src, dst, sem, n_slots):
    pltpu.make_async_copy(src.at[pl.ds(0, n_slots * SUB), :], dst.at[pl.ds(0, n_slots * SUB), :], sem).wait()


def _dispatch_kernel(pos_ref, x_ref, xbuf_hbm, xs_ref, sem, *, rows):
    _to_slot_major(xs_ref, 0, x_ref[...], rows)

    def body(j, carry):
        for u in range(ROW_UNROLL):
            r = j * ROW_UNROLL + u
            for k in range(TOP_K):
                _slot_copy(xs_ref, xbuf_hbm, sem, r, pos_ref[k * rows + r]).start(priority=k % 2)
        return carry

    lax.fori_loop(0, rows // ROW_UNROLL, body, 0)
    for _ in range(TOP_K):
        _slots_wait(xs_ref, xbuf_hbm, sem, rows)


def _dispatch(x, pos_flat, n_slots):
    T, D = x.shape
    R = DISPATCH_TILE
    return pl.pallas_call(
        functools.partial(_dispatch_kernel, rows=R),
        grid=(T // R,),
        in_specs=[pl.BlockSpec((R * TOP_K,), lambda i: (i,), memory_space=pltpu.SMEM),
                  pl.BlockSpec((R, D), lambda i: (i, 0))],
        out_specs=pl.BlockSpec(memory_space=pl.ANY),
        out_shape=jax.ShapeDtypeStruct((n_slots * SUB, LANES), F32),
        scratch_shapes=[pltpu.VMEM((R * SUB, LANES), F32), pltpu.SemaphoreType.DMA(())],
        compiler_params=_params(1),
        name="moe_dispatch",
    )(pos_flat, x)


def _expert_kernel(blk_e_ref, valid_ref, x_ref, wgu_ref, bgu_ref, wdn_ref, bdn_ref, y_ref, wgu_b, wdn_b):
    i = pl.program_id(0)
    valid = valid_ref[i]
    prev_e = blk_e_ref[jnp.maximum(i - 1, 0)]
    new_expert = jnp.logical_or(i == 0, blk_e_ref[i] != prev_e)

    @pl.when(jnp.logical_and(valid > 0, new_expert))
    def _():
        wgu_b[...] = wgu_ref[...].astype(BF16)
        wdn_b[...] = wdn_ref[...].astype(BF16)

    @pl.when(valid > 0)
    def _():
        x = _from_slot_major(x_ref, 0, EXPERT_TILE)
        row = lax.broadcasted_iota(jnp.int32, x.shape, 0)
        xb = jnp.where(row < valid, x, 0.0).astype(BF16)
        acc = jnp.broadcast_to(bdn_ref[...], x.shape)
        nc = EXPERT_COL_CHUNK
        for j in range(D_EXPERT // nc):
            g = _dot(xb, wgu_b[:, j * nc:(j + 1) * nc]) + bgu_ref[:, j * nc:(j + 1) * nc]
            u = (_dot(xb, wgu_b[:, D_EXPERT + j * nc:D_EXPERT + (j + 1) * nc])
                 + bgu_ref[:, D_EXPERT + j * nc:D_EXPERT + (j + 1) * nc])
            g = jnp.minimum(g, SWIGLU_LIMIT)
            u = jnp.clip(u, -SWIGLU_LIMIT, SWIGLU_LIMIT)
            act = (u + 1.0) * (g * jax.nn.sigmoid(SWIGLU_ALPHA * g))
            acc = acc + _dot(act.astype(BF16), wdn_b[j * nc:(j + 1) * nc, :])
        _to_slot_major(y_ref, 0, acc, EXPERT_TILE)

    @pl.when(valid <= 0)
    def _():
        y_ref[...] = jnp.zeros_like(y_ref)


def _expert_mlp(x_buf, blk_e, blk_valid, layer, wgu, bgu, wdn, bdn):
    D = D_MODEL
    tm = EXPERT_TILE
    per_expert = lambda r, c: pl.BlockSpec((None, None, r, c), lambda i, be, bv: (layer, be[i], 0, 0))
    slots = pl.BlockSpec((tm * SUB, LANES), lambda i, be, bv: (i, 0))
    grid_spec = pltpu.PrefetchScalarGridSpec(
        num_scalar_prefetch=2,
        grid=(x_buf.shape[0] // (tm * SUB),),
        in_specs=[slots, per_expert(D, 2 * D_EXPERT), per_expert(1, 2 * D_EXPERT),
                  per_expert(D_EXPERT, D), per_expert(1, D)],
        out_specs=slots,
        scratch_shapes=[pltpu.VMEM((D, 2 * D_EXPERT), BF16), pltpu.VMEM((D_EXPERT, D), BF16)],
    )
    return pl.pallas_call(
        _expert_kernel,
        grid_spec=grid_spec,
        out_shape=jax.ShapeDtypeStruct(x_buf.shape, F32),
        compiler_params=_params(0, 1),
        name="moe_expert_mlp",
    )(blk_e, blk_valid, x_buf, wgu, bgu, wdn, bdn)


def _combine_kernel(pos_ref, x_ref, gate_ref, y_hbm, g_ref, b_ref, o_ref, *scratch, rows, n_tiles):
    n_slots = COMBINE_LAG + 1
    bufs, sem = scratch[:n_slots], scratch[n_slots]
    i = pl.program_id(0)
    sub = COMBINE_SUB

    def run(s_issue, s_done):
        def issue(j):
            for u in range(sub):
                r = j * sub + u
                for k in range(TOP_K):
                    _slot_copy(y_hbm, bufs[s_issue].at[k], sem.at[s_issue], pos_ref[k * rows + r], r).start(
                        priority=k % 2)

        def finish(j):
            rs = pl.ds(pl.multiple_of(j * sub, sub), sub)
            row0 = pl.multiple_of(j * (sub * SUB), sub * SUB)
            gates = gate_ref[rs, :]
            y = gates[:, 0:1] * _from_slot_major(bufs[s_done].at[0], row0, sub)
            for k in range(1, TOP_K):
                y = y + gates[:, k:k + 1] * _from_slot_major(bufs[s_done].at[k], row0, sub)
            o_ref[rs, :] = _ln_rows(DN_ALPHA * x_ref[rs, :] + y, g_ref[...], b_ref[...])

        def wait_done():
            for k in range(TOP_K):
                _slots_wait(y_hbm, bufs[s_done].at[k], sem.at[s_done], rows)

        def loop(do_issue, do_finish):
            def body(j, carry):
                if do_issue:
                    issue(j)
                if do_finish:
                    finish(j)
                return carry

            lax.fori_loop(0, rows // sub, body, 0)

        @pl.when(i < COMBINE_LAG)
        def _():
            loop(True, False)

        @pl.when(jnp.logical_and(i >= COMBINE_LAG, i < n_tiles))
        def _():
            wait_done()
            loop(True, True)

        @pl.when(i >= n_tiles)
        def _():
            wait_done()
            loop(False, True)

    for s in range(n_slots):
        pl.when(i % n_slots == s)(functools.partial(run, s, (s + 1) % n_slots))


def _combine(x, gates, pos_flat, y_buf, g, b):
    T, D = x.shape
    R = COMBINE_TILE
    n = T // R
    lag = COMBINE_LAG
    done = lambda i: jnp.maximum(i - lag, 0)
    return pl.pallas_call(
        functools.partial(_combine_kernel, rows=R, n_tiles=n),
        grid=(n + lag,),
        in_specs=[pl.BlockSpec((R * TOP_K,), lambda i: (jnp.minimum(i, n - 1),), memory_space=pltpu.SMEM),
                  pl.BlockSpec((R, D), lambda i: (done(i), 0)),
                  pl.BlockSpec((R, LANES), lambda i: (done(i), 0)),
                  pl.BlockSpec(memory_space=pl.ANY),
                  _full(g.shape), _full(b.shape)],
        out_specs=pl.BlockSpec((R, D), lambda i: (done(i), 0)),
        out_shape=jax.ShapeDtypeStruct((T, D), F32),
        scratch_shapes=[pltpu.VMEM((TOP_K, R * SUB, LANES), F32) for _ in range(lag + 1)]
                       + [pltpu.SemaphoreType.DMA((lag + 1,))],
        compiler_params=_params(0, 1),
        name="moe_combine_ln",
    )(pos_flat, x, gates, y_buf, g, b)


def _tile_major(pos, tile):
    K, T = pos.shape
    return pos.reshape(K, T // tile, tile).transpose(1, 0, 2).reshape(-1)


def _routing_plan(route, counts, n_tokens):
    tm = EXPERT_TILE
    e = route[:TOP_K]
    rank = route[TOP_K:]
    counts = counts[:, 0].astype(jnp.int32)
    padded = (counts + tm - 1) // tm * tm
    pend = jnp.cumsum(padded)
    pstart = pend - padded
    experts = jnp.arange(N_EXPERTS, dtype=jnp.int32)[:, None, None]
    base = jnp.sum(jnp.where(e[None] == experts, pstart[:, None, None], 0), axis=0)
    pos = (base + rank).astype(jnp.int32)
    n_slots = n_tokens * TOP_K + N_EXPERTS * tm
    blk_start = jnp.arange(n_slots // tm, dtype=jnp.int32) * tm
    blk_e = jnp.minimum(jnp.sum(blk_start[:, None] >= pend[None, :], axis=1), N_EXPERTS - 1).astype(jnp.int32)
    blk_valid = jnp.clip(pstart[blk_e] + counts[blk_e] - blk_start, 0, tm).astype(jnp.int32)
    return pos, blk_e, blk_valid, n_slots


def _moe(x1, route, gates, counts, layer, wgu, bgu, wdn, bdn, ln_g, ln_b):
    T, D = x1.shape
    pos, blk_e, blk_valid, n_slots = _routing_plan(route, counts, T)
    x_buf = _dispatch(x1, _tile_major(pos, DISPATCH_TILE), n_slots)
    y_buf = _expert_mlp(x_buf, blk_e, blk_valid, layer, wgu, bgu[:, :, None, :], wdn, bdn[:, :, None, :])
    return _combine(x1, gates, _tile_major(pos, COMBINE_TILE), y_buf, ln_g.reshape(1, D), ln_b.reshape(1, D))


def _trunk(x_parts, trunks, p):
    x = _embed_ln(x_parts[0], x_parts[1], p["emb_ln_g"], p["emb_ln_b"])
    T, D = x.shape
    mc = jnp.asarray(_fnet_channel_np(), BF16)
    for l in range(DEPTH):
        w_in = p["w_in"][l]
        wa = w_in[:, :SSD_WIDTH + SSD_XBC].astype(BF16)
        wdt = jnp.pad(w_in[:, SSD_WIDTH + SSD_XBC:SSD_IN], ((0, 0), (0, LANES - 2 * SSD_HEADS))).astype(BF16)
        wf = w_in[:, SSD_IN:SSD_IN + FNET_WIDTH].astype(BF16)
        wuv = w_in[:, SSD_IN + FNET_WIDTH:].astype(BF16)
        z, xbc, dt, fr, fi, uv = _inproj(x, wa, wdt, wf, wuv, mc)

        lane_pad = LANES - 2 * SSD_HEADS
        dtb = jnp.pad(p["dt_bias"][l].reshape(1, -1), ((0, 0), (0, lane_pad)))
        a_neg = jnp.pad(-jnp.exp(p["a_log"][l].astype(F32)).reshape(1, -1), ((0, 0), (0, lane_pad)))
        dskip_w = jnp.repeat(p["d_skip"][l], SSD_HEAD_DIM).reshape(1, SSD_WIDTH)
        normg = p["ssd_norm_g"][l].reshape(1, SSD_WIDTH)
        convw = p["conv_w"][l]
        convb = p["conv_b"][l].reshape(1, SSD_XBC)
        w_bd = jax.scipy.linalg.block_diag(*[p["fnet_w"][l, g] for g in range(FNET_GROUPS)]).astype(BF16)
        b_f = p["fnet_b"][l].reshape(1, FNET_WIDTH)

        ssd_parts, fn_parts = [], []
        row0 = 0
        for (B, L) in trunks:
            ssd_parts.append(_ssd_mixer(z, xbc, dt, row0, B, L, convw, convb, dtb, a_neg, dskip_w, normg))
            fn_parts.append(_fourier_mixer(fr, fi, row0, B, L, w_bd, b_f))
            row0 += B * L
        ssd = jnp.concatenate(ssd_parts, axis=0)
        fn = jnp.concatenate(fn_parts, axis=0)

        ws_cat = jnp.concatenate([p["gmlp_ws"][l, h] for h in range(GMLP_HEADS)], axis=1).astype(BF16)
        bs_wide = jnp.repeat(p["gmlp_bs"][l].T, GMLP_HDIM, axis=1)
        gm = _gmlp_mixer(uv, p["gmlp_ln_g"][l].reshape(1, -1), p["gmlp_ln_b"][l].reshape(1, -1),
                         ws_cat, bs_wide)

        w_r = jnp.pad(p["router_w"][l], ((0, 0), (0, LANES - N_EXPERTS)))
        wrh = w_r.astype(BF16)
        wrl = (w_r - wrh.astype(F32)).astype(BF16)
        b_r = jnp.pad(p["router_b"][l].reshape(1, -1), ((0, 0), (0, LANES - N_EXPERTS)),
                      constant_values=NEG_BIG)
        x1, idx, gates, counts = _outproj(
            x, ssd, fn, gm, p["w_out"][l].astype(BF16), p["b_out"][l].reshape(1, D),
            p["ln1_g"][l].reshape(1, D), p["ln1_b"][l].reshape(1, D), wrh, wrl, b_r)

        x = _moe(x1, idx, gates, counts, l, p["exp_w_gu"], p["exp_b_gu"], p["exp_w_dn"], p["exp_b_dn"],
                 p["ln2_g"][l], p["ln2_b"][l])
    return x


def kernel(x_prompt, x_sample, emb_ln_g, emb_ln_b, w_in, conv_w, conv_b, dt_bias, a_log, d_skip, ssd_norm_g, fnet_w, fnet_b, gmlp_ln_g, gmlp_ln_b, gmlp_ws, gmlp_bs, w_out, b_out, ln1_g, ln1_b, router_w, router_b, exp_w_gu, exp_b_gu, exp_w_dn, exp_b_dn, ln2_g, ln2_b):
    p = dict(emb_ln_g=emb_ln_g, emb_ln_b=emb_ln_b, w_in=w_in, conv_w=conv_w, conv_b=conv_b,
             dt_bias=dt_bias, a_log=a_log, d_skip=d_skip, ssd_norm_g=ssd_norm_g, fnet_w=fnet_w,
             fnet_b=fnet_b, gmlp_ln_g=gmlp_ln_g, gmlp_ln_b=gmlp_ln_b, gmlp_ws=gmlp_ws, gmlp_bs=gmlp_bs,
             w_out=w_out, b_out=b_out, ln1_g=ln1_g, ln1_b=ln1_b, router_w=router_w, router_b=router_b,
             exp_w_gu=exp_w_gu, exp_b_gu=exp_b_gu, exp_w_dn=exp_w_dn, exp_b_dn=exp_b_dn,
             ln2_g=ln2_g, ln2_b=ln2_b)
    D = x_prompt.shape[-1]
    trunks = [x_prompt.shape[:2], x_sample.shape[:2]]
    y = _trunk([x_prompt.reshape(-1, D), x_sample.reshape(-1, D)], trunks, p)
    n_p = x_prompt.shape[0] * x_prompt.shape[1]
    return (y[:n_p].reshape(x_prompt.shape), y[n_p:].reshape(x_sample.shape))
```

```python
import functools
import math

import numpy as np
import jax
import jax.numpy as jnp
from jax import lax
from jax.experimental import pallas as pl
from jax.experimental.pallas import tpu as pltpu

F32 = jnp.float32
BF16 = jnp.bfloat16

D_MODEL = 1024
DEPTH = 4
SSD_WIDTH = 512
SSD_HEAD_DIM = 64
SSD_HEADS = 8
SSD_GROUPS = 2
SSD_STATE = 64
SSD_CONV = 5
SSD_GN = SSD_GROUPS * SSD_STATE
SSD_XBC = SSD_WIDTH + 2 * SSD_GN
SSD_IN = SSD_WIDTH + SSD_XBC + 2 * SSD_HEADS
FNET_WIDTH = 256
FNET_GROUPS = 4
FNET_GDIM = 64
GMLP_WIDTH = 256
GMLP_HEADS = 4
GMLP_HDIM = 64
N_EXPERTS = 32
TOP_K = 4
D_EXPERT = D_MODEL
SWIGLU_LIMIT = 7.0
SWIGLU_ALPHA = 1.702
LN_EPS = 1e-5
RMS_EPS = 1e-5
DN_ALPHA = (2 * DEPTH) ** 0.25

LANES = 128
CHUNK = 128
HALO = 16
VMEM_LIMIT = 56 * 1024 * 1024

TOK_TILE = 512
EXPERT_TILE = 512
EXPERT_COL_CHUNK = 512
DISPATCH_TILE = 512
COMBINE_TILE = 256
COMBINE_SUB = 64
COMBINE_LAG = 2
ROW_UNROLL = 8
SSD_STEP_CHUNKS = 4
FNET_ROW_GROUP = 16
NEG_BIG = -1e30


def _dot(a, b):
    return jnp.dot(a, b, preferred_element_type=F32)


def _split2(v):
    hi = v.astype(BF16)
    lo = (v - hi.astype(F32)).astype(BF16)
    return hi, lo


def _split3(v):
    hi = v.astype(BF16)
    r = v - hi.astype(F32)
    mid = r.astype(BF16)
    lo = (r - mid.astype(F32)).astype(BF16)
    return hi, mid, lo


def _ln_rows(xf, g, b):
    mu = jnp.mean(xf, -1, keepdims=True)
    xc = xf - mu
    var = jnp.mean(xc * xc, -1, keepdims=True)
    return xc * lax.rsqrt(var + LN_EPS) * g + b


def _params(n_parallel=1, n_arbitrary=0):
    sem = ("parallel",) * n_parallel + ("arbitrary",) * n_arbitrary
    return pltpu.CompilerParams(dimension_semantics=sem, vmem_limit_bytes=VMEM_LIMIT)


def _full(shape):
    nd = len(shape)
    return pl.BlockSpec(shape, lambda *_: (0,) * nd)


def _embed_ln_kernel(xa_ref, xb_ref, g_ref, b_ref, o_ref, *, tiles_a):
    i = pl.program_id(0)

    @pl.when(i < tiles_a)
    def _():
        o_ref[...] = _ln_rows(xa_ref[...], g_ref[...], b_ref[...])

    @pl.when(i >= tiles_a)
    def _():
        o_ref[...] = _ln_rows(xb_ref[...], g_ref[...], b_ref[...])


def _embed_ln(xa, xb, g, b):
    D = xa.shape[1]
    na, nb = xa.shape[0] // TOK_TILE, xb.shape[0] // TOK_TILE
    return pl.pallas_call(
        functools.partial(_embed_ln_kernel, tiles_a=na),
        grid=(na + nb,),
        in_specs=[pl.BlockSpec((TOK_TILE, D), lambda i: (jnp.minimum(i, na - 1), 0)),
                  pl.BlockSpec((TOK_TILE, D), lambda i: (jnp.maximum(i - na, 0), 0)),
                  _full((1, D)), _full((1, D))],
        out_specs=pl.BlockSpec((TOK_TILE, D), lambda i: (i, 0)),
        out_shape=jax.ShapeDtypeStruct(((na + nb) * TOK_TILE, D), F32),
        compiler_params=_params(1),
        name="embed_ln",
    )(xa, xb, g.reshape(1, D), b.reshape(1, D))


def _inproj_kernel(x_ref, wa_ref, wdt_ref, wf_ref, wuv_ref, mc_ref,
                   z_ref, xbc_ref, dt_ref, fr_ref, fi_ref, uv_ref):
    xb = x_ref[...].astype(BF16)
    a = _dot(xb, wa_ref[...])
    z_ref[...] = a[:, :SSD_WIDTH].astype(BF16)
    xbc_ref[...] = a[:, SSD_WIDTH:].astype(BF16)
    dt_ref[...] = _dot(xb, wdt_ref[...])
    f = _dot(xb, wf_ref[...]).astype(BF16)
    fri = _dot(f, mc_ref[...])
    fr_ref[...] = fri[:, :FNET_WIDTH]
    fi_ref[...] = fri[:, FNET_WIDTH:]
    uv_ref[...] = _dot(xb, wuv_ref[...]).astype(BF16)


def _inproj(x, wa, wdt, wf, wuv, mc):
    T, D = x.shape
    tm = TOK_TILE
    row = lambda w: pl.BlockSpec((tm, w), lambda i: (i, 0))
    return pl.pallas_call(
        _inproj_kernel,
        grid=(T // tm,),
        in_specs=[row(D), _full(wa.shape), _full(wdt.shape), _full(wf.shape), _full(wuv.shape),
                  _full(mc.shape)],
        out_specs=[row(SSD_WIDTH), row(SSD_XBC), row(LANES), row(FNET_WIDTH), row(FNET_WIDTH),
                   row(2 * GMLP_WIDTH)],
        out_shape=[jax.ShapeDtypeStruct((T, SSD_WIDTH), BF16),
                   jax.ShapeDtypeStruct((T, SSD_XBC), BF16),
                   jax.ShapeDtypeStruct((T, LANES), F32),
                   jax.ShapeDtypeStruct((T, FNET_WIDTH), F32),
                   jax.ShapeDtypeStruct((T, FNET_WIDTH), F32),
                   jax.ShapeDtypeStruct((T, 2 * GMLP_WIDTH), BF16)],
        compiler_params=_params(1),
        name="inproj",
    )(x, wa, wdt, wf, wuv, mc)


def _conv_silu(xc_ref, xp_ref, xn_ref, convw_ref, convb_ref, first, last):
    Q = xc_ref.shape[0]
    cur = xc_ref[...].astype(F32)
    prev = jnp.where(first, 0.0, xp_ref[...].astype(F32))
    nxt = jnp.where(last, 0.0, xn_ref[...].astype(F32))
    ext = jnp.concatenate([prev[HALO - 8:], cur, nxt[:8]], axis=0)
    n_ext = Q + 16
    pad = SSD_CONV // 2
    conv = convb_ref[...]
    for k in range(SSD_CONV):
        shift = (pad - k) % n_ext
        tap = ext if shift == 0 else pltpu.roll(ext, shift, axis=0)
        conv = conv + tap[8:8 + Q] * convw_ref[k:k + 1, :]
    return conv * jax.nn.sigmoid(conv)


def _ssd_chunk(act, dt_raw, dtb_ref, a_ref, cum_ref, expand_ref, state_ref, *, backward):
    Q = CHUNK
    h_off = SSD_HEADS if backward else 0
    xs = act[:, :SSD_WIDTH]
    bm = act[:, SSD_WIDTH:SSD_WIDTH + SSD_GN]
    cm = act[:, SSD_WIDTH + SSD_GN:]

    dt = jax.nn.softplus(dt_raw + dtb_ref[...])
    adt = dt * a_ref[...]
    cum = cum_ref[...]
    h3 = _split3(adt)
    acum = _dot(cum, h3[0]) + _dot(cum, h3[1]) + _dot(cum, h3[2])
    acum_t = acum.T
    total = acum[0:1, :] if backward else acum[Q - 1:Q, :]
    dec_in = jnp.exp(acum)
    dec_end = jnp.exp(total - acum)

    expand = expand_ref[...]

    def widen(v):
        hi, lo = _split2(v)
        return _dot(hi, expand) + _dot(lo, expand)

    dt_w = widen(dt)
    dec_in_w = widen(dec_in)
    dec_end_w = widen(dec_end)
    tdec_w = dec_in_w[0:1, :] if backward else dec_in_w[Q - 1:Q, :]

    xdt = xs * dt_w

    row = lax.broadcasted_iota(jnp.int32, (Q, Q), 0)
    col = lax.broadcasted_iota(jnp.int32, (Q, Q), 1)
    mask = (col > row) if backward else (col <= row)
    lane = lax.broadcasted_iota(jnp.int32, (Q, LANES), 1)
    lo_half = lane < SSD_STATE

    bm_b = bm.astype(BF16)
    cb = []
    for g in range(SSD_GROUPS):
        cm_g = jnp.where(lo_half if g == 0 else jnp.logical_not(lo_half), cm, 0.0).astype(BF16)
        cb.append(lax.dot_general(cm_g, bm_b, (((1,), (1,)), ((), ())),
                                  preferred_element_type=F32))

    y_parts = []
    for pair in range(SSD_HEADS // 2):
        gmats = []
        for r in range(2):
            h = 2 * pair + r
            hl = h_off + h
            diff = acum[:, hl:hl + 1] - acum_t[hl:hl + 1, :]
            dec = jnp.exp(jnp.where(mask, diff, NEG_BIG))
            gmats.append((cb[h // (SSD_HEADS // SSD_GROUPS)] * dec).astype(BF16))
        lhs = jnp.concatenate(gmats, axis=1)
        x2 = xdt[:, pair * LANES:(pair + 1) * LANES]
        rhs = jnp.concatenate([jnp.where(lo_half, x2, 0.0), jnp.where(lo_half, 0.0, x2)],
                              axis=0).astype(BF16)
        y_parts.append(_dot(lhs, rhs))
    y_diag = jnp.concatenate(y_parts, axis=1)

    srow = lax.broadcasted_iota(jnp.int32, (SSD_GN, SSD_WIDTH), 0)
    scol = lax.broadcasted_iota(jnp.int32, (SSD_GN, SSD_WIDTH), 1)
    on_group = (srow // SSD_STATE) == (scol // (SSD_WIDTH // SSD_GROUPS))
    st_new = _dot(bm.T.astype(BF16), (xdt * dec_end_w).astype(BF16))
    st_new = jnp.where(on_group, st_new, 0.0)
    st_prev = state_ref[...]
    y_off = _dot(cm.astype(BF16), st_prev.astype(BF16)) * dec_in_w
    state_ref[...] = st_prev * tdec_w + st_new
    return y_diag + y_off, xs


def _ssd_bwd_kernel(xc_ref, xp_ref, xn_ref, dt_ref, convw_ref, convb_ref, dtb_ref, a_ref,
                    cum_ref, expand_ref, yb_ref, act_ref, state_ref, *, n_steps):
    c = pl.program_id(1)
    step = n_steps - 1 - c

    @pl.when(c == 0)
    def _():
        state_ref[...] = jnp.zeros_like(state_ref)

    act = _conv_silu(xc_ref, xp_ref, xn_ref, convw_ref, convb_ref,
                     first=step == 0, last=step == n_steps - 1)
    act_ref[...] = act.astype(BF16)
    for s in reversed(range(SSD_STEP_CHUNKS)):
        rows = slice(s * CHUNK, (s + 1) * CHUNK)
        y, _ = _ssd_chunk(act[rows], dt_ref[rows, :], dtb_ref, a_ref, cum_ref, expand_ref, state_ref,
                          backward=True)
        yb_ref[rows, :] = y


def _ssd_fwd_kernel(act_ref, dt_ref, z_ref, yb_ref, dtb_ref, a_ref, cum_ref, expand_ref,
                    dskip_ref, normg_ref, o_ref, state_ref):
    c = pl.program_id(1)

    @pl.when(c == 0)
    def _():
        state_ref[...] = jnp.zeros_like(state_ref)

    gw = SSD_WIDTH // SSD_GROUPS
    for s in range(SSD_STEP_CHUNKS):
        rows = slice(s * CHUNK, (s + 1) * CHUNK)
        y, xs = _ssd_chunk(act_ref[rows, :].astype(F32), dt_ref[rows, :], dtb_ref, a_ref, cum_ref,
                           expand_ref, state_ref, backward=False)
        z = z_ref[rows, :].astype(F32)
        y = (y + yb_ref[rows, :] + xs * dskip_ref[...]) * (z * jax.nn.sigmoid(z))
        outs = []
        for g in range(SSD_GROUPS):
            yg = y[:, g * gw:(g + 1) * gw]
            outs.append(yg * lax.rsqrt(jnp.mean(yg * yg, -1, keepdims=True) + RMS_EPS))
        o_ref[rows, :] = (jnp.concatenate(outs, axis=1) * normg_ref[...]).astype(BF16)


def _ssd_mixer(z, xbc, dt, row0, B, L, convw, convb, dtb, a_neg, dskip_w, normg):
    R = SSD_STEP_CHUNKS * CHUNK
    C = L // R
    c0 = row0 // R
    hpc = R // HALO
    n_halo = z.shape[0] // HALO
    consts = _ssd_consts()

    def specs(chunk_of):
        cur = lambda w: pl.BlockSpec((R, w), lambda b, c: (c0 + b * C + chunk_of(c), 0))
        prv = pl.BlockSpec((HALO, SSD_XBC),
                           lambda b, c: (jnp.maximum((c0 + b * C + chunk_of(c)) * hpc - 1, 0), 0))
        nxt = pl.BlockSpec((HALO, SSD_XBC),
                           lambda b, c: (jnp.minimum((c0 + b * C + chunk_of(c) + 1) * hpc, n_halo - 1), 0))
        return cur, prv, nxt

    out_row = lambda chunk_of, w: pl.BlockSpec((R, w), lambda b, c: (b * C + chunk_of(c), 0))

    rev = lambda c: C - 1 - c
    cur, prv, nxt = specs(rev)
    yb, act = pl.pallas_call(
        functools.partial(_ssd_bwd_kernel, n_steps=C),
        grid=(B, C),
        in_specs=[cur(SSD_XBC), prv, nxt, cur(LANES), _full(convw.shape), _full(convb.shape),
                  _full(dtb.shape), _full(a_neg.shape),
                  _full(consts["cum_b"].shape), _full(consts["expand_b"].shape)],
        out_specs=[out_row(rev, SSD_WIDTH), out_row(rev, SSD_XBC)],
        out_shape=[jax.ShapeDtypeStruct((B * L, SSD_WIDTH), F32),
                   jax.ShapeDtypeStruct((B * L, SSD_XBC), BF16)],
        scratch_shapes=[pltpu.VMEM((SSD_GN, SSD_WIDTH), F32)],
        compiler_params=_params(1, 1),
        name="ssd_bwd",
    )(xbc, xbc, xbc, dt, convw, convb, dtb, a_neg, consts["cum_b"], consts["expand_b"])

    fwd = lambda c: c
    cur, _, _ = specs(fwd)
    return pl.pallas_call(
        _ssd_fwd_kernel,
        grid=(B, C),
        in_specs=[out_row(fwd, SSD_XBC), cur(LANES), cur(SSD_WIDTH), out_row(fwd, SSD_WIDTH),
                  _full(dtb.shape), _full(a_neg.shape),
                  _full(consts["cum_f"].shape), _full(consts["expand_f"].shape),
                  _full(dskip_w.shape), _full(normg.shape)],
        out_specs=out_row(fwd, SSD_WIDTH),
        out_shape=jax.ShapeDtypeStruct((B * L, SSD_WIDTH), BF16),
        scratch_shapes=[pltpu.VMEM((SSD_GN, SSD_WIDTH), F32)],
        compiler_params=_params(1, 1),
        name="ssd_fwd",
    )(act, dt, z, yb, dtb, a_neg, consts["cum_f"], consts["expand_f"], dskip_w, normg)


@functools.lru_cache(maxsize=None)
def _ssd_consts_np():
    q = np.arange(CHUNK)
    cum_f = (q[None, :] <= q[:, None]).astype(np.float32)
    cum_b = (q[None, :] >= q[:, None]).astype(np.float32)
    out = {"cum_f": cum_f, "cum_b": cum_b}
    for name, off in (("expand_f", 0), ("expand_b", SSD_HEADS)):
        e = np.zeros((LANES, SSD_WIDTH), np.float32)
        for h in range(SSD_HEADS):
            e[off + h, h * SSD_HEAD_DIM:(h + 1) * SSD_HEAD_DIM] = 1.0
        out[name] = e
    return out


def _ssd_consts():
    return {k: jnp.asarray(v, BF16) for k, v in _ssd_consts_np().items()}


@functools.lru_cache(maxsize=None)
def _fnet_tables_np(L):
    L2 = CHUNK
    L1 = L // L2
    k1 = np.arange(L1)
    ang1 = 2.0 * np.pi * ((k1[:, None] * k1[None, :]) % L1) / L1
    f1 = np.concatenate([np.cos(ang1), np.sin(ang1)], axis=0) / math.sqrt(L1)
    k = k1[:, None, None] + L1 * np.arange(L2)[None, :, None]
    l2 = np.arange(L2)[None, None, :]
    ang2 = 2.0 * np.pi * ((k * l2) % L) / L
    gc = np.cos(ang2) / math.sqrt(L2)
    gs = np.sin(ang2) / math.sqrt(L2)
    return f1.astype(np.float32), gc.astype(np.float32), gs.astype(np.float32)


@functools.lru_cache(maxsize=None)
def _fnet_channel_np():
    c = np.arange(FNET_GDIM)
    ang = 2.0 * np.pi * ((c[:, None] * c[None, :]) % FNET_GDIM) / FNET_GDIM
    cc = np.cos(ang) / math.sqrt(FNET_GDIM)
    sc = np.sin(ang) / math.sqrt(FNET_GDIM)
    eye = np.eye(FNET_GROUPS)
    return np.concatenate([np.kron(eye, cc), np.kron(eye, -sc)], axis=1).astype(np.float32)


def _fnet_stage1_kernel(f_ref, xr_ref, xi_ref, yr_ref, yi_ref, *, L1, group):
    W = FNET_WIDTH
    f = f_ref[...]
    for j in range(group):
        x = jnp.concatenate([xr_ref[:, j, :], xi_ref[:, j, :]], axis=1).astype(BF16)
        p = _dot(f, x)
        yr_ref[:, j, :] = p[:L1, :W] + p[L1:, W:]
        yi_ref[:, j, :] = p[:L1, W:] - p[L1:, :W]


def _fnet_stage2_kernel(yr_ref, yi_ref, g_ref, w_ref, b_ref, o_ref, *, group):
    for j in range(group):
        y = jnp.concatenate([yr_ref[j], yi_ref[j]], axis=0).astype(BF16)
        zr = _dot(g_ref[j], y)
        o_ref[:, j, :] = _dot(zr.astype(BF16), w_ref[...]) + b_ref[...]


def _fourier_mixer(fr, fi, row0, B, L, w_bd, b_f):
    L2 = CHUNK
    L1 = L // L2
    W = FNET_WIDTH
    f1, gc, gs = _fnet_tables_np(L)
    f1 = jnp.asarray(f1, BF16)
    g_cat = jnp.asarray(np.concatenate([gc, gs], axis=2), BF16)
    xr = fr.reshape(-1, L2, W)
    xi = fi.reshape(-1, L2, W)
    blk0 = row0 // L2 // L1
    grp1 = FNET_ROW_GROUP
    xblk = pl.BlockSpec((L1, grp1, W), lambda b, j: (blk0 + b, j, 0))
    yblk = pl.BlockSpec((L1, grp1, W), lambda b, j: (b, j, 0))
    yr, yi = pl.pallas_call(
        functools.partial(_fnet_stage1_kernel, L1=L1, group=grp1),
        grid=(B, L2 // grp1),
        in_specs=[_full(f1.shape), xblk, xblk],
        out_specs=[yblk, yblk],
        out_shape=[jax.ShapeDtypeStruct((B * L1, L2, W), F32)] * 2,
        compiler_params=_params(2),
        name="fnet_stage1",
    )(f1, xr, xi)
    grp2 = FNET_ROW_GROUP // 2
    n2 = L1 // grp2
    yblk2 = pl.BlockSpec((grp2, L2, W), lambda b, k: (b * n2 + k, 0, 0))
    gblk = pl.BlockSpec((grp2, L2, 2 * L2), lambda b, k: (k, 0, 0))
    out = pl.pallas_call(
        functools.partial(_fnet_stage2_kernel, group=grp2),
        grid=(B, n2),
        in_specs=[yblk2, yblk2, gblk, _full(w_bd.shape), _full(b_f.shape)],
        out_specs=pl.BlockSpec((L2, grp2, W), lambda b, k: (b, k, 0)),
        out_shape=jax.ShapeDtypeStruct((B * L2, L1, W), F32),
        compiler_params=_params(2),
        name="fnet_stage2",
    )(yr, yi, g_cat, w_bd, b_f)
    return out.reshape(B * L, W)


def _gmlp_kernel(uv_ref, g_ref, b_ref, ws_ref, bs_ref, o_ref, *, n_chunks):
    uv = uv_ref[...].astype(F32)
    ge = 0.5 * uv * (1.0 + lax.erf(uv * (1.0 / math.sqrt(2.0))))
    u = ge[:, :GMLP_WIDTH]
    v = _ln_rows(ge[:, GMLP_WIDTH:], g_ref[...], b_ref[...])
    lane = lax.broadcasted_iota(jnp.int32, (CHUNK, GMLP_WIDTH), 1)
    ws = ws_ref[...]
    bs = bs_ref[...]
    for j in range(n_chunks):
        vj = v[j * CHUNK:(j + 1) * CHUNK]
        rhs = jnp.concatenate(
            [jnp.where(lane // GMLP_HDIM == h, vj, 0.0) for h in range(GMLP_HEADS)], axis=0)
        sv = _dot(ws, rhs.astype(BF16)) + bs
        o_ref[j * CHUNK:(j + 1) * CHUNK, :] = (u[j * CHUNK:(j + 1) * CHUNK] * sv).astype(BF16)


def _gmlp_mixer(uv, ln_g, ln_b, ws_cat, bs_wide):
    T = uv.shape[0]
    tm = TOK_TILE
    return pl.pallas_call(
        functools.partial(_gmlp_kernel, n_chunks=tm // CHUNK),
        grid=(T // tm,),
        in_specs=[pl.BlockSpec((tm, 2 * GMLP_WIDTH), lambda i: (i, 0)), _full(ln_g.shape),
                  _full(ln_b.shape), _full(ws_cat.shape), _full(bs_wide.shape)],
        out_specs=pl.BlockSpec((tm, GMLP_WIDTH), lambda i: (i, 0)),
        out_shape=jax.ShapeDtypeStruct((T, GMLP_WIDTH), BF16),
        compiler_params=_params(1),
        name="gmlp",
    )(uv, ln_g, ln_b, ws_cat, bs_wide)


def _outproj_kernel(x_ref, ssd_a_ref, ssd_b_ref, fn_a_ref, fn_b_ref, gm_ref, wo_ref, bo_ref, g_ref, b_ref,
                    wrh_ref, wrl_ref, br_ref, tri_ref, x1_ref, route_ref, gate_ref, cnt_ref, seen_ref,
                    *, tiles_a):
    @pl.when(pl.program_id(0) == 0)
    def _():
        seen_ref[...] = jnp.zeros_like(seen_ref)

    in_a = pl.program_id(0) < tiles_a
    ssd = jnp.where(in_a, ssd_a_ref[...], ssd_b_ref[...])
    fn = jnp.where(in_a, fn_a_ref[...], fn_b_ref[...])
    acc = _dot(ssd, wo_ref[0:SSD_WIDTH, :])
    acc = acc + _dot(fn.astype(BF16), wo_ref[SSD_WIDTH:SSD_WIDTH + FNET_WIDTH, :])
    acc = acc + _dot(gm_ref[...], wo_ref[SSD_WIDTH + FNET_WIDTH:, :])
    h = DN_ALPHA * x_ref[...] + acc + bo_ref[...]
    x1 = _ln_rows(h, g_ref[...], b_ref[...])
    x1_ref[...] = x1

    xh, xl = _split2(x1)
    wrh = wrh_ref[...]
    logits = _dot(xh, wrh) + _dot(xl, wrh) + _dot(xh, wrl_ref[...]) + br_ref[...]
    work = logits.T[:N_EXPERTS]
    n_tok = work.shape[1]
    row = lax.broadcasted_iota(jnp.int32, work.shape, 0)
    vals, idxs = [], []
    for _ in range(TOP_K):
        m = jnp.max(work, axis=0, keepdims=True)
        i = jnp.min(jnp.where(work == m, row, N_EXPERTS), axis=0, keepdims=True)
        vals.append(m)
        idxs.append(i)
        work = jnp.where(row == i, -jnp.inf, work)
    exps = [jnp.exp(v - vals[0]) for v in vals]
    denom = exps[0]
    for e in exps[1:]:
        denom = denom + e

    onehots = [row == i for i in idxs]
    chosen = jnp.zeros(work.shape, F32)
    for oh in onehots:
        chosen = jnp.where(oh, 1.0, chosen)
    seen = seen_ref[...]
    before = _dot(chosen.astype(BF16), tri_ref[...]) + seen[:, 0:1]
    seen = seen + jnp.sum(chosen, axis=1, keepdims=True)
    seen_ref[...] = seen
    cnt_ref[...] = seen

    row8 = lax.broadcasted_iota(jnp.int32, (2 * TOP_K, n_tok), 0)
    route = jnp.zeros((2 * TOP_K, n_tok), jnp.int32)
    gates_t = jnp.zeros((2 * TOP_K, n_tok), F32)
    for k in range(TOP_K):
        rank = jnp.sum(jnp.where(onehots[k], before, 0.0), axis=0, keepdims=True).astype(jnp.int32)
        route = jnp.where(row8 == k, idxs[k], route)
        route = jnp.where(row8 == TOP_K + k, rank, route)
        gates_t = jnp.where(row8 == k, exps[k] / denom, gates_t)
    route_ref[...] = route
    pad = jnp.zeros((LANES - 2 * TOP_K, n_tok), F32)
    gate_ref[...] = jnp.concatenate([gates_t, pad], axis=0).T


def _outproj(x, ssd_parts, fn_parts, gm, wo, bo, g, b, wrh, wrl, br):
    T, D = x.shape
    tm = TOK_TILE
    row = lambda w: pl.BlockSpec((tm, w), lambda i: (i, 0))
    na = ssd_parts[0].shape[0] // tm
    part_a = lambda w: pl.BlockSpec((tm, w), lambda i: (jnp.minimum(i, na - 1), 0))
    part_b = lambda w: pl.BlockSpec((tm, w), lambda i: (jnp.maximum(i - na, 0), 0))
    q = np.arange(tm)
    tri = jnp.asarray((q[:, None] < q[None, :]).astype(np.float32), BF16)
    parts = (*ssd_parts, *fn_parts)
    return pl.pallas_call(
        functools.partial(_outproj_kernel, tiles_a=na),
        grid=(T // tm,),
        in_specs=[row(D), part_a(SSD_WIDTH), part_b(SSD_WIDTH), part_a(FNET_WIDTH), part_b(FNET_WIDTH),
                  row(GMLP_WIDTH), _full(wo.shape),
                  _full(bo.shape), _full(g.shape), _full(b.shape), _full(wrh.shape),
                  _full(wrl.shape), _full(br.shape), _full(tri.shape)],
        out_specs=[row(D), pl.BlockSpec((2 * TOP_K, tm), lambda i: (0, i)), row(LANES),
                   _full((N_EXPERTS, LANES))],
        out_shape=[jax.ShapeDtypeStruct((T, D), F32),
                   jax.ShapeDtypeStruct((2 * TOP_K, T), jnp.int32),
                   jax.ShapeDtypeStruct((T, LANES), F32),
                   jax.ShapeDtypeStruct((N_EXPERTS, LANES), F32)],
        scratch_shapes=[pltpu.VMEM((N_EXPERTS, LANES), F32)],
        compiler_params=_params(0, 1),
        name="outproj_ln_router",
    )(x, *parts, gm, wo, bo, g, b, wrh, wrl, br, tri)


SUB = D_MODEL // LANES


def _to_slot_major(dst_ref, dst_row0, x, n):
    for c in range(SUB):
        dst_ref[pl.ds(dst_row0 + c, n, stride=SUB), :] = x[:, c * LANES:(c + 1) * LANES]


def _from_slot_major(src_ref, src_row0, n):
    return jnp.concatenate([src_ref[pl.ds(src_row0 + c, n, stride=SUB), :] for c in range(SUB)], axis=1)


def _slot_copy(src, dst, sem, src_slot, dst_slot):
    s0 = pl.multiple_of(src_slot * SUB, SUB)
    d0 = pl.multiple_of(dst_slot * SUB, SUB)
    return pltpu.make_async_copy(src.at[pl.ds(s0, SUB), :], dst.at[pl.ds(d0, SUB), :], sem)


def _slots_wait(src, dst, sem, n_slots):
    pltpu.make_async_copy(src.at[pl.ds(0, n_slots * SUB), :], dst.at[pl.ds(0, n_slots * SUB), :], sem).wait()


def _dispatch_kernel(pos_ref, x_ref, xbuf_hbm, xs_ref, sem, *, rows):
    _to_slot_major(xs_ref, 0, x_ref[...], rows)

    def body(j, carry):
        for u in range(ROW_UNROLL):
            r = j * ROW_UNROLL + u
            for k in range(TOP_K):
                _slot_copy(xs_ref, xbuf_hbm, sem, r, pos_ref[k * rows + r]).start(priority=k % 2)
        return carry

    lax.fori_loop(0, rows // ROW_UNROLL, body, 0)
    for _ in range(TOP_K):
        _slots_wait(xs_ref, xbuf_hbm, sem, rows)


def _dispatch(x, pos_flat, n_slots):
    T, D = x.shape
    R = DISPATCH_TILE
    return pl.pallas_call(
        functools.partial(_dispatch_kernel, rows=R),
        grid=(T // R,),
        in_specs=[pl.BlockSpec((R * TOP_K,), lambda i: (i,), memory_space=pltpu.SMEM),
                  pl.BlockSpec((R, D), lambda i: (i, 0))],
        out_specs=pl.BlockSpec(memory_space=pl.ANY),
        out_shape=jax.ShapeDtypeStruct((n_slots * SUB, LANES), F32),
        scratch_shapes=[pltpu.VMEM((R * SUB, LANES), F32), pltpu.SemaphoreType.DMA(())],
        compiler_params=_params(1),
        name="moe_dispatch",
    )(pos_flat, x)


def _expert_kernel(blk_e_ref, valid_ref, x_ref, wgu_ref, bgu_ref, wdn_ref, bdn_ref, y_ref, wgu_b, wdn_b):
    i = pl.program_id(0)
    valid = valid_ref[i]
    prev_e = blk_e_ref[jnp.maximum(i - 1, 0)]
    new_expert = jnp.logical_or(i == 0, blk_e_ref[i] != prev_e)

    @pl.when(jnp.logical_and(valid > 0, new_expert))
    def _():
        wgu_b[...] = wgu_ref[...].astype(BF16)
        wdn_b[...] = wdn_ref[...].astype(BF16)

    @pl.when(valid > 0)
    def _():
        x = _from_slot_major(x_ref, 0, EXPERT_TILE)
        row = lax.broadcasted_iota(jnp.int32, x.shape, 0)
        xb = jnp.where(row < valid, x, 0.0).astype(BF16)
        acc = jnp.broadcast_to(bdn_ref[...], x.shape)
        nc = EXPERT_COL_CHUNK
        for j in range(D_EXPERT // nc):
            g = _dot(xb, wgu_b[:, j * nc:(j + 1) * nc]) + bgu_ref[:, j * nc:(j + 1) * nc]
            u = (_dot(xb, wgu_b[:, D_EXPERT + j * nc:D_EXPERT + (j + 1) * nc])
                 + bgu_ref[:, D_EXPERT + j * nc:D_EXPERT + (j + 1) * nc])
            g = jnp.minimum(g, SWIGLU_LIMIT)
            u = jnp.clip(u, -SWIGLU_LIMIT, SWIGLU_LIMIT)
            act = (u + 1.0) * (g * jax.nn.sigmoid(SWIGLU_ALPHA * g))
            acc = acc + _dot(act.astype(BF16), wdn_b[j * nc:(j + 1) * nc, :])
        _to_slot_major(y_ref, 0, acc, EXPERT_TILE)

    @pl.when(valid <= 0)
    def _():
        y_ref[...] = jnp.zeros_like(y_ref)


def _expert_mlp(x_buf, blk_e, blk_valid, layer, wgu, bgu, wdn, bdn):
    D = D_MODEL
    tm = EXPERT_TILE
    per_expert = lambda r, c: pl.BlockSpec((None, None, r, c), lambda i, be, bv: (layer, be[i], 0, 0))
    slots = pl.BlockSpec((tm * SUB, LANES), lambda i, be, bv: (i, 0))
    grid_spec = pltpu.PrefetchScalarGridSpec(
        num_scalar_prefetch=2,
        grid=(x_buf.shape[0] // (tm * SUB),),
        in_specs=[slots, per_expert(D, 2 * D_EXPERT), per_expert(1, 2 * D_EXPERT),
                  per_expert(D_EXPERT, D), per_expert(1, D)],
        out_specs=slots,
        scratch_shapes=[pltpu.VMEM((D, 2 * D_EXPERT), BF16), pltpu.VMEM((D_EXPERT, D), BF16)],
    )
    return pl.pallas_call(
        _expert_kernel,
        grid_spec=grid_spec,
        out_shape=jax.ShapeDtypeStruct(x_buf.shape, F32),
        compiler_params=_params(0, 1),
        name="moe_expert_mlp",
    )(blk_e, blk_valid, x_buf, wgu, bgu, wdn, bdn)


def _combine_kernel(pos_ref, x_ref, gate_ref, y_hbm, g_ref, b_ref, o_ref, *scratch, rows, n_tiles):
    n_slots = COMBINE_LAG + 1
    bufs, sem = scratch[:n_slots], scratch[n_slots]
    i = pl.program_id(0)
    sub = COMBINE_SUB

    def run(s_issue, s_done):
        def issue(j):
            for u in range(sub):
                r = j * sub + u
                for k in range(TOP_K):
                    _slot_copy(y_hbm, bufs[s_issue].at[k], sem.at[s_issue], pos_ref[k * rows + r], r).start(
                        priority=k % 2)

        def finish(j):
            rs = pl.ds(pl.multiple_of(j * sub, sub), sub)
            row0 = pl.multiple_of(j * (sub * SUB), sub * SUB)
            gates = gate_ref[rs, :]
            y = gates[:, 0:1] * _from_slot_major(bufs[s_done].at[0], row0, sub)
            for k in range(1, TOP_K):
                y = y + gates[:, k:k + 1] * _from_slot_major(bufs[s_done].at[k], row0, sub)
            o_ref[rs, :] = _ln_rows(DN_ALPHA * x_ref[rs, :] + y, g_ref[...], b_ref[...])

        def wait_done():
            for k in range(TOP_K):
                _slots_wait(y_hbm, bufs[s_done].at[k], sem.at[s_done], rows)

        def loop(do_issue, do_finish):
            def body(j, carry):
                if do_issue:
                    issue(j)
                if do_finish:
                    finish(j)
                return carry

            lax.fori_loop(0, rows // sub, body, 0)

        @pl.when(i < COMBINE_LAG)
        def _():
            loop(True, False)

        @pl.when(jnp.logical_and(i >= COMBINE_LAG, i < n_tiles))
        def _():
            wait_done()
            loop(True, True)

        @pl.when(i >= n_tiles)
        def _():
            wait_done()
            loop(False, True)

    for s in range(n_slots):
        pl.when(i % n_slots == s)(functools.partial(run, s, (s + 1) % n_slots))


def _combine(x, gates, pos_flat, y_buf, g, b):
    T, D = x.shape
    R = COMBINE_TILE
    n = T // R
    lag = COMBINE_LAG
    done = lambda i: jnp.maximum(i - lag, 0)
    return pl.pallas_call(
        functools.partial(_combine_kernel, rows=R, n_tiles=n),
        grid=(n + lag,),
        in_specs=[pl.BlockSpec((R * TOP_K,), lambda i: (jnp.minimum(i, n - 1),), memory_space=pltpu.SMEM),
                  pl.BlockSpec((R, D), lambda i: (done(i), 0)),
                  pl.BlockSpec((R, LANES), lambda i: (done(i), 0)),
                  pl.BlockSpec(memory_space=pl.ANY),
                  _full(g.shape), _full(b.shape)],
        out_specs=pl.BlockSpec((R, D), lambda i: (done(i), 0)),
        out_shape=jax.ShapeDtypeStruct((T, D), F32),
        scratch_shapes=[pltpu.VMEM((TOP_K, R * SUB, LANES), F32) for _ in range(lag + 1)]
                       + [pltpu.SemaphoreType.DMA((lag + 1,))],
        compiler_params=_params(0, 1),
        name="moe_combine_ln",
    )(pos_flat, x, gates, y_buf, g, b)


def _tile_major(pos, tile):
    K, T = pos.shape
    return pos.reshape(K, T // tile, tile).transpose(1, 0, 2).reshape(-1)


def _routing_plan(route, counts, n_tokens):
    tm = EXPERT_TILE
    e = route[:TOP_K]
    rank = route[TOP_K:]
    counts = counts[:, 0].astype(jnp.int32)
    padded = (counts + tm - 1) // tm * tm
    pend = jnp.cumsum(padded)
    pstart = pend - padded
    experts = jnp.arange(N_EXPERTS, dtype=jnp.int32)[:, None, None]
    base = jnp.sum(jnp.where(e[None] == experts, pstart[:, None, None], 0), axis=0)
    pos = (base + rank).astype(jnp.int32)
    n_slots = n_tokens * TOP_K + N_EXPERTS * tm
    blk_start = jnp.arange(n_slots // tm, dtype=jnp.int32) * tm
    blk_e = jnp.minimum(jnp.sum(blk_start[:, None] >= pend[None, :], axis=1), N_EXPERTS - 1).astype(jnp.int32)
    blk_valid = jnp.clip(pstart[blk_e] + counts[blk_e] - blk_start, 0, tm).astype(jnp.int32)
    return pos, blk_e, blk_valid, n_slots


def _moe(x1, route, gates, counts, layer, wgu, bgu, wdn, bdn, ln_g, ln_b):
    T, D = x1.shape
    pos, blk_e, blk_valid, n_slots = _routing_plan(route, counts, T)
    x_buf = _dispatch(x1, _tile_major(pos, DISPATCH_TILE), n_slots)
    y_buf = _expert_mlp(x_buf, blk_e, blk_valid, layer, wgu, bgu[:, :, None, :], wdn, bdn[:, :, None, :])
    return _combine(x1, gates, _tile_major(pos, COMBINE_TILE), y_buf, ln_g.reshape(1, D), ln_b.reshape(1, D))


def _trunk(x_parts, trunks, p):
    x = _embed_ln(x_parts[0], x_parts[1], p["emb_ln_g"], p["emb_ln_b"])
    T, D = x.shape
    mc = jnp.asarray(_fnet_channel_np(), BF16)
    for l in range(DEPTH):
        w_in = p["w_in"][l]
        wa = w_in[:, :SSD_WIDTH + SSD_XBC].astype(BF16)
        wdt = jnp.pad(w_in[:, SSD_WIDTH + SSD_XBC:SSD_IN], ((0, 0), (0, LANES - 2 * SSD_HEADS))).astype(BF16)
        wf = w_in[:, SSD_IN:SSD_IN + FNET_WIDTH].astype(BF16)
        wuv = w_in[:, SSD_IN + FNET_WIDTH:].astype(BF16)
        z, xbc, dt, fr, fi, uv = _inproj(x, wa, wdt, wf, wuv, mc)

        lane_pad = LANES - 2 * SSD_HEADS
        dtb = jnp.pad(p["dt_bias"][l].reshape(1, -1), ((0, 0), (0, lane_pad)))
        a_neg = jnp.pad(-jnp.exp(p["a_log"][l].astype(F32)).reshape(1, -1), ((0, 0), (0, lane_pad)))
        dskip_w = jnp.repeat(p["d_skip"][l], SSD_HEAD_DIM).reshape(1, SSD_WIDTH)
        normg = p["ssd_norm_g"][l].reshape(1, SSD_WIDTH)
        convw = p["conv_w"][l]
        convb = p["conv_b"][l].reshape(1, SSD_XBC)
        w_bd = jax.scipy.linalg.block_diag(*[p["fnet_w"][l, g] for g in range(FNET_GROUPS)]).astype(BF16)
        b_f = p["fnet_b"][l].reshape(1, FNET_WIDTH)

        ssd_parts, fn_parts = [], []
        row0 = 0
        for (B, L) in trunks:
            ssd_parts.append(_ssd_mixer(z, xbc, dt, row0, B, L, convw, convb, dtb, a_neg, dskip_w, normg))
            fn_parts.append(_fourier_mixer(fr, fi, row0, B, L, w_bd, b_f))
            row0 += B * L

        ws_cat = jnp.concatenate([p["gmlp_ws"][l, h] for h in range(GMLP_HEADS)], axis=1).astype(BF16)
        bs_wide = jnp.repeat(p["gmlp_bs"][l].T, GMLP_HDIM, axis=1)
        gm = _gmlp_mixer(uv, p["gmlp_ln_g"][l].reshape(1, -1), p["gmlp_ln_b"][l].reshape(1, -1),
                         ws_cat, bs_wide)

        w_r = jnp.pad(p["router_w"][l], ((0, 0), (0, LANES - N_EXPERTS)))
        wrh = w_r.astype(BF16)
        wrl = (w_r - wrh.astype(F32)).astype(BF16)
        b_r = jnp.pad(p["router_b"][l].reshape(1, -1), ((0, 0), (0, LANES - N_EXPERTS)),
                      constant_values=NEG_BIG)
        x1, idx, gates, counts = _outproj(
            x, ssd_parts, fn_parts, gm, p["w_out"][l].astype(BF16), p["b_out"][l].reshape(1, D),
            p["ln1_g"][l].reshape(1, D), p["ln1_b"][l].reshape(1, D), wrh, wrl, b_r)

        x = _moe(x1, idx, gates, counts, l, p["exp_w_gu"], p["exp_b_gu"], p["exp_w_dn"], p["exp_b_dn"],
                 p["ln2_g"][l], p["ln2_b"][l])
    return x


def kernel(x_prompt, x_sample, emb_ln_g, emb_ln_b, w_in, conv_w, conv_b, dt_bias, a_log, d_skip, ssd_norm_g, fnet_w, fnet_b, gmlp_ln_g, gmlp_ln_b, gmlp_ws, gmlp_bs, w_out, b_out, ln1_g, ln1_b, router_w, router_b, exp_w_gu, exp_b_gu, exp_w_dn, exp_b_dn, ln2_g, ln2_b):
    p = dict(emb_ln_g=emb_ln_g, emb_ln_b=emb_ln_b, w_in=w_in, conv_w=conv_w, conv_b=conv_b,
             dt_bias=dt_bias, a_log=a_log, d_skip=d_skip, ssd_norm_g=ssd_norm_g, fnet_w=fnet_w,
             fnet_b=fnet_b, gmlp_ln_g=gmlp_ln_g, gmlp_ln_b=gmlp_ln_b, gmlp_ws=gmlp_ws, gmlp_bs=gmlp_bs,
             w_out=w_out, b_out=b_out, ln1_g=ln1_g, ln1_b=ln1_b, router_w=router_w, router_b=router_b,
             exp_w_gu=exp_w_gu, exp_b_gu=exp_b_gu, exp_w_dn=exp_w_dn, exp_b_dn=exp_b_dn,
             ln2_g=ln2_g, ln2_b=ln2_b)
    D = x_prompt.shape[-1]
    trunks = [x_prompt.shape[:2], x_sample.shape[:2]]
    y = _trunk([x_prompt.reshape(-1, D), x_sample.reshape(-1, D)], trunks, p)
    n_p = x_prompt.shape[0] * x_prompt.shape[1]
    return (y[:n_p].reshape(x_prompt.shape), y[n_p:].reshape(x_sample.shape))
```

```python
import functools
import math

import numpy as np
import jax
import jax.numpy as jnp
from jax import lax
from jax.experimental import pallas as pl
from jax.experimental.pallas import tpu as pltpu

F32 = jnp.float32
BF16 = jnp.bfloat16

D_MODEL = 1024
DEPTH = 4
SSD_WIDTH = 512
SSD_HEAD_DIM = 64
SSD_HEADS = 8
SSD_GROUPS = 2
SSD_STATE = 64
SSD_CONV = 5
SSD_GN = SSD_GROUPS * SSD_STATE
SSD_XBC = SSD_WIDTH + 2 * SSD_GN
SSD_IN = SSD_WIDTH + SSD_XBC + 2 * SSD_HEADS
FNET_WIDTH = 256
FNET_GROUPS = 4
FNET_GDIM = 64
GMLP_WIDTH = 256
GMLP_HEADS = 4
GMLP_HDIM = 64
N_EXPERTS = 32
TOP_K = 4
D_EXPERT = D_MODEL
SWIGLU_LIMIT = 7.0
SWIGLU_ALPHA = 1.702
LN_EPS = 1e-5
RMS_EPS = 1e-5
DN_ALPHA = (2 * DEPTH) ** 0.25

LANES = 128
CHUNK = 128
HALO = 16
VMEM_LIMIT = 56 * 1024 * 1024

TOK_TILE = 512
EXPERT_TILE = 512
EXPERT_COL_CHUNK = 512
DISPATCH_TILE = 512
COMBINE_TILE = 256
COMBINE_SUB = 64
COMBINE_LAG = 2
ROW_UNROLL = 8
SSD_STEP_CHUNKS = 8
FNET_ROW_GROUP = 16
NEG_BIG = -1e30


def _dot(a, b):
    return jnp.dot(a, b, preferred_element_type=F32)


def _split2(v):
    hi = v.astype(BF16)
    lo = (v - hi.astype(F32)).astype(BF16)
    return hi, lo


def _split3(v):
    hi = v.astype(BF16)
    r = v - hi.astype(F32)
    mid = r.astype(BF16)
    lo = (r - mid.astype(F32)).astype(BF16)
    return hi, mid, lo


def _ln_rows(xf, g, b):
    mu = jnp.mean(xf, -1, keepdims=True)
    xc = xf - mu
    var = jnp.mean(xc * xc, -1, keepdims=True)
    return xc * lax.rsqrt(var + LN_EPS) * g + b


def _params(n_parallel=1, n_arbitrary=0):
    sem = ("parallel",) * n_parallel + ("arbitrary",) * n_arbitrary
    return pltpu.CompilerParams(dimension_semantics=sem, vmem_limit_bytes=VMEM_LIMIT)


def _full(shape):
    nd = len(shape)
    return pl.BlockSpec(shape, lambda *_: (0,) * nd)


def _embed_ln_kernel(xa_ref, xb_ref, g_ref, b_ref, o_ref, *, tiles_a):
    i = pl.program_id(0)

    @pl.when(i < tiles_a)
    def _():
        o_ref[...] = _ln_rows(xa_ref[...], g_ref[...], b_ref[...])

    @pl.when(i >= tiles_a)
    def _():
        o_ref[...] = _ln_rows(xb_ref[...], g_ref[...], b_ref[...])


def _embed_ln(xa, xb, g, b):
    D = xa.shape[1]
    na, nb = xa.shape[0] // TOK_TILE, xb.shape[0] // TOK_TILE
    return pl.pallas_call(
        functools.partial(_embed_ln_kernel, tiles_a=na),
        grid=(na + nb,),
        in_specs=[pl.BlockSpec((TOK_TILE, D), lambda i: (jnp.minimum(i, na - 1), 0)),
                  pl.BlockSpec((TOK_TILE, D), lambda i: (jnp.maximum(i - na, 0), 0)),
                  _full((1, D)), _full((1, D))],
        out_specs=pl.BlockSpec((TOK_TILE, D), lambda i: (i, 0)),
        out_shape=jax.ShapeDtypeStruct(((na + nb) * TOK_TILE, D), F32),
        compiler_params=_params(1),
        name="embed_ln",
    )(xa, xb, g.reshape(1, D), b.reshape(1, D))


def _inproj_kernel(x_ref, wa_ref, wdt_ref, wf_ref, wuv_ref, mc_ref,
                   z_ref, xbc_ref, dt_ref, fr_ref, fi_ref, uv_ref):
    xb = x_ref[...].astype(BF16)
    a = _dot(xb, wa_ref[...])
    z_ref[...] = a[:, :SSD_WIDTH].astype(BF16)
    xbc_ref[...] = a[:, SSD_WIDTH:].astype(BF16)
    dt_ref[...] = _dot(xb, wdt_ref[...])
    f = _dot(xb, wf_ref[...]).astype(BF16)
    fri = _dot(f, mc_ref[...])
    fr_ref[...] = fri[:, :FNET_WIDTH]
    fi_ref[...] = fri[:, FNET_WIDTH:]
    uv_ref[...] = _dot(xb, wuv_ref[...]).astype(BF16)


def _inproj(x, wa, wdt, wf, wuv, mc):
    T, D = x.shape
    tm = TOK_TILE
    row = lambda w: pl.BlockSpec((tm, w), lambda i: (i, 0))
    return pl.pallas_call(
        _inproj_kernel,
        grid=(T // tm,),
        in_specs=[row(D), _full(wa.shape), _full(wdt.shape), _full(wf.shape), _full(wuv.shape),
                  _full(mc.shape)],
        out_specs=[row(SSD_WIDTH), row(SSD_XBC), row(LANES), row(FNET_WIDTH), row(FNET_WIDTH),
                   row(2 * GMLP_WIDTH)],
        out_shape=[jax.ShapeDtypeStruct((T, SSD_WIDTH), BF16),
                   jax.ShapeDtypeStruct((T, SSD_XBC), BF16),
                   jax.ShapeDtypeStruct((T, LANES), F32),
                   jax.ShapeDtypeStruct((T, FNET_WIDTH), F32),
                   jax.ShapeDtypeStruct((T, FNET_WIDTH), F32),
                   jax.ShapeDtypeStruct((T, 2 * GMLP_WIDTH), BF16)],
        compiler_params=_params(1),
        name="inproj",
    )(x, wa, wdt, wf, wuv, mc)


def _conv_silu(xc_ref, xp_ref, xn_ref, convw_ref, convb_ref, first, last):
    Q = xc_ref.shape[0]
    cur = xc_ref[...].astype(F32)
    prev = jnp.where(first, 0.0, xp_ref[...].astype(F32))
    nxt = jnp.where(last, 0.0, xn_ref[...].astype(F32))
    ext = jnp.concatenate([prev[HALO - 8:], cur, nxt[:8]], axis=0)
    n_ext = Q + 16
    pad = SSD_CONV // 2
    conv = convb_ref[...]
    for k in range(SSD_CONV):
        shift = (pad - k) % n_ext
        tap = ext if shift == 0 else pltpu.roll(ext, shift, axis=0)
        conv = conv + tap[8:8 + Q] * convw_ref[k:k + 1, :]
    return conv * jax.nn.sigmoid(conv)


def _ssd_chunk(act, dt_raw, dtb_ref, a_ref, cum_ref, expand_ref, state_ref, *, backward):
    Q = CHUNK
    h_off = SSD_HEADS if backward else 0
    xs = act[:, :SSD_WIDTH]
    bm = act[:, SSD_WIDTH:SSD_WIDTH + SSD_GN]
    cm = act[:, SSD_WIDTH + SSD_GN:]

    dt = jax.nn.softplus(dt_raw + dtb_ref[...])
    adt = dt * a_ref[...]
    cum = cum_ref[...]
    h3 = _split3(adt)
    acum = _dot(cum, h3[0]) + _dot(cum, h3[1]) + _dot(cum, h3[2])
    acum_t = acum.T
    total = acum[0:1, :] if backward else acum[Q - 1:Q, :]
    dec_in = jnp.exp(acum)
    dec_end = jnp.exp(total - acum)

    expand = expand_ref[...]

    def widen(v):
        hi, lo = _split2(v)
        return _dot(hi, expand) + _dot(lo, expand)

    dt_w = widen(dt)
    dec_in_w = widen(dec_in)
    dec_end_w = widen(dec_end)
    tdec_w = dec_in_w[0:1, :] if backward else dec_in_w[Q - 1:Q, :]

    xdt = xs * dt_w

    row = lax.broadcasted_iota(jnp.int32, (Q, Q), 0)
    col = lax.broadcasted_iota(jnp.int32, (Q, Q), 1)
    mask = (col > row) if backward else (col <= row)
    lane = lax.broadcasted_iota(jnp.int32, (Q, LANES), 1)
    lo_half = lane < SSD_STATE

    bm_b = bm.astype(BF16)
    cb = []
    for g in range(SSD_GROUPS):
        cm_g = jnp.where(lo_half if g == 0 else jnp.logical_not(lo_half), cm, 0.0).astype(BF16)
        cb.append(lax.dot_general(cm_g, bm_b, (((1,), (1,)), ((), ())),
                                  preferred_element_type=F32))

    y_parts = []
    for pair in range(SSD_HEADS // 2):
        gmats = []
        for r in range(2):
            h = 2 * pair + r
            hl = h_off + h
            diff = acum[:, hl:hl + 1] - acum_t[hl:hl + 1, :]
            dec = jnp.exp(jnp.where(mask, diff, NEG_BIG))
            gmats.append((cb[h // (SSD_HEADS // SSD_GROUPS)] * dec).astype(BF16))
        lhs = jnp.concatenate(gmats, axis=1)
        x2 = xdt[:, pair * LANES:(pair + 1) * LANES]
        rhs = jnp.concatenate([jnp.where(lo_half, x2, 0.0), jnp.where(lo_half, 0.0, x2)],
                              axis=0).astype(BF16)
        y_parts.append(_dot(lhs, rhs))
    y_diag = jnp.concatenate(y_parts, axis=1)

    srow = lax.broadcasted_iota(jnp.int32, (SSD_GN, SSD_WIDTH), 0)
    scol = lax.broadcasted_iota(jnp.int32, (SSD_GN, SSD_WIDTH), 1)
    on_group = (srow // SSD_STATE) == (scol // (SSD_WIDTH // SSD_GROUPS))
    st_new = _dot(bm.T.astype(BF16), (xdt * dec_end_w).astype(BF16))
    st_new = jnp.where(on_group, st_new, 0.0)
    st_prev = state_ref[...]
    y_off = _dot(cm.astype(BF16), st_prev.astype(BF16)) * dec_in_w
    state_ref[...] = st_prev * tdec_w + st_new
    return y_diag + y_off, xs


def _ssd_bwd_kernel(xc_ref, xp_ref, xn_ref, dt_ref, convw_ref, convb_ref, dtb_ref, a_ref,
                    cum_ref, expand_ref, yb_ref, act_ref, state_ref, *, n_steps):
    c = pl.program_id(1)
    step = n_steps - 1 - c

    @pl.when(c == 0)
    def _():
        state_ref[...] = jnp.zeros_like(state_ref)

    act = _conv_silu(xc_ref, xp_ref, xn_ref, convw_ref, convb_ref,
                     first=step == 0, last=step == n_steps - 1)
    act_ref[...] = act.astype(BF16)
    for s in reversed(range(SSD_STEP_CHUNKS)):
        rows = slice(s * CHUNK, (s + 1) * CHUNK)
        y, _ = _ssd_chunk(act[rows], dt_ref[rows, :], dtb_ref, a_ref, cum_ref, expand_ref, state_ref,
                          backward=True)
        yb_ref[rows, :] = y


def _ssd_fwd_kernel(act_ref, dt_ref, z_ref, yb_ref, dtb_ref, a_ref, cum_ref, expand_ref,
                    dskip_ref, normg_ref, o_ref, state_ref):
    c = pl.program_id(1)

    @pl.when(c == 0)
    def _():
        state_ref[...] = jnp.zeros_like(state_ref)

    gw = SSD_WIDTH // SSD_GROUPS
    for s in range(SSD_STEP_CHUNKS):
        rows = slice(s * CHUNK, (s + 1) * CHUNK)
        y, xs = _ssd_chunk(act_ref[rows, :].astype(F32), dt_ref[rows, :], dtb_ref, a_ref, cum_ref,
                           expand_ref, state_ref, backward=False)
        z = z_ref[rows, :].astype(F32)
        y = (y + yb_ref[rows, :] + xs * dskip_ref[...]) * (z * jax.nn.sigmoid(z))
        outs = []
        for g in range(SSD_GROUPS):
            yg = y[:, g * gw:(g + 1) * gw]
            outs.append(yg * lax.rsqrt(jnp.mean(yg * yg, -1, keepdims=True) + RMS_EPS))
        o_ref[rows, :] = (jnp.concatenate(outs, axis=1) * normg_ref[...]).astype(BF16)


def _ssd_mixer(z, xbc, dt, row0, B, L, convw, convb, dtb, a_neg, dskip_w, normg):
    R = SSD_STEP_CHUNKS * CHUNK
    C = L // R
    c0 = row0 // R
    hpc = R // HALO
    n_halo = z.shape[0] // HALO
    consts = _ssd_consts()

    def specs(chunk_of):
        cur = lambda w: pl.BlockSpec((R, w), lambda b, c: (c0 + b * C + chunk_of(c), 0))
        prv = pl.BlockSpec((HALO, SSD_XBC),
                           lambda b, c: (jnp.maximum((c0 + b * C + chunk_of(c)) * hpc - 1, 0), 0))
        nxt = pl.BlockSpec((HALO, SSD_XBC),
                           lambda b, c: (jnp.minimum((c0 + b * C + chunk_of(c) + 1) * hpc, n_halo - 1), 0))
        return cur, prv, nxt

    out_row = lambda chunk_of, w: pl.BlockSpec((R, w), lambda b, c: (b * C + chunk_of(c), 0))

    rev = lambda c: C - 1 - c
    cur, prv, nxt = specs(rev)
    yb, act = pl.pallas_call(
        functools.partial(_ssd_bwd_kernel, n_steps=C),
        grid=(B, C),
        in_specs=[cur(SSD_XBC), prv, nxt, cur(LANES), _full(convw.shape), _full(convb.shape),
                  _full(dtb.shape), _full(a_neg.shape),
                  _full(consts["cum_b"].shape), _full(consts["expand_b"].shape)],
        out_specs=[out_row(rev, SSD_WIDTH), out_row(rev, SSD_XBC)],
        out_shape=[jax.ShapeDtypeStruct((B * L, SSD_WIDTH), F32),
                   jax.ShapeDtypeStruct((B * L, SSD_XBC), BF16)],
        scratch_shapes=[pltpu.VMEM((SSD_GN, SSD_WIDTH), F32)],
        compiler_params=_params(1, 1),
        name="ssd_bwd",
    )(xbc, xbc, xbc, dt, convw, convb, dtb, a_neg, consts["cum_b"], consts["expand_b"])

    fwd = lambda c: c
    cur, _, _ = specs(fwd)
    return pl.pallas_call(
        _ssd_fwd_kernel,
        grid=(B, C),
        in_specs=[out_row(fwd, SSD_XBC), cur(LANES), cur(SSD_WIDTH), out_row(fwd, SSD_WIDTH),
                  _full(dtb.shape), _full(a_neg.shape),
                  _full(consts["cum_f"].shape), _full(consts["expand_f"].shape),
                  _full(dskip_w.shape), _full(normg.shape)],
        out_specs=out_row(fwd, SSD_WIDTH),
        out_shape=jax.ShapeDtypeStruct((B * L, SSD_WIDTH), BF16),
        scratch_shapes=[pltpu.VMEM((SSD_GN, SSD_WIDTH), F32)],
        compiler_params=_params(1, 1),
        name="ssd_fwd",
    )(act, dt, z, yb, dtb, a_neg, consts["cum_f"], consts["expand_f"], dskip_w, normg)


@functools.lru_cache(maxsize=None)
def _ssd_consts_np():
    q = np.arange(CHUNK)
    cum_f = (q[None, :] <= q[:, None]).astype(np.float32)
    cum_b = (q[None, :] >= q[:, None]).astype(np.float32)
    out = {"cum_f": cum_f, "cum_b": cum_b}
    for name, off in (("expand_f", 0), ("expand_b", SSD_HEADS)):
        e = np.zeros((LANES, SSD_WIDTH), np.float32)
        for h in range(SSD_HEADS):
            e[off + h, h * SSD_HEAD_DIM:(h + 1) * SSD_HEAD_DIM] = 1.0
        out[name] = e
    return out


def _ssd_consts():
    return {k: jnp.asarray(v, BF16) for k, v in _ssd_consts_np().items()}


@functools.lru_cache(maxsize=None)
def _fnet_tables_np(L):
    L2 = CHUNK
    L1 = L // L2
    k1 = np.arange(L1)
    ang1 = 2.0 * np.pi * ((k1[:, None] * k1[None, :]) % L1) / L1
    f1 = np.concatenate([np.cos(ang1), np.sin(ang1)], axis=0) / math.sqrt(L1)
    k = k1[:, None, None] + L1 * np.arange(L2)[None, :, None]
    l2 = np.arange(L2)[None, None, :]
    ang2 = 2.0 * np.pi * ((k * l2) % L) / L
    gc = np.cos(ang2) / math.sqrt(L2)
    gs = np.sin(ang2) / math.sqrt(L2)
    return f1.astype(np.float32), gc.astype(np.float32), gs.astype(np.float32)


@functools.lru_cache(maxsize=None)
def _fnet_channel_np():
    c = np.arange(FNET_GDIM)
    ang = 2.0 * np.pi * ((c[:, None] * c[None, :]) % FNET_GDIM) / FNET_GDIM
    cc = np.cos(ang) / math.sqrt(FNET_GDIM)
    sc = np.sin(ang) / math.sqrt(FNET_GDIM)
    eye = np.eye(FNET_GROUPS)
    return np.concatenate([np.kron(eye, cc), np.kron(eye, -sc)], axis=1).astype(np.float32)


def _fnet_stage1_kernel(f_ref, xr_ref, xi_ref, yr_ref, yi_ref, *, L1, group):
    W = FNET_WIDTH
    f = f_ref[...]
    for j in range(group):
        x = jnp.concatenate([xr_ref[:, j, :], xi_ref[:, j, :]], axis=1).astype(BF16)
        p = _dot(f, x)
        yr_ref[:, j, :] = p[:L1, :W] + p[L1:, W:]
        yi_ref[:, j, :] = p[:L1, W:] - p[L1:, :W]


def _fnet_stage2_kernel(yr_ref, yi_ref, g_ref, w_ref, b_ref, o_ref, *, group):
    for j in range(group):
        y = jnp.concatenate([yr_ref[j], yi_ref[j]], axis=0).astype(BF16)
        zr = _dot(g_ref[j], y)
        o_ref[:, j, :] = _dot(zr.astype(BF16), w_ref[...]) + b_ref[...]


def _fourier_mixer(fr, fi, row0, B, L, w_bd, b_f):
    L2 = CHUNK
    L1 = L // L2
    W = FNET_WIDTH
    f1, gc, gs = _fnet_tables_np(L)
    f1 = jnp.asarray(f1, BF16)
    g_cat = jnp.asarray(np.concatenate([gc, gs], axis=2), BF16)
    xr = fr.reshape(-1, L2, W)
    xi = fi.reshape(-1, L2, W)
    blk0 = row0 // L2 // L1
    grp1 = FNET_ROW_GROUP
    xblk = pl.BlockSpec((L1, grp1, W), lambda b, j: (blk0 + b, j, 0))
    yblk = pl.BlockSpec((L1, grp1, W), lambda b, j: (b, j, 0))
    yr, yi = pl.pallas_call(
        functools.partial(_fnet_stage1_kernel, L1=L1, group=grp1),
        grid=(B, L2 // grp1),
        in_specs=[_full(f1.shape), xblk, xblk],
        out_specs=[yblk, yblk],
        out_shape=[jax.ShapeDtypeStruct((B * L1, L2, W), F32)] * 2,
        compiler_params=_params(2),
        name="fnet_stage1",
    )(f1, xr, xi)
    grp2 = FNET_ROW_GROUP // 2
    n2 = L1 // grp2
    yblk2 = pl.BlockSpec((grp2, L2, W), lambda b, k: (b * n2 + k, 0, 0))
    gblk = pl.BlockSpec((grp2, L2, 2 * L2), lambda b, k: (k, 0, 0))
    out = pl.pallas_call(
        functools.partial(_fnet_stage2_kernel, group=grp2),
        grid=(B, n2),
        in_specs=[yblk2, yblk2, gblk, _full(w_bd.shape), _full(b_f.shape)],
        out_specs=pl.BlockSpec((L2, grp2, W), lambda b, k: (b, k, 0)),
        out_shape=jax.ShapeDtypeStruct((B * L2, L1, W), F32),
        compiler_params=_params(2),
        name="fnet_stage2",
    )(yr, yi, g_cat, w_bd, b_f)
    return out.reshape(B * L, W)


def _gmlp_kernel(uv_ref, g_ref, b_ref, ws_ref, bs_ref, o_ref, *, n_chunks):
    uv = uv_ref[...].astype(F32)
    ge = 0.5 * uv * (1.0 + lax.erf(uv * (1.0 / math.sqrt(2.0))))
    u = ge[:, :GMLP_WIDTH]
    v = _ln_rows(ge[:, GMLP_WIDTH:], g_ref[...], b_ref[...])
    lane = lax.broadcasted_iota(jnp.int32, (CHUNK, GMLP_WIDTH), 1)
    ws = ws_ref[...]
    bs = bs_ref[...]
    for j in range(n_chunks):
        vj = v[j * CHUNK:(j + 1) * CHUNK]
        rhs = jnp.concatenate(
            [jnp.where(lane // GMLP_HDIM == h, vj, 0.0) for h in range(GMLP_HEADS)], axis=0)
        sv = _dot(ws, rhs.astype(BF16)) + bs
        o_ref[j * CHUNK:(j + 1) * CHUNK, :] = (u[j * CHUNK:(j + 1) * CHUNK] * sv).astype(BF16)


def _gmlp_mixer(uv, ln_g, ln_b, ws_cat, bs_wide):
    T = uv.shape[0]
    tm = TOK_TILE
    return pl.pallas_call(
        functools.partial(_gmlp_kernel, n_chunks=tm // CHUNK),
        grid=(T // tm,),
        in_specs=[pl.BlockSpec((tm, 2 * GMLP_WIDTH), lambda i: (i, 0)), _full(ln_g.shape),
                  _full(ln_b.shape), _full(ws_cat.shape), _full(bs_wide.shape)],
        out_specs=pl.BlockSpec((tm, GMLP_WIDTH), lambda i: (i, 0)),
        out_shape=jax.ShapeDtypeStruct((T, GMLP_WIDTH), BF16),
        compiler_params=_params(1),
        name="gmlp",
    )(uv, ln_g, ln_b, ws_cat, bs_wide)


def _outproj_kernel(x_ref, ssd_a_ref, ssd_b_ref, fn_a_ref, fn_b_ref, gm_ref, wo_ref, bo_ref, g_ref, b_ref,
                    wrh_ref, wrl_ref, br_ref, tri_ref, x1_ref, route_ref, gate_ref, cnt_ref, seen_ref,
                    *, tiles_a):
    @pl.when(pl.program_id(0) == 0)
    def _():
        seen_ref[...] = jnp.zeros_like(seen_ref)

    in_a = pl.program_id(0) < tiles_a
    ssd = jnp.where(in_a, ssd_a_ref[...], ssd_b_ref[...])
    fn = jnp.where(in_a, fn_a_ref[...], fn_b_ref[...])
    acc = _dot(ssd, wo_ref[0:SSD_WIDTH, :])
    acc = acc + _dot(fn.astype(BF16), wo_ref[SSD_WIDTH:SSD_WIDTH + FNET_WIDTH, :])
    acc = acc + _dot(gm_ref[...], wo_ref[SSD_WIDTH + FNET_WIDTH:, :])
    h = DN_ALPHA * x_ref[...] + acc + bo_ref[...]
    x1 = _ln_rows(h, g_ref[...], b_ref[...])
    x1_ref[...] = x1

    xh, xl = _split2(x1)
    wrh = wrh_ref[...]
    logits = _dot(xh, wrh) + _dot(xl, wrh) + _dot(xh, wrl_ref[...]) + br_ref[...]
    work = logits.T[:N_EXPERTS]
    n_tok = work.shape[1]
    row = lax.broadcasted_iota(jnp.int32, work.shape, 0)
    vals, idxs = [], []
    for _ in range(TOP_K):
        m = jnp.max(work, axis=0, keepdims=True)
        i = jnp.min(jnp.where(work == m, row, N_EXPERTS), axis=0, keepdims=True)
        vals.append(m)
        idxs.append(i)
        work = jnp.where(row == i, -jnp.inf, work)
    exps = [jnp.exp(v - vals[0]) for v in vals]
    denom = exps[0]
    for e in exps[1:]:
        denom = denom + e

    onehots = [row == i for i in idxs]
    chosen = jnp.zeros(work.shape, F32)
    for oh in onehots:
        chosen = jnp.where(oh, 1.0, chosen)
    seen = seen_ref[...]
    before = _dot(chosen.astype(BF16), tri_ref[...]) + seen[:, 0:1]
    seen = seen + jnp.sum(chosen, axis=1, keepdims=True)
    seen_ref[...] = seen
    cnt_ref[...] = seen

    row8 = lax.broadcasted_iota(jnp.int32, (2 * TOP_K, n_tok), 0)
    route = jnp.zeros((2 * TOP_K, n_tok), jnp.int32)
    gates_t = jnp.zeros((2 * TOP_K, n_tok), F32)
    for k in range(TOP_K):
        rank = jnp.sum(jnp.where(onehots[k], before, 0.0), axis=0, keepdims=True).astype(jnp.int32)
        route = jnp.where(row8 == k, idxs[k], route)
        route = jnp.where(row8 == TOP_K + k, rank, route)
        gates_t = jnp.where(row8 == k, exps[k] / denom, gates_t)
    route_ref[...] = route
    pad = jnp.zeros((LANES - 2 * TOP_K, n_tok), F32)
    gate_ref[...] = jnp.concatenate([gates_t, pad], axis=0).T


def _outproj(x, ssd_parts, fn_parts, gm, wo, bo, g, b, wrh, wrl, br):
    T, D = x.shape
    tm = TOK_TILE
    row = lambda w: pl.BlockSpec((tm, w), lambda i: (i, 0))
    na = ssd_parts[0].shape[0] // tm
    part_a = lambda w: pl.BlockSpec((tm, w), lambda i: (jnp.minimum(i, na - 1), 0))
    part_b = lambda w: pl.BlockSpec((tm, w), lambda i: (jnp.maximum(i - na, 0), 0))
    q = np.arange(tm)
    tri = jnp.asarray((q[:, None] < q[None, :]).astype(np.float32), BF16)
    parts = (*ssd_parts, *fn_parts)
    return pl.pallas_call(
        functools.partial(_outproj_kernel, tiles_a=na),
        grid=(T // tm,),
        in_specs=[row(D), part_a(SSD_WIDTH), part_b(SSD_WIDTH), part_a(FNET_WIDTH), part_b(FNET_WIDTH),
                  row(GMLP_WIDTH), _full(wo.shape),
                  _full(bo.shape), _full(g.shape), _full(b.shape), _full(wrh.shape),
                  _full(wrl.shape), _full(br.shape), _full(tri.shape)],
        out_specs=[row(D), pl.BlockSpec((2 * TOP_K, tm), lambda i: (0, i)), row(LANES),
                   _full((N_EXPERTS, LANES))],
        out_shape=[jax.ShapeDtypeStruct((T, D), F32),
                   jax.ShapeDtypeStruct((2 * TOP_K, T), jnp.int32),
                   jax.ShapeDtypeStruct((T, LANES), F32),
                   jax.ShapeDtypeStruct((N_EXPERTS, LANES), F32)],
        scratch_shapes=[pltpu.VMEM((N_EXPERTS, LANES), F32)],
        compiler_params=_params(0, 1),
        name="outproj_ln_router",
    )(x, *parts, gm, wo, bo, g, b, wrh, wrl, br, tri)


SUB = D_MODEL // LANES


def _to_slot_major(dst_ref, dst_row0, x, n):
    for c in range(SUB):
        dst_ref[pl.ds(dst_row0 + c, n, stride=SUB), :] = x[:, c * LANES:(c + 1) * LANES]


def _from_slot_major(src_ref, src_row0, n):
    return jnp.concatenate([src_ref[pl.ds(src_row0 + c, n, stride=SUB), :] for c in range(SUB)], axis=1)


def _slot_copy(src, dst, sem, src_slot, dst_slot):
    s0 = pl.multiple_of(src_slot * SUB, SUB)
    d0 = pl.multiple_of(dst_slot * SUB, SUB)
    return pltpu.make_async_copy(src.at[pl.ds(s0, SUB), :], dst.at[pl.ds(d0, SUB), :], sem)


def _slots_wait(src, dst, sem, n_slots):
    pltpu.make_async_copy(src.at[pl.ds(0, n_slots * SUB), :], dst.at[pl.ds(0, n_slots * SUB), :], sem).wait()


def _dispatch_kernel(pos_ref, x_ref, xbuf_hbm, xs_ref, sem, *, rows):
    _to_slot_major(xs_ref, 0, x_ref[...], rows)

    def body(j, carry):
        for u in range(ROW_UNROLL):
            r = j * ROW_UNROLL + u
            for k in range(TOP_K):
                _slot_copy(xs_ref, xbuf_hbm, sem, r, pos_ref[k * rows + r]).start(priority=k % 2)
        return carry

    lax.fori_loop(0, rows // ROW_UNROLL, body, 0)
    for _ in range(TOP_K):
        _slots_wait(xs_ref, xbuf_hbm, sem, rows)


def _dispatch(x, pos_flat, n_slots):
    T, D = x.shape
    R = DISPATCH_TILE
    return pl.pallas_call(
        functools.partial(_dispatch_kernel, rows=R),
        grid=(T // R,),
        in_specs=[pl.BlockSpec((R * TOP_K,), lambda i: (i,), memory_space=pltpu.SMEM),
                  pl.BlockSpec((R, D), lambda i: (i, 0))],
        out_specs=pl.BlockSpec(memory_space=pl.ANY),
        out_shape=jax.ShapeDtypeStruct((n_slots * SUB, LANES), F32),
        scratch_shapes=[pltpu.VMEM((R * SUB, LANES), F32), pltpu.SemaphoreType.DMA(())],
        compiler_params=_params(1),
        name="moe_dispatch",
    )(pos_flat, x)


def _expert_kernel(blk_e_ref, valid_ref, x_ref, wgu_ref, bgu_ref, wdn_ref, bdn_ref, y_ref, wgu_b, wdn_b):
    i = pl.program_id(0)
    valid = valid_ref[i]
    prev_e = blk_e_ref[jnp.maximum(i - 1, 0)]
    new_expert = jnp.logical_or(i == 0, blk_e_ref[i] != prev_e)

    @pl.when(jnp.logical_and(valid > 0, new_expert))
    def _():
        wgu_b[...] = wgu_ref[...].astype(BF16)
        wdn_b[...] = wdn_ref[...].astype(BF16)

    @pl.when(valid > 0)
    def _():
        x = _from_slot_major(x_ref, 0, EXPERT_TILE)
        row = lax.broadcasted_iota(jnp.int32, x.shape, 0)
        xb = jnp.where(row < valid, x, 0.0).astype(BF16)
        acc = jnp.broadcast_to(bdn_ref[...], x.shape)
        nc = EXPERT_COL_CHUNK
        for j in range(D_EXPERT // nc):
            g = _dot(xb, wgu_b[:, j * nc:(j + 1) * nc]) + bgu_ref[:, j * nc:(j + 1) * nc]
            u = (_dot(xb, wgu_b[:, D_EXPERT + j * nc:D_EXPERT + (j + 1) * nc])
                 + bgu_ref[:, D_EXPERT + j * nc:D_EXPERT + (j + 1) * nc])
            g = jnp.minimum(g, SWIGLU_LIMIT)
            u = jnp.clip(u, -SWIGLU_LIMIT, SWIGLU_LIMIT)
            act = (u + 1.0) * (g * jax.nn.sigmoid(SWIGLU_ALPHA * g))
            acc = acc + _dot(act.astype(BF16), wdn_b[j * nc:(j + 1) * nc, :])
        _to_slot_major(y_ref, 0, acc, EXPERT_TILE)

    @pl.when(valid <= 0)
    def _():
        y_ref[...] = jnp.zeros_like(y_ref)


def _expert_mlp(x_buf, blk_e, blk_valid, layer, wgu, bgu, wdn, bdn):
    D = D_MODEL
    tm = EXPERT_TILE
    per_expert = lambda r, c: pl.BlockSpec((None, None, r, c), lambda i, be, bv: (layer, be[i], 0, 0))
    slots = pl.BlockSpec((tm * SUB, LANES), lambda i, be, bv: (i, 0))
    grid_spec = pltpu.PrefetchScalarGridSpec(
        num_scalar_prefetch=2,
        grid=(x_buf.shape[0] // (tm * SUB),),
        in_specs=[slots, per_expert(D, 2 * D_EXPERT), per_expert(1, 2 * D_EXPERT),
                  per_expert(D_EXPERT, D), per_expert(1, D)],
        out_specs=slots,
        scratch_shapes=[pltpu.VMEM((D, 2 * D_EXPERT), BF16), pltpu.VMEM((D_EXPERT, D), BF16)],
    )
    return pl.pallas_call(
        _expert_kernel,
        grid_spec=grid_spec,
        out_shape=jax.ShapeDtypeStruct(x_buf.shape, F32),
        compiler_params=_params(0, 1),
        name="moe_expert_mlp",
    )(blk_e, blk_valid, x_buf, wgu, bgu, wdn, bdn)


def _combine_kernel(pos_ref, x_ref, gate_ref, y_hbm, g_ref, b_ref, o_ref, *scratch, rows, n_tiles):
    n_slots = COMBINE_LAG + 1
    bufs, sem = scratch[:n_slots], scratch[n_slots]
    i = pl.program_id(0)
    sub = COMBINE_SUB

    def run(s_issue, s_done):
        def issue(j):
            for u in range(sub):
                r = j * sub + u
                for k in range(TOP_K):
                    _slot_copy(y_hbm, bufs[s_issue].at[k], sem.at[s_issue], pos_ref[k * rows + r], r).start(
                        priority=k % 2)

        def finish(j):
            rs = pl.ds(pl.multiple_of(j * sub, sub), sub)
            row0 = pl.multiple_of(j * (sub * SUB), sub * SUB)
            gates = gate_ref[rs, :]
            y = gates[:, 0:1] * _from_slot_major(bufs[s_done].at[0], row0, sub)
            for k in range(1, TOP_K):
                y = y + gates[:, k:k + 1] * _from_slot_major(bufs[s_done].at[k], row0, sub)
            o_ref[rs, :] = _ln_rows(DN_ALPHA * x_ref[rs, :] + y, g_ref[...], b_ref[...])

        def wait_done():
            for k in range(TOP_K):
                _slots_wait(y_hbm, bufs[s_done].at[k], sem.at[s_done], rows)

        def loop(do_issue, do_finish):
            def body(j, carry):
                if do_issue:
                    issue(j)
                if do_finish:
                    finish(j)
                return carry

            lax.fori_loop(0, rows // sub, body, 0)

        @pl.when(i < COMBINE_LAG)
        def _():
            loop(True, False)

        @pl.when(jnp.logical_and(i >= COMBINE_LAG, i < n_tiles))
        def _():
            wait_done()
            loop(True, True)

        @pl.when(i >= n_tiles)
        def _():
            wait_done()
            loop(False, True)

    for s in range(n_slots):
        pl.when(i % n_slots == s)(functools.partial(run, s, (s + 1) % n_slots))


def _combine(x, gates, pos_flat, y_buf, g, b):
    T, D = x.shape
    R = COMBINE_TILE
    n = T // R
    lag = COMBINE_LAG
    done = lambda i: jnp.maximum(i - lag, 0)
    return pl.pallas_call(
        functools.partial(_combine_kernel, rows=R, n_tiles=n),
        grid=(n + lag,),
        in_specs=[pl.BlockSpec((R * TOP_K,), lambda i: (jnp.minimum(i, n - 1),), memory_space=pltpu.SMEM),
                  pl.BlockSpec((R, D), lambda i: (done(i), 0)),
                  pl.BlockSpec((R, LANES), lambda i: (done(i), 0)),
                  pl.BlockSpec(memory_space=pl.ANY),
                  _full(g.shape), _full(b.shape)],
        out_specs=pl.BlockSpec((R, D), lambda i: (done(i), 0)),
        out_shape=jax.ShapeDtypeStruct((T, D), F32),
        scratch_shapes=[pltpu.VMEM((TOP_K, R * SUB, LANES), F32) for _ in range(lag + 1)]
                       + [pltpu.SemaphoreType.DMA((lag + 1,))],
        compiler_params=_params(0, 1),
        name="moe_combine_ln",
    )(pos_flat, x, gates, y_buf, g, b)


def _tile_major(pos, tile):
    K, T = pos.shape
    return pos.reshape(K, T // tile, tile).transpose(1, 0, 2).reshape(-1)


def _routing_plan(route, counts, n_tokens):
    tm = EXPERT_TILE
    e = route[:TOP_K]
    rank = route[TOP_K:]
    counts = counts[:, 0].astype(jnp.int32)
    padded = (counts + tm - 1) // tm * tm
    pend = jnp.cumsum(padded)
    pstart = pend - padded
    experts = jnp.arange(N_EXPERTS, dtype=jnp.int32)[:, None, None]
    base = jnp.sum(jnp.where(e[None] == experts, pstart[:, None, None], 0), axis=0)
    pos = (base + rank).astype(jnp.int32)
    n_slots = n_tokens * TOP_K + N_EXPERTS * tm
    blk_start = jnp.arange(n_slots // tm, dtype=jnp.int32) * tm
    blk_e = jnp.minimum(jnp.sum(blk_start[:, None] >= pend[None, :], axis=1), N_EXPERTS - 1).astype(jnp.int32)
    blk_valid = jnp.clip(pstart[blk_e] + counts[blk_e] - blk_start, 0, tm).astype(jnp.int32)
    return pos, blk_e, blk_valid, n_slots


def _moe(x1, route, gates, counts, layer, wgu, bgu, wdn, bdn, ln_g, ln_b):
    T, D = x1.shape
    pos, blk_e, blk_valid, n_slots = _routing_plan(route, counts, T)
    x_buf = _dispatch(x1, _tile_major(pos, DISPATCH_TILE), n_slots)
    y_buf = _expert_mlp(x_buf, blk_e, blk_valid, layer, wgu, bgu[:, :, None, :], wdn, bdn[:, :, None, :])
    return _combine(x1, gates, _tile_major(pos, COMBINE_TILE), y_buf, ln_g.reshape(1, D), ln_b.reshape(1, D))


def _trunk(x_parts, trunks, p):
    x = _embed_ln(x_parts[0], x_parts[1], p["emb_ln_g"], p["emb_ln_b"])
    T, D = x.shape
    mc = jnp.asarray(_fnet_channel_np(), BF16)
    for l in range(DEPTH):
        w_in = p["w_in"][l]
        wa = w_in[:, :SSD_WIDTH + SSD_XBC].astype(BF16)
        wdt = jnp.pad(w_in[:, SSD_WIDTH + SSD_XBC:SSD_IN], ((0, 0), (0, LANES - 2 * SSD_HEADS))).astype(BF16)
        wf = w_in[:, SSD_IN:SSD_IN + FNET_WIDTH].astype(BF16)
        wuv = w_in[:, SSD_IN + FNET_WIDTH:].astype(BF16)
        z, xbc, dt, fr, fi, uv = _inproj(x, wa, wdt, wf, wuv, mc)

        lane_pad = LANES - 2 * SSD_HEADS
        dtb = jnp.pad(p["dt_bias"][l].reshape(1, -1), ((0, 0), (0, lane_pad)))
        a_neg = jnp.pad(-jnp.exp(p["a_log"][l].astype(F32)).reshape(1, -1), ((0, 0), (0, lane_pad)))
        dskip_w = jnp.repeat(p["d_skip"][l], SSD_HEAD_DIM).reshape(1, SSD_WIDTH)
        normg = p["ssd_norm_g"][l].reshape(1, SSD_WIDTH)
        convw = p["conv_w"][l]
        convb = p["conv_b"][l].reshape(1, SSD_XBC)
        w_bd = jax.scipy.linalg.block_diag(*[p["fnet_w"][l, g] for g in range(FNET_GROUPS)]).astype(BF16)
        b_f = p["fnet_b"][l].reshape(1, FNET_WIDTH)

        ssd_parts, fn_parts = [], []
        row0 = 0
        for (B, L) in trunks:
            ssd_parts.append(_ssd_mixer(z, xbc, dt, row0, B, L, convw, convb, dtb, a_neg, dskip_w, normg))
            fn_parts.append(_fourier_mixer(fr, fi, row0, B, L, w_bd, b_f))
            row0 += B * L

        ws_cat = jnp.concatenate([p["gmlp_ws"][l, h] for h in range(GMLP_HEADS)], axis=1).astype(BF16)
        bs_wide = jnp.repeat(p["gmlp_bs"][l].T, GMLP_HDIM, axis=1)
        gm = _gmlp_mixer(uv, p["gmlp_ln_g"][l].reshape(1, -1), p["gmlp_ln_b"][l].reshape(1, -1),
                         ws_cat, bs_wide)

        w_r = jnp.pad(p["router_w"][l], ((0, 0), (0, LANES - N_EXPERTS)))
        wrh = w_r.astype(BF16)
        wrl = (w_r - wrh.astype(F32)).astype(BF16)
        b_r = jnp.pad(p["router_b"][l].reshape(1, -1), ((0, 0), (0, LANES - N_EXPERTS)),
                      constant_values=NEG_BIG)
        x1, idx, gates, counts = _outproj(
            x, ssd_parts, fn_parts, gm, p["w_out"][l].astype(BF16), p["b_out"][l].reshape(1, D),
            p["ln1_g"][l].reshape(1, D), p["ln1_b"][l].reshape(1, D), wrh, wrl, b_r)

        x = _moe(x1, idx, gates, counts, l, p["exp_w_gu"], p["exp_b_gu"], p["exp_w_dn"], p["exp_b_dn"],
                 p["ln2_g"][l], p["ln2_b"][l])
    return x


def kernel(x_prompt, x_sample, emb_ln_g, emb_ln_b, w_in, conv_w, conv_b, dt_bias, a_log, d_skip, ssd_norm_g, fnet_w, fnet_b, gmlp_ln_g, gmlp_ln_b, gmlp_ws, gmlp_bs, w_out, b_out, ln1_g, ln1_b, router_w, router_b, exp_w_gu, exp_b_gu, exp_w_dn, exp_b_dn, ln2_g, ln2_b):
    p = dict(emb_ln_g=emb_ln_g, emb_ln_b=emb_ln_b, w_in=w_in, conv_w=conv_w, conv_b=conv_b,
             dt_bias=dt_bias, a_log=a_log, d_skip=d_skip, ssd_norm_g=ssd_norm_g, fnet_w=fnet_w,
             fnet_b=fnet_b, gmlp_ln_g=gmlp_ln_g, gmlp_ln_b=gmlp_ln_b, gmlp_ws=gmlp_ws, gmlp_bs=gmlp_bs,
             w_out=w_out, b_out=b_out, ln1_g=ln1_g, ln1_b=ln1_b, router_w=router_w, router_b=router_b,
             exp_w_gu=exp_w_gu, exp_b_gu=exp_b_gu, exp_w_dn=exp_w_dn, exp_b_dn=exp_b_dn,
             ln2_g=ln2_g, ln2_b=ln2_b)
    D = x_prompt.shape[-1]
    trunks = [x_prompt.shape[:2], x_sample.shape[:2]]
    y = _trunk([x_prompt.reshape(-1, D), x_sample.reshape(-1, D)], trunks, p)
    n_p = x_prompt.shape[0] * x_prompt.shape[1]
    return (y[:n_p].reshape(x_prompt.shape), y[n_p:].reshape(x_sample.shape))
```

```python
import functools
import math

import numpy as np
import jax
import jax.numpy as jnp
from jax import lax
from jax.experimental import pallas as pl
from jax.experimental.pallas import tpu as pltpu

F32 = jnp.float32
BF16 = jnp.bfloat16

D_MODEL = 1024
DEPTH = 4
SSD_WIDTH = 512
SSD_HEAD_DIM = 64
SSD_HEADS = 8
SSD_GROUPS = 2
SSD_STATE = 64
SSD_CONV = 5
SSD_GN = SSD_GROUPS * SSD_STATE
SSD_XBC = SSD_WIDTH + 2 * SSD_GN
SSD_IN = SSD_WIDTH + SSD_XBC + 2 * SSD_HEADS
FNET_WIDTH = 256
FNET_GROUPS = 4
FNET_GDIM = 64
GMLP_WIDTH = 256
GMLP_HEADS = 4
GMLP_HDIM = 64
N_EXPERTS = 32
TOP_K = 4
D_EXPERT = D_MODEL
SWIGLU_LIMIT = 7.0
SWIGLU_ALPHA = 1.702
LN_EPS = 1e-5
RMS_EPS = 1e-5
DN_ALPHA = (2 * DEPTH) ** 0.25

LANES = 128
CHUNK = 128
HALO = 16
VMEM_LIMIT = 56 * 1024 * 1024

TOK_TILE = 512
EXPERT_TILE = 512
EXPERT_COL_CHUNK = 512
DISPATCH_TILE = 512
COMBINE_TILE = 256
COMBINE_SUB = 64
COMBINE_LAG = 2
ROW_UNROLL = 8
SSD_STEP_CHUNKS = 8
FNET_ROW_GROUP = 16
NEG_BIG = -1e30


def _dot(a, b):
    return jnp.dot(a, b, preferred_element_type=F32)


def _split2(v):
    hi = v.astype(BF16)
    lo = (v - hi.astype(F32)).astype(BF16)
    return hi, lo


def _split3(v):
    hi = v.astype(BF16)
    r = v - hi.astype(F32)
    mid = r.astype(BF16)
    lo = (r - mid.astype(F32)).astype(BF16)
    return hi, mid, lo


def _ln_rows(xf, g, b):
    mu = jnp.mean(xf, -1, keepdims=True)
    xc = xf - mu
    var = jnp.mean(xc * xc, -1, keepdims=True)
    return xc * lax.rsqrt(var + LN_EPS) * g + b


def _params(n_parallel=1, n_arbitrary=0):
    sem = ("parallel",) * n_parallel + ("arbitrary",) * n_arbitrary
    return pltpu.CompilerParams(dimension_semantics=sem, vmem_limit_bytes=VMEM_LIMIT)


def _full(shape):
    nd = len(shape)
    return pl.BlockSpec(shape, lambda *_: (0,) * nd)


def _embed_ln_kernel(xa_ref, xb_ref, g_ref, b_ref, o_ref, *, tiles_a):
    i = pl.program_id(0)

    @pl.when(i < tiles_a)
    def _():
        o_ref[...] = _ln_rows(xa_ref[...], g_ref[...], b_ref[...])

    @pl.when(i >= tiles_a)
    def _():
        o_ref[...] = _ln_rows(xb_ref[...], g_ref[...], b_ref[...])


def _embed_ln(xa, xb, g, b):
    D = xa.shape[1]
    na, nb = xa.shape[0] // TOK_TILE, xb.shape[0] // TOK_TILE
    return pl.pallas_call(
        functools.partial(_embed_ln_kernel, tiles_a=na),
        grid=(na + nb,),
        in_specs=[pl.BlockSpec((TOK_TILE, D), lambda i: (jnp.minimum(i, na - 1), 0)),
                  pl.BlockSpec((TOK_TILE, D), lambda i: (jnp.maximum(i - na, 0), 0)),
                  _full((1, D)), _full((1, D))],
        out_specs=pl.BlockSpec((TOK_TILE, D), lambda i: (i, 0)),
        out_shape=jax.ShapeDtypeStruct(((na + nb) * TOK_TILE, D), F32),
        compiler_params=_params(1),
        name="embed_ln",
    )(xa, xb, g.reshape(1, D), b.reshape(1, D))


def _inproj_kernel(x_ref, wa_ref, wdt_ref, wf_ref, wuv_ref, mc_ref,
                   z_ref, xbc_ref, dt_ref, fr_ref, fi_ref, uv_ref):
    xb = x_ref[...].astype(BF16)
    a = _dot(xb, wa_ref[...])
    z_ref[...] = a[:, :SSD_WIDTH].astype(BF16)
    xbc_ref[...] = a[:, SSD_WIDTH:].astype(BF16)
    dt_ref[...] = _dot(xb, wdt_ref[...])
    f = _dot(xb, wf_ref[...]).astype(BF16)
    fri = _dot(f, mc_ref[...])
    fr_ref[...] = fri[:, :FNET_WIDTH]
    fi_ref[...] = fri[:, FNET_WIDTH:]
    uv_ref[...] = _dot(xb, wuv_ref[...]).astype(BF16)


def _inproj(x, wa, wdt, wf, wuv, mc):
    T, D = x.shape
    tm = TOK_TILE
    row = lambda w: pl.BlockSpec((tm, w), lambda i: (i, 0))
    return pl.pallas_call(
        _inproj_kernel,
        grid=(T // tm,),
        in_specs=[row(D), _full(wa.shape), _full(wdt.shape), _full(wf.shape), _full(wuv.shape),
                  _full(mc.shape)],
        out_specs=[row(SSD_WIDTH), row(SSD_XBC), row(LANES), row(FNET_WIDTH), row(FNET_WIDTH),
                   row(2 * GMLP_WIDTH)],
        out_shape=[jax.ShapeDtypeStruct((T, SSD_WIDTH), BF16),
                   jax.ShapeDtypeStruct((T, SSD_XBC), BF16),
                   jax.ShapeDtypeStruct((T, LANES), F32),
                   jax.ShapeDtypeStruct((T, FNET_WIDTH), F32),
                   jax.ShapeDtypeStruct((T, FNET_WIDTH), F32),
                   jax.ShapeDtypeStruct((T, 2 * GMLP_WIDTH), BF16)],
        compiler_params=_params(1),
        name="inproj",
    )(x, wa, wdt, wf, wuv, mc)


def _conv_silu(xc_ref, xp_ref, xn_ref, convw_ref, convb_ref, first, last):
    Q = xc_ref.shape[0]
    cur = xc_ref[...].astype(F32)
    prev = jnp.where(first, 0.0, xp_ref[...].astype(F32))
    nxt = jnp.where(last, 0.0, xn_ref[...].astype(F32))
    ext = jnp.concatenate([prev[HALO - 8:], cur, nxt[:8]], axis=0)
    n_ext = Q + 16
    pad = SSD_CONV // 2
    conv = convb_ref[...]
    for k in range(SSD_CONV):
        shift = (pad - k) % n_ext
        tap = ext if shift == 0 else pltpu.roll(ext, shift, axis=0)
        conv = conv + tap[8:8 + Q] * convw_ref[k:k + 1, :]
    return conv * jax.nn.sigmoid(conv)


def _ssd_chunk(act, dt_raw, dtb_ref, a_ref, cum_ref, expand_ref, state_ref, *, backward):
    Q = CHUNK
    h_off = SSD_HEADS if backward else 0
    xs = act[:, :SSD_WIDTH]
    bm = act[:, SSD_WIDTH:SSD_WIDTH + SSD_GN]
    cm = act[:, SSD_WIDTH + SSD_GN:]

    dt = jax.nn.softplus(dt_raw + dtb_ref[...])
    adt = dt * a_ref[...]
    cum = cum_ref[...]
    h3 = _split3(adt)
    acum = _dot(cum, h3[0]) + _dot(cum, h3[1]) + _dot(cum, h3[2])
    acum_t = acum.T
    total = acum[0:1, :] if backward else acum[Q - 1:Q, :]
    dec_in = jnp.exp(acum)
    dec_end = jnp.exp(total - acum)

    expand = expand_ref[...]

    def widen(v):
        hi, lo = _split2(v)
        return _dot(hi, expand) + _dot(lo, expand)

    dt_w = widen(dt)
    dec_in_w = widen(dec_in)
    dec_end_w = widen(dec_end)
    tdec_w = dec_in_w[0:1, :] if backward else dec_in_w[Q - 1:Q, :]

    xdt = xs * dt_w

    row = lax.broadcasted_iota(jnp.int32, (Q, Q), 0)
    col = lax.broadcasted_iota(jnp.int32, (Q, Q), 1)
    mask = (col > row) if backward else (col <= row)
    lane = lax.broadcasted_iota(jnp.int32, (Q, LANES), 1)
    lo_half = lane < SSD_STATE

    bm_b = bm.astype(BF16)
    cb = []
    for g in range(SSD_GROUPS):
        cm_g = jnp.where(lo_half if g == 0 else jnp.logical_not(lo_half), cm, 0.0).astype(BF16)
        cb.append(lax.dot_general(cm_g, bm_b, (((1,), (1,)), ((), ())),
                                  preferred_element_type=F32))

    y_parts = []
    for pair in range(SSD_HEADS // 2):
        gmats = []
        for r in range(2):
            h = 2 * pair + r
            hl = h_off + h
            diff = acum[:, hl:hl + 1] - acum_t[hl:hl + 1, :]
            dec = jnp.exp(jnp.where(mask, diff, NEG_BIG))
            gmats.append((cb[h // (SSD_HEADS // SSD_GROUPS)] * dec).astype(BF16))
        lhs = jnp.concatenate(gmats, axis=1)
        x2 = xdt[:, pair * LANES:(pair + 1) * LANES]
        rhs = jnp.concatenate([jnp.where(lo_half, x2, 0.0), jnp.where(lo_half, 0.0, x2)],
                              axis=0).astype(BF16)
        y_parts.append(_dot(lhs, rhs))
    y_diag = jnp.concatenate(y_parts, axis=1)

    srow = lax.broadcasted_iota(jnp.int32, (SSD_GN, SSD_WIDTH), 0)
    scol = lax.broadcasted_iota(jnp.int32, (SSD_GN, SSD_WIDTH), 1)
    on_group = (srow // SSD_STATE) == (scol // (SSD_WIDTH // SSD_GROUPS))
    st_new = _dot(bm.T.astype(BF16), (xdt * dec_end_w).astype(BF16))
    st_new = jnp.where(on_group, st_new, 0.0)
    st_prev = state_ref[...]
    y_off = _dot(cm.astype(BF16), st_prev.astype(BF16)) * dec_in_w
    state_ref[...] = st_prev * tdec_w + st_new
    return y_diag + y_off, xs


def _ssd_bwd_kernel(xc_ref, xp_ref, xn_ref, dt_ref, convw_ref, convb_ref, dtb_ref, a_ref,
                    cum_ref, expand_ref, yb_ref, act_ref, state_ref, *, n_steps):
    c = pl.program_id(1)
    step = n_steps - 1 - c

    @pl.when(c == 0)
    def _():
        state_ref[...] = jnp.zeros_like(state_ref)

    act = _conv_silu(xc_ref, xp_ref, xn_ref, convw_ref, convb_ref,
                     first=step == 0, last=step == n_steps - 1)
    act_ref[...] = act.astype(BF16)
    for s in reversed(range(SSD_STEP_CHUNKS)):
        rows = slice(s * CHUNK, (s + 1) * CHUNK)
        y, _ = _ssd_chunk(act[rows], dt_ref[rows, :], dtb_ref, a_ref, cum_ref, expand_ref, state_ref,
                          backward=True)
        yb_ref[rows, :] = y


def _ssd_fwd_kernel(act_ref, dt_ref, z_ref, yb_ref, dtb_ref, a_ref, cum_ref, expand_ref,
                    dskip_ref, normg_ref, o_ref, state_ref):
    c = pl.program_id(1)

    @pl.when(c == 0)
    def _():
        state_ref[...] = jnp.zeros_like(state_ref)

    gw = SSD_WIDTH // SSD_GROUPS
    for s in range(SSD_STEP_CHUNKS):
        rows = slice(s * CHUNK, (s + 1) * CHUNK)
        y, xs = _ssd_chunk(act_ref[rows, :].astype(F32), dt_ref[rows, :], dtb_ref, a_ref, cum_ref,
                           expand_ref, state_ref, backward=False)
        z = z_ref[rows, :].astype(F32)
        y = (y + yb_ref[rows, :] + xs * dskip_ref[...]) * (z * jax.nn.sigmoid(z))
        outs = []
        for g in range(SSD_GROUPS):
            yg = y[:, g * gw:(g + 1) * gw]
            outs.append(yg * lax.rsqrt(jnp.mean(yg * yg, -1, keepdims=True) + RMS_EPS))
        o_ref[rows, :] = (jnp.concatenate(outs, axis=1) * normg_ref[...]).astype(BF16)


def _ssd_mixer(z, xbc, dt, row0, B, L, convw, convb, dtb, a_neg, dskip_w, normg):
    R = SSD_STEP_CHUNKS * CHUNK
    C = L // R
    c0 = row0 // R
    hpc = R // HALO
    n_halo = z.shape[0] // HALO
    consts = _ssd_consts()

    def specs(chunk_of):
        cur = lambda w: pl.BlockSpec((R, w), lambda b, c: (c0 + b * C + chunk_of(c), 0))
        prv = pl.BlockSpec((HALO, SSD_XBC),
                           lambda b, c: (jnp.maximum((c0 + b * C + chunk_of(c)) * hpc - 1, 0), 0))
        nxt = pl.BlockSpec((HALO, SSD_XBC),
                           lambda b, c: (jnp.minimum((c0 + b * C + chunk_of(c) + 1) * hpc, n_halo - 1), 0))
        return cur, prv, nxt

    out_row = lambda chunk_of, w: pl.BlockSpec((R, w), lambda b, c: (b * C + chunk_of(c), 0))

    rev = lambda c: C - 1 - c
    cur, prv, nxt = specs(rev)
    yb, act = pl.pallas_call(
        functools.partial(_ssd_bwd_kernel, n_steps=C),
        grid=(B, C),
        in_specs=[cur(SSD_XBC), prv, nxt, cur(LANES), _full(convw.shape), _full(convb.shape),
                  _full(dtb.shape), _full(a_neg.shape),
                  _full(consts["cum_b"].shape), _full(consts["expand_b"].shape)],
        out_specs=[out_row(rev, SSD_WIDTH), out_row(rev, SSD_XBC)],
        out_shape=[jax.ShapeDtypeStruct((B * L, SSD_WIDTH), F32),
                   jax.ShapeDtypeStruct((B * L, SSD_XBC), BF16)],
        scratch_shapes=[pltpu.VMEM((SSD_GN, SSD_WIDTH), F32)],
        compiler_params=_params(1, 1),
        name="ssd_bwd",
    )(xbc, xbc, xbc, dt, convw, convb, dtb, a_neg, consts["cum_b"], consts["expand_b"])

    fwd = lambda c: c
    cur, _, _ = specs(fwd)
    return pl.pallas_call(
        _ssd_fwd_kernel,
        grid=(B, C),
        in_specs=[out_row(fwd, SSD_XBC), cur(LANES), cur(SSD_WIDTH), out_row(fwd, SSD_WIDTH),
                  _full(dtb.shape), _full(a_neg.shape),
                  _full(consts["cum_f"].shape), _full(consts["expand_f"].shape),
                  _full(dskip_w.shape), _full(normg.shape)],
        out_specs=out_row(fwd, SSD_WIDTH),
        out_shape=jax.ShapeDtypeStruct((B * L, SSD_WIDTH), BF16),
        scratch_shapes=[pltpu.VMEM((SSD_GN, SSD_WIDTH), F32)],
        compiler_params=_params(1, 1),
        name="ssd_fwd",
    )(act, dt, z, yb, dtb, a_neg, consts["cum_f"], consts["expand_f"], dskip_w, normg)


@functools.lru_cache(maxsize=None)
def _ssd_consts_np():
    q = np.arange(CHUNK)
    cum_f = (q[None, :] <= q[:, None]).astype(np.float32)
    cum_b = (q[None, :] >= q[:, None]).astype(np.float32)
    out = {"cum_f": cum_f, "cum_b": cum_b}
    for name, off in (("expand_f", 0), ("expand_b", SSD_HEADS)):
        e = np.zeros((LANES, SSD_WIDTH), np.float32)
        for h in range(SSD_HEADS):
            e[off + h, h * SSD_HEAD_DIM:(h + 1) * SSD_HEAD_DIM] = 1.0
        out[name] = e
    return out


def _ssd_consts():
    return {k: jnp.asarray(v, BF16) for k, v in _ssd_consts_np().items()}


@functools.lru_cache(maxsize=None)
def _fnet_tables_np(L):
    L2 = CHUNK
    L1 = L // L2
    k1 = np.arange(L1)
    ang1 = 2.0 * np.pi * ((k1[:, None] * k1[None, :]) % L1) / L1
    f1 = np.concatenate([np.cos(ang1), np.sin(ang1)], axis=0) / math.sqrt(L1)
    k = k1[:, None, None] + L1 * np.arange(L2)[None, :, None]
    l2 = np.arange(L2)[None, None, :]
    ang2 = 2.0 * np.pi * ((k * l2) % L) / L
    gc = np.cos(ang2) / math.sqrt(L2)
    gs = np.sin(ang2) / math.sqrt(L2)
    return f1.astype(np.float32), gc.astype(np.float32), gs.astype(np.float32)


@functools.lru_cache(maxsize=None)
def _fnet_channel_np():
    c = np.arange(FNET_GDIM)
    ang = 2.0 * np.pi * ((c[:, None] * c[None, :]) % FNET_GDIM) / FNET_GDIM
    cc = np.cos(ang) / math.sqrt(FNET_GDIM)
    sc = np.sin(ang) / math.sqrt(FNET_GDIM)
    eye = np.eye(FNET_GROUPS)
    return np.concatenate([np.kron(eye, cc), np.kron(eye, -sc)], axis=1).astype(np.float32)


def _fnet_stage1_kernel(f_ref, xr0_ref, xr1_ref, xi0_ref, xi1_ref, yr0_ref, yr1_ref, yi0_ref, yi1_ref,
                        *, L1, group):
    f = f_ref[...]
    flat = lambda ref: ref.reshape(L1 * group, LANES)
    xs = [flat(r) for r in (xr0_ref, xr1_ref, xi0_ref, xi1_ref)]
    ys = [flat(r) for r in (yr0_ref, yr1_ref, yi0_ref, yi1_ref)]
    W = FNET_WIDTH
    for j in range(group):
        rows = pl.ds(j, L1, stride=group)
        x = jnp.concatenate([r[rows, :] for r in xs], axis=1).astype(BF16)
        p = _dot(f, x)
        yr = p[:L1, :W] + p[L1:, W:]
        yi = p[:L1, W:] - p[L1:, :W]
        ys[0][rows, :] = yr[:, :LANES]
        ys[1][rows, :] = yr[:, LANES:]
        ys[2][rows, :] = yi[:, :LANES]
        ys[3][rows, :] = yi[:, LANES:]


def _fnet_stage2_kernel(yr0_ref, yr1_ref, yi0_ref, yi1_ref, g_ref, w_ref, b_ref, o0_ref, o1_ref, *, group):
    L2 = o0_ref.shape[0]
    o0 = o0_ref.reshape(L2 * group, LANES)
    o1 = o1_ref.reshape(L2 * group, LANES)
    for j in range(group):
        y = jnp.concatenate([jnp.concatenate([yr0_ref[j], yr1_ref[j]], axis=1),
                             jnp.concatenate([yi0_ref[j], yi1_ref[j]], axis=1)], axis=0).astype(BF16)
        zr = _dot(g_ref[j], y)
        out = _dot(zr.astype(BF16), w_ref[...]) + b_ref[...]
        rows = pl.ds(j, L2, stride=group)
        o0[rows, :] = out[:, :LANES]
        o1[rows, :] = out[:, LANES:]


def _fourier_mixer(fr, fi, row0, B, L, w_bd, b_f):
    L2 = CHUNK
    L1 = L // L2
    W = FNET_WIDTH
    f1, gc, gs = _fnet_tables_np(L)
    f1 = jnp.asarray(f1, BF16)
    g_cat = jnp.asarray(np.concatenate([gc, gs], axis=2), BF16)
    xr = fr.reshape(-1, L2, W)
    xi = fi.reshape(-1, L2, W)
    blk0 = row0 // L2 // L1
    grp1 = FNET_ROW_GROUP
    xblk = lambda h: pl.BlockSpec((L1, grp1, LANES), lambda b, j: (blk0 + b, j, h))
    yblk = pl.BlockSpec((L1, grp1, LANES), lambda b, j: (b, j, 0))
    halves = pl.pallas_call(
        functools.partial(_fnet_stage1_kernel, L1=L1, group=grp1),
        grid=(B, L2 // grp1),
        in_specs=[_full(f1.shape), xblk(0), xblk(1), xblk(0), xblk(1)],
        out_specs=[yblk] * 4,
        out_shape=[jax.ShapeDtypeStruct((B * L1, L2, LANES), F32)] * 4,
        compiler_params=_params(2),
        name="fnet_stage1",
    )(f1, xr, xr, xi, xi)
    grp2 = FNET_ROW_GROUP // 2
    n2 = L1 // grp2
    yblk2 = pl.BlockSpec((grp2, L2, LANES), lambda b, k: (b * n2 + k, 0, 0))
    gblk = pl.BlockSpec((grp2, L2, 2 * L2), lambda b, k: (k, 0, 0))
    oblk = pl.BlockSpec((L2, grp2, LANES), lambda b, k: (b, k, 0))
    out = pl.pallas_call(
        functools.partial(_fnet_stage2_kernel, group=grp2),
        grid=(B, n2),
        in_specs=[yblk2] * 4 + [gblk, _full(w_bd.shape), _full(b_f.shape)],
        out_specs=[oblk, oblk],
        out_shape=[jax.ShapeDtypeStruct((B * L2, L1, LANES), F32)] * 2,
        compiler_params=_params(2),
        name="fnet_stage2",
    )(*halves, g_cat, w_bd, b_f)
    return [o.reshape(B * L, LANES) for o in out]


def _gmlp_kernel(uv_ref, g_ref, b_ref, ws_ref, bs_ref, o_ref, *, n_chunks):
    uv = uv_ref[...].astype(F32)
    ge = 0.5 * uv * (1.0 + lax.erf(uv * (1.0 / math.sqrt(2.0))))
    u = ge[:, :GMLP_WIDTH]
    v = _ln_rows(ge[:, GMLP_WIDTH:], g_ref[...], b_ref[...])
    lane = lax.broadcasted_iota(jnp.int32, (CHUNK, GMLP_WIDTH), 1)
    ws = ws_ref[...]
    bs = bs_ref[...]
    for j in range(n_chunks):
        vj = v[j * CHUNK:(j + 1) * CHUNK]
        rhs = jnp.concatenate(
            [jnp.where(lane // GMLP_HDIM == h, vj, 0.0) for h in range(GMLP_HEADS)], axis=0)
        sv = _dot(ws, rhs.astype(BF16)) + bs
        o_ref[j * CHUNK:(j + 1) * CHUNK, :] = (u[j * CHUNK:(j + 1) * CHUNK] * sv).astype(BF16)


def _gmlp_mixer(uv, ln_g, ln_b, ws_cat, bs_wide):
    T = uv.shape[0]
    tm = TOK_TILE
    return pl.pallas_call(
        functools.partial(_gmlp_kernel, n_chunks=tm // CHUNK),
        grid=(T // tm,),
        in_specs=[pl.BlockSpec((tm, 2 * GMLP_WIDTH), lambda i: (i, 0)), _full(ln_g.shape),
                  _full(ln_b.shape), _full(ws_cat.shape), _full(bs_wide.shape)],
        out_specs=pl.BlockSpec((tm, GMLP_WIDTH), lambda i: (i, 0)),
        out_shape=jax.ShapeDtypeStruct((T, GMLP_WIDTH), BF16),
        compiler_params=_params(1),
        name="gmlp",
    )(uv, ln_g, ln_b, ws_cat, bs_wide)


def _outproj_kernel(x_ref, ssd_a_ref, ssd_b_ref, fn0_a_ref, fn1_a_ref, fn0_b_ref, fn1_b_ref,
                    gm_ref, wo_ref, bo_ref, g_ref, b_ref,
                    wrh_ref, wrl_ref, br_ref, tri_ref, x1_ref, route_ref, gate_ref, cnt_ref, seen_ref,
                    *, tiles_a):
    @pl.when(pl.program_id(0) == 0)
    def _():
        seen_ref[...] = jnp.zeros_like(seen_ref)

    in_a = pl.program_id(0) < tiles_a
    ssd = jnp.where(in_a, ssd_a_ref[...], ssd_b_ref[...])
    fn = jnp.concatenate([jnp.where(in_a, fn0_a_ref[...], fn0_b_ref[...]),
                          jnp.where(in_a, fn1_a_ref[...], fn1_b_ref[...])], axis=1)
    acc = _dot(ssd, wo_ref[0:SSD_WIDTH, :])
    acc = acc + _dot(fn.astype(BF16), wo_ref[SSD_WIDTH:SSD_WIDTH + FNET_WIDTH, :])
    acc = acc + _dot(gm_ref[...], wo_ref[SSD_WIDTH + FNET_WIDTH:, :])
    h = DN_ALPHA * x_ref[...] + acc + bo_ref[...]
    x1 = _ln_rows(h, g_ref[...], b_ref[...])
    x1_ref[...] = x1

    xh, xl = _split2(x1)
    wrh = wrh_ref[...]
    logits = _dot(xh, wrh) + _dot(xl, wrh) + _dot(xh, wrl_ref[...]) + br_ref[...]
    work = logits.T[:N_EXPERTS]
    n_tok = work.shape[1]
    row = lax.broadcasted_iota(jnp.int32, work.shape, 0)
    vals, idxs = [], []
    for _ in range(TOP_K):
        m = jnp.max(work, axis=0, keepdims=True)
        i = jnp.min(jnp.where(work == m, row, N_EXPERTS), axis=0, keepdims=True)
        vals.append(m)
        idxs.append(i)
        work = jnp.where(row == i, -jnp.inf, work)
    exps = [jnp.exp(v - vals[0]) for v in vals]
    denom = exps[0]
    for e in exps[1:]:
        denom = denom + e

    onehots = [row == i for i in idxs]
    chosen = jnp.zeros(work.shape, F32)
    for oh in onehots:
        chosen = jnp.where(oh, 1.0, chosen)
    seen = seen_ref[...]
    before = _dot(chosen.astype(BF16), tri_ref[...]) + seen[:, 0:1]
    seen = seen + jnp.sum(chosen, axis=1, keepdims=True)
    seen_ref[...] = seen
    cnt_ref[...] = seen

    row8 = lax.broadcasted_iota(jnp.int32, (2 * TOP_K, n_tok), 0)
    route = jnp.zeros((2 * TOP_K, n_tok), jnp.int32)
    gates_t = jnp.zeros((2 * TOP_K, n_tok), F32)
    for k in range(TOP_K):
        rank = jnp.sum(jnp.where(onehots[k], before, 0.0), axis=0, keepdims=True).astype(jnp.int32)
        route = jnp.where(row8 == k, idxs[k], route)
        route = jnp.where(row8 == TOP_K + k, rank, route)
        gates_t = jnp.where(row8 == k, exps[k] / denom, gates_t)
    route_ref[...] = route
    pad = jnp.zeros((LANES - 2 * TOP_K, n_tok), F32)
    gate_ref[...] = jnp.concatenate([gates_t, pad], axis=0).T


def _outproj(x, ssd_parts, fn_parts, gm, wo, bo, g, b, wrh, wrl, br):
    T, D = x.shape
    tm = TOK_TILE
    row = lambda w: pl.BlockSpec((tm, w), lambda i: (i, 0))
    na = ssd_parts[0].shape[0] // tm
    part_a = lambda w: pl.BlockSpec((tm, w), lambda i: (jnp.minimum(i, na - 1), 0))
    part_b = lambda w: pl.BlockSpec((tm, w), lambda i: (jnp.maximum(i - na, 0), 0))
    q = np.arange(tm)
    tri = jnp.asarray((q[:, None] < q[None, :]).astype(np.float32), BF16)
    parts = (*ssd_parts, *fn_parts[0], *fn_parts[1])
    return pl.pallas_call(
        functools.partial(_outproj_kernel, tiles_a=na),
        grid=(T // tm,),
        in_specs=[row(D), part_a(SSD_WIDTH), part_b(SSD_WIDTH),
                  part_a(LANES), part_a(LANES), part_b(LANES), part_b(LANES),
                  row(GMLP_WIDTH), _full(wo.shape),
                  _full(bo.shape), _full(g.shape), _full(b.shape), _full(wrh.shape),
                  _full(wrl.shape), _full(br.shape), _full(tri.shape)],
        out_specs=[row(D), pl.BlockSpec((2 * TOP_K, tm), lambda i: (0, i)), row(LANES),
                   _full((N_EXPERTS, LANES))],
        out_shape=[jax.ShapeDtypeStruct((T, D), F32),
                   jax.ShapeDtypeStruct((2 * TOP_K, T), jnp.int32),
                   jax.ShapeDtypeStruct((T, LANES), F32),
                   jax.ShapeDtypeStruct((N_EXPERTS, LANES), F32)],
        scratch_shapes=[pltpu.VMEM((N_EXPERTS, LANES), F32)],
        compiler_params=_params(0, 1),
        name="outproj_ln_router",
    )(x, *parts, gm, wo, bo, g, b, wrh, wrl, br, tri)


SUB = D_MODEL // LANES


def _to_slot_major(dst_ref, dst_row0, x, n):
    for c in range(SUB):
        dst_ref[pl.ds(dst_row0 + c, n, stride=SUB), :] = x[:, c * LANES:(c + 1) * LANES]


def _from_slot_major(src_ref, src_row0, n):
    return jnp.concatenate([src_ref[pl.ds(src_row0 + c, n, stride=SUB), :] for c in range(SUB)], axis=1)


def _slot_copy(src, dst, sem, src_slot, dst_slot):
    s0 = pl.multiple_of(src_slot * SUB, SUB)
    d0 = pl.multiple_of(dst_slot * SUB, SUB)
    return pltpu.make_async_copy(src.at[pl.ds(s0, SUB), :], dst.at[pl.ds(d0, SUB), :], sem)


def _slots_wait(src, dst, sem, n_slots):
    pltpu.make_async_copy(src.at[pl.ds(0, n_slots * SUB), :], dst.at[pl.ds(0, n_slots * SUB), :], sem).wait()


def _dispatch_kernel(pos_ref, x_ref, xbuf_hbm, xs_ref, sem, *, rows):
    _to_slot_major(xs_ref, 0, x_ref[...], rows)

    def body(j, carry):
        for u in range(ROW_UNROLL):
            r = j * ROW_UNROLL + u
            for k in range(TOP_K):
                _slot_copy(xs_ref, xbuf_hbm, sem, r, pos_ref[k * rows + r]).start(priority=k % 2)
        return carry

    lax.fori_loop(0, rows // ROW_UNROLL, body, 0)
    for _ in range(TOP_K):
        _slots_wait(xs_ref, xbuf_hbm, sem, rows)


def _dispatch(x, pos_flat, n_slots):
    T, D = x.shape
    R = DISPATCH_TILE
    return pl.pallas_call(
        functools.partial(_dispatch_kernel, rows=R),
        grid=(T // R,),
        in_specs=[pl.BlockSpec((R * TOP_K,), lambda i: (i,), memory_space=pltpu.SMEM),
                  pl.BlockSpec((R, D), lambda i: (i, 0))],
        out_specs=pl.BlockSpec(memory_space=pl.ANY),
        out_shape=jax.ShapeDtypeStruct((n_slots * SUB, LANES), F32),
        scratch_shapes=[pltpu.VMEM((R * SUB, LANES), F32), pltpu.SemaphoreType.DMA(())],
        compiler_params=_params(1),
        name="moe_dispatch",
    )(pos_flat, x)


def _expert_kernel(blk_e_ref, valid_ref, x_ref, wgu_ref, bgu_ref, wdn_ref, bdn_ref, y_ref, wgu_b, wdn_b):
    i = pl.program_id(0)
    valid = valid_ref[i]
    prev_e = blk_e_ref[jnp.maximum(i - 1, 0)]
    new_expert = jnp.logical_or(i == 0, blk_e_ref[i] != prev_e)

    @pl.when(jnp.logical_and(valid > 0, new_expert))
    def _():
        wgu_b[...] = wgu_ref[...].astype(BF16)
        wdn_b[...] = wdn_ref[...].astype(BF16)

    @pl.when(valid > 0)
    def _():
        x = _from_slot_major(x_ref, 0, EXPERT_TILE)
        row = lax.broadcasted_iota(jnp.int32, x.shape, 0)
        xb = jnp.where(row < valid, x, 0.0).astype(BF16)
        acc = jnp.broadcast_to(bdn_ref[...], x.shape)
        nc = EXPERT_COL_CHUNK
        for j in range(D_EXPERT // nc):
            g = _dot(xb, wgu_b[:, j * nc:(j + 1) * nc]) + bgu_ref[:, j * nc:(j + 1) * nc]
            u = (_dot(xb, wgu_b[:, D_EXPERT + j * nc:D_EXPERT + (j + 1) * nc])
                 + bgu_ref[:, D_EXPERT + j * nc:D_EXPERT + (j + 1) * nc])
            g = jnp.minimum(g, SWIGLU_LIMIT)
            u = jnp.clip(u, -SWIGLU_LIMIT, SWIGLU_LIMIT)
            act = (u + 1.0) * (g * jax.nn.sigmoid(SWIGLU_ALPHA * g))
            acc = acc + _dot(act.astype(BF16), wdn_b[j * nc:(j + 1) * nc, :])
        _to_slot_major(y_ref, 0, acc, EXPERT_TILE)

    @pl.when(valid <= 0)
    def _():
        y_ref[...] = jnp.zeros_like(y_ref)


def _expert_mlp(x_buf, blk_e, blk_valid, layer, wgu, bgu, wdn, bdn):
    D = D_MODEL
    tm = EXPERT_TILE
    per_expert = lambda r, c: pl.BlockSpec((None, None, r, c), lambda i, be, bv: (layer, be[i], 0, 0))
    slots = pl.BlockSpec((tm * SUB, LANES), lambda i, be, bv: (i, 0))
    grid_spec = pltpu.PrefetchScalarGridSpec(
        num_scalar_prefetch=2,
        grid=(x_buf.shape[0] // (tm * SUB),),
        in_specs=[slots, per_expert(D, 2 * D_EXPERT), per_expert(1, 2 * D_EXPERT),
                  per_expert(D_EXPERT, D), per_expert(1, D)],
        out_specs=slots,
        scratch_shapes=[pltpu.VMEM((D, 2 * D_EXPERT), BF16), pltpu.VMEM((D_EXPERT, D), BF16)],
    )
    return pl.pallas_call(
        _expert_kernel,
        grid_spec=grid_spec,
        out_shape=jax.ShapeDtypeStruct(x_buf.shape, F32),
        compiler_params=_params(0, 1),
        name="moe_expert_mlp",
    )(blk_e, blk_valid, x_buf, wgu, bgu, wdn, bdn)


def _combine_kernel(pos_ref, x_ref, gate_ref, y_hbm, g_ref, b_ref, o_ref, *scratch, rows, n_tiles):
    n_slots = COMBINE_LAG + 1
    bufs, sem = scratch[:n_slots], scratch[n_slots]
    i = pl.program_id(0)
    sub = COMBINE_SUB

    def run(s_issue, s_done):
        def issue(j):
            for u in range(sub):
                r = j * sub + u
                for k in range(TOP_K):
                    _slot_copy(y_hbm, bufs[s_issue].at[k], sem.at[s_issue], pos_ref[k * rows + r], r).start(
                        priority=k % 2)

        def finish(j):
            rs = pl.ds(pl.multiple_of(j * sub, sub), sub)
            row0 = pl.multiple_of(j * (sub * SUB), sub * SUB)
            gates = gate_ref[rs, :]
            y = gates[:, 0:1] * _from_slot_major(bufs[s_done].at[0], row0, sub)
            for k in range(1, TOP_K):
                y = y + gates[:, k:k + 1] * _from_slot_major(bufs[s_done].at[k], row0, sub)
            o_ref[rs, :] = _ln_rows(DN_ALPHA * x_ref[rs, :] + y, g_ref[...], b_ref[...])

        def wait_done():
            for k in range(TOP_K):
                _slots_wait(y_hbm, bufs[s_done].at[k], sem.at[s_done], rows)

        def loop(do_issue, do_finish):
            def body(j, carry):
                if do_issue:
                    issue(j)
                if do_finish:
                    finish(j)
                return carry

            lax.fori_loop(0, rows // sub, body, 0)

        @pl.when(i < COMBINE_LAG)
        def _():
            loop(True, False)

        @pl.when(jnp.logical_and(i >= COMBINE_LAG, i < n_tiles))
        def _():
            wait_done()
            loop(True, True)

        @pl.when(i >= n_tiles)
        def _():
            wait_done()
            loop(False, True)

    for s in range(n_slots):
        pl.when(i % n_slots == s)(functools.partial(run, s, (s + 1) % n_slots))


def _combine(x, gates, pos_flat, y_buf, g, b):
    T, D = x.shape
    R = COMBINE_TILE
    n = T // R
    lag = COMBINE_LAG
    done = lambda i: jnp.maximum(i - lag, 0)
    return pl.pallas_call(
        functools.partial(_combine_kernel, rows=R, n_tiles=n),
        grid=(n + lag,),
        in_specs=[pl.BlockSpec((R * TOP_K,), lambda i: (jnp.minimum(i, n - 1),), memory_space=pltpu.SMEM),
                  pl.BlockSpec((R, D), lambda i: (done(i), 0)),
                  pl.BlockSpec((R, LANES), lambda i: (done(i), 0)),
                  pl.BlockSpec(memory_space=pl.ANY),
                  _full(g.shape), _full(b.shape)],
        out_specs=pl.BlockSpec((R, D), lambda i: (done(i), 0)),
        out_shape=jax.ShapeDtypeStruct((T, D), F32),
        scratch_shapes=[pltpu.VMEM((TOP_K, R * SUB, LANES), F32) for _ in range(lag + 1)]
                       + [pltpu.SemaphoreType.DMA((lag + 1,))],
        compiler_params=_params(0, 1),
        name="moe_combine_ln",
    )(pos_flat, x, gates, y_buf, g, b)


def _tile_major(pos, tile):
    K, T = pos.shape
    return pos.reshape(K, T // tile, tile).transpose(1, 0, 2).reshape(-1)


def _routing_plan(route, counts, n_tokens):
    tm = EXPERT_TILE
    e = route[:TOP_K]
    rank = route[TOP_K:]
    counts = counts[:, 0].astype(jnp.int32)
    padded = (counts + tm - 1) // tm * tm
    pend = jnp.cumsum(padded)
    pstart = pend - padded
    experts = jnp.arange(N_EXPERTS, dtype=jnp.int32)[:, None, None]
    base = jnp.sum(jnp.where(e[None] == experts, pstart[:, None, None], 0), axis=0)
    pos = (base + rank).astype(jnp.int32)
    n_slots = n_tokens * TOP_K + N_EXPERTS * tm
    blk_start = jnp.arange(n_slots // tm, dtype=jnp.int32) * tm
    blk_e = jnp.minimum(jnp.sum(blk_start[:, None] >= pend[None, :], axis=1), N_EXPERTS - 1).astype(jnp.int32)
    blk_valid = jnp.clip(pstart[blk_e] + counts[blk_e] - blk_start, 0, tm).astype(jnp.int32)
    return pos, blk_e, blk_valid, n_slots


def _moe(x1, route, gates, counts, layer, wgu, bgu, wdn, bdn, ln_g, ln_b):
    T, D = x1.shape
    pos, blk_e, blk_valid, n_slots = _routing_plan(route, counts, T)
    x_buf = _dispatch(x1, _tile_major(pos, DISPATCH_TILE), n_slots)
    y_buf = _expert_mlp(x_buf, blk_e, blk_valid, layer, wgu, bgu[:, :, None, :], wdn, bdn[:, :, None, :])
    return _combine(x1, gates, _tile_major(pos, COMBINE_TILE), y_buf, ln_g.reshape(1, D), ln_b.reshape(1, D))


def _trunk(x_parts, trunks, p):
    x = _embed_ln(x_parts[0], x_parts[1], p["emb_ln_g"], p["emb_ln_b"])
    T, D = x.shape
    mc = jnp.asarray(_fnet_channel_np(), BF16)
    for l in range(DEPTH):
        w_in = p["w_in"][l]
        wa = w_in[:, :SSD_WIDTH + SSD_XBC].astype(BF16)
        wdt = jnp.pad(w_in[:, SSD_WIDTH + SSD_XBC:SSD_IN], ((0, 0), (0, LANES - 2 * SSD_HEADS))).astype(BF16)
        wf = w_in[:, SSD_IN:SSD_IN + FNET_WIDTH].astype(BF16)
        wuv = w_in[:, SSD_IN + FNET_WIDTH:].astype(BF16)
        z, xbc, dt, fr, fi, uv = _inproj(x, wa, wdt, wf, wuv, mc)

        lane_pad = LANES - 2 * SSD_HEADS
        dtb = jnp.pad(p["dt_bias"][l].reshape(1, -1), ((0, 0), (0, lane_pad)))
        a_neg = jnp.pad(-jnp.exp(p["a_log"][l].astype(F32)).reshape(1, -1), ((0, 0), (0, lane_pad)))
        dskip_w = jnp.repeat(p["d_skip"][l], SSD_HEAD_DIM).reshape(1, SSD_WIDTH)
        normg = p["ssd_norm_g"][l].reshape(1, SSD_WIDTH)
        convw = p["conv_w"][l]
        convb = p["conv_b"][l].reshape(1, SSD_XBC)
        w_bd = jax.scipy.linalg.block_diag(*[p["fnet_w"][l, g] for g in range(FNET_GROUPS)]).astype(BF16)
        b_f = p["fnet_b"][l].reshape(1, FNET_WIDTH)

        ssd_parts, fn_parts = [], []
        row0 = 0
        for (B, L) in trunks:
            ssd_parts.append(_ssd_mixer(z, xbc, dt, row0, B, L, convw, convb, dtb, a_neg, dskip_w, normg))
            fn_parts.append(_fourier_mixer(fr, fi, row0, B, L, w_bd, b_f))
            row0 += B * L

        ws_cat = jnp.concatenate([p["gmlp_ws"][l, h] for h in range(GMLP_HEADS)], axis=1).astype(BF16)
        bs_wide = jnp.repeat(p["gmlp_bs"][l].T, GMLP_HDIM, axis=1)
        gm = _gmlp_mixer(uv, p["gmlp_ln_g"][l].reshape(1, -1), p["gmlp_ln_b"][l].reshape(1, -1),
                         ws_cat, bs_wide)

        w_r = jnp.pad(p["router_w"][l], ((0, 0), (0, LANES - N_EXPERTS)))
        wrh = w_r.astype(BF16)
        wrl = (w_r - wrh.astype(F32)).astype(BF16)
        b_r = jnp.pad(p["router_b"][l].reshape(1, -1), ((0, 0), (0, LANES - N_EXPERTS)),
                      constant_values=NEG_BIG)
        x1, idx, gates, counts = _outproj(
            x, ssd_parts, fn_parts, gm, p["w_out"][l].astype(BF16), p["b_out"][l].reshape(1, D),
            p["ln1_g"][l].reshape(1, D), p["ln1_b"][l].reshape(1, D), wrh, wrl, b_r)

        x = _moe(x1, idx, gates, counts, l, p["exp_w_gu"], p["exp_b_gu"], p["exp_w_dn"], p["exp_b_dn"],
                 p["ln2_g"][l], p["ln2_b"][l])
    return x


def kernel(x_prompt, x_sample, emb_ln_g, emb_ln_b, w_in, conv_w, conv_b, dt_bias, a_log, d_skip, ssd_norm_g, fnet_w, fnet_b, gmlp_ln_g, gmlp_ln_b, gmlp_ws, gmlp_bs, w_out, b_out, ln1_g, ln1_b, router_w, router_b, exp_w_gu, exp_b_gu, exp_w_dn, exp_b_dn, ln2_g, ln2_b):
    p = dict(emb_ln_g=emb_ln_g, emb_ln_b=emb_ln_b, w_in=w_in, conv_w=conv_w, conv_b=conv_b,
             dt_bias=dt_bias, a_log=a_log, d_skip=d_skip, ssd_norm_g=ssd_norm_g, fnet_w=fnet_w,
             fnet_b=fnet_b, gmlp_ln_g=gmlp_ln_g, gmlp_ln_b=gmlp_ln_b, gmlp_ws=gmlp_ws, gmlp_bs=gmlp_bs,
             w_out=w_out, b_out=b_out, ln1_g=ln1_g, ln1_b=ln1_b, router_w=router_w, router_b=router_b,
             exp_w_gu=exp_w_gu, exp_b_gu=exp_b_gu, exp_w_dn=exp_w_dn, exp_b_dn=exp_b_dn,
             ln2_g=ln2_g, ln2_b=ln2_b)
    D = x_prompt.shape[-1]
    trunks = [x_prompt.shape[:2], x_sample.shape[:2]]
    y = _trunk([x_prompt.reshape(-1, D), x_sample.reshape(-1, D)], trunks, p)
    n_p = x_prompt.shape[0] * x_prompt.shape[1]
    return (y[:n_p].reshape(x_prompt.shape), y[n_p:].reshape(x_sample.shape))
```

```python
import functools
import math

import numpy as np
import jax
import jax.numpy as jnp
from jax import lax
from jax.experimental import pallas as pl
from jax.experimental.pallas import tpu as pltpu

F32 = jnp.float32
BF16 = jnp.bfloat16

D_MODEL = 1024
DEPTH = 4
SSD_WIDTH = 512
SSD_HEAD_DIM = 64
SSD_HEADS = 8
SSD_GROUPS = 2
SSD_STATE = 64
SSD_CONV = 5
SSD_GN = SSD_GROUPS * SSD_STATE
SSD_XBC = SSD_WIDTH + 2 * SSD_GN
SSD_IN = SSD_WIDTH + SSD_XBC + 2 * SSD_HEADS
FNET_WIDTH = 256
FNET_GROUPS = 4
FNET_GDIM = 64
GMLP_WIDTH = 256
GMLP_HEADS = 4
GMLP_HDIM = 64
N_EXPERTS = 32
TOP_K = 4
D_EXPERT = D_MODEL
SWIGLU_LIMIT = 7.0
SWIGLU_ALPHA = 1.702
LN_EPS = 1e-5
RMS_EPS = 1e-5
DN_ALPHA = (2 * DEPTH) ** 0.25

LANES = 128
CHUNK = 128
HALO = 16
VMEM_LIMIT = 56 * 1024 * 1024

TOK_TILE = 512
EXPERT_TILE = 512
EXPERT_COL_CHUNK = 512
DISPATCH_TILE = 512
COMBINE_TILE = 256
COMBINE_SUB = 64
COMBINE_LAG = 2
ROW_UNROLL = 8
SSD_STEP_CHUNKS = 8
FNET_ROW_GROUP = 32
NEG_BIG = -1e30


def _dot(a, b):
    return jnp.dot(a, b, preferred_element_type=F32)


def _split2(v):
    hi = v.astype(BF16)
    lo = (v - hi.astype(F32)).astype(BF16)
    return hi, lo


def _split3(v):
    hi = v.astype(BF16)
    r = v - hi.astype(F32)
    mid = r.astype(BF16)
    lo = (r - mid.astype(F32)).astype(BF16)
    return hi, mid, lo


def _ln_rows(xf, g, b):
    mu = jnp.mean(xf, -1, keepdims=True)
    xc = xf - mu
    var = jnp.mean(xc * xc, -1, keepdims=True)
    return xc * lax.rsqrt(var + LN_EPS) * g + b


def _params(n_parallel=1, n_arbitrary=0):
    sem = ("parallel",) * n_parallel + ("arbitrary",) * n_arbitrary
    return pltpu.CompilerParams(dimension_semantics=sem, vmem_limit_bytes=VMEM_LIMIT)


def _full(shape):
    nd = len(shape)
    return pl.BlockSpec(shape, lambda *_: (0,) * nd)


def _embed_ln_kernel(xa_ref, xb_ref, g_ref, b_ref, o_ref, *, tiles_a):
    i = pl.program_id(0)

    @pl.when(i < tiles_a)
    def _():
        o_ref[...] = _ln_rows(xa_ref[...], g_ref[...], b_ref[...])

    @pl.when(i >= tiles_a)
    def _():
        o_ref[...] = _ln_rows(xb_ref[...], g_ref[...], b_ref[...])


def _embed_ln(xa, xb, g, b):
    D = xa.shape[1]
    na, nb = xa.shape[0] // TOK_TILE, xb.shape[0] // TOK_TILE
    return pl.pallas_call(
        functools.partial(_embed_ln_kernel, tiles_a=na),
        grid=(na + nb,),
        in_specs=[pl.BlockSpec((TOK_TILE, D), lambda i: (jnp.minimum(i, na - 1), 0)),
                  pl.BlockSpec((TOK_TILE, D), lambda i: (jnp.maximum(i - na, 0), 0)),
                  _full((1, D)), _full((1, D))],
        out_specs=pl.BlockSpec((TOK_TILE, D), lambda i: (i, 0)),
        out_shape=jax.ShapeDtypeStruct(((na + nb) * TOK_TILE, D), F32),
        compiler_params=_params(1),
        name="embed_ln",
    )(xa, xb, g.reshape(1, D), b.reshape(1, D))


def _inproj_kernel(x_ref, wa_ref, wdt_ref, wf_ref, wuv_ref, mc_ref,
                   z_ref, xbc_ref, dt_ref, fr_ref, fi_ref, uv_ref):
    xb = x_ref[...].astype(BF16)
    a = _dot(xb, wa_ref[...])
    z_ref[...] = a[:, :SSD_WIDTH].astype(BF16)
    xbc_ref[...] = a[:, SSD_WIDTH:].astype(BF16)
    dt_ref[...] = _dot(xb, wdt_ref[...])
    f = _dot(xb, wf_ref[...]).astype(BF16)
    fri = _dot(f, mc_ref[...])
    fr_ref[...] = fri[:, :FNET_WIDTH]
    fi_ref[...] = fri[:, FNET_WIDTH:]
    uv_ref[...] = _dot(xb, wuv_ref[...]).astype(BF16)


def _inproj(x, wa, wdt, wf, wuv, mc):
    T, D = x.shape
    tm = TOK_TILE
    row = lambda w: pl.BlockSpec((tm, w), lambda i: (i, 0))
    return pl.pallas_call(
        _inproj_kernel,
        grid=(T // tm,),
        in_specs=[row(D), _full(wa.shape), _full(wdt.shape), _full(wf.shape), _full(wuv.shape),
                  _full(mc.shape)],
        out_specs=[row(SSD_WIDTH), row(SSD_XBC), row(LANES), row(FNET_WIDTH), row(FNET_WIDTH),
                   row(2 * GMLP_WIDTH)],
        out_shape=[jax.ShapeDtypeStruct((T, SSD_WIDTH), BF16),
                   jax.ShapeDtypeStruct((T, SSD_XBC), BF16),
                   jax.ShapeDtypeStruct((T, LANES), F32),
                   jax.ShapeDtypeStruct((T, FNET_WIDTH), F32),
                   jax.ShapeDtypeStruct((T, FNET_WIDTH), F32),
                   jax.ShapeDtypeStruct((T, 2 * GMLP_WIDTH), BF16)],
        compiler_params=_params(1),
        name="inproj",
    )(x, wa, wdt, wf, wuv, mc)


def _conv_silu(xc_ref, xp_ref, xn_ref, convw_ref, convb_ref, first, last):
    Q = xc_ref.shape[0]
    cur = xc_ref[...].astype(F32)
    prev = jnp.where(first, 0.0, xp_ref[...].astype(F32))
    nxt = jnp.where(last, 0.0, xn_ref[...].astype(F32))
    ext = jnp.concatenate([prev[HALO - 8:], cur, nxt[:8]], axis=0)
    n_ext = Q + 16
    pad = SSD_CONV // 2
    conv = convb_ref[...]
    for k in range(SSD_CONV):
        shift = (pad - k) % n_ext
        tap = ext if shift == 0 else pltpu.roll(ext, shift, axis=0)
        conv = conv + tap[8:8 + Q] * convw_ref[k:k + 1, :]
    return conv * jax.nn.sigmoid(conv)


def _ssd_chunk(act, dt_raw, dtb_ref, a_ref, cum_ref, expand_ref, state_ref, *, backward):
    Q = CHUNK
    h_off = SSD_HEADS if backward else 0
    xs = act[:, :SSD_WIDTH]
    bm = act[:, SSD_WIDTH:SSD_WIDTH + SSD_GN]
    cm = act[:, SSD_WIDTH + SSD_GN:]

    dt = jax.nn.softplus(dt_raw + dtb_ref[...])
    adt = dt * a_ref[...]
    cum = cum_ref[...]
    h3 = _split3(adt)
    acum = _dot(cum, h3[0]) + _dot(cum, h3[1]) + _dot(cum, h3[2])
    acum_t = acum.T
    total = acum[0:1, :] if backward else acum[Q - 1:Q, :]
    dec_in = jnp.exp(acum)
    dec_end = jnp.exp(total - acum)

    expand = expand_ref[...]

    def widen(v):
        hi, lo = _split2(v)
        return _dot(hi, expand) + _dot(lo, expand)

    dt_w = widen(dt)
    dec_in_w = widen(dec_in)
    dec_end_w = widen(dec_end)
    tdec_w = dec_in_w[0:1, :] if backward else dec_in_w[Q - 1:Q, :]

    xdt = xs * dt_w

    row = lax.broadcasted_iota(jnp.int32, (Q, Q), 0)
    col = lax.broadcasted_iota(jnp.int32, (Q, Q), 1)
    mask = (col > row) if backward else (col <= row)
    lane = lax.broadcasted_iota(jnp.int32, (Q, LANES), 1)
    lo_half = lane < SSD_STATE

    bm_b = bm.astype(BF16)
    cb = []
    for g in range(SSD_GROUPS):
        cm_g = jnp.where(lo_half if g == 0 else jnp.logical_not(lo_half), cm, 0.0).astype(BF16)
        cb.append(lax.dot_general(cm_g, bm_b, (((1,), (1,)), ((), ())),
                                  preferred_element_type=F32))

    y_parts = []
    for pair in range(SSD_HEADS // 2):
        gmats = []
        for r in range(2):
            h = 2 * pair + r
            hl = h_off + h
            diff = acum[:, hl:hl + 1] - acum_t[hl:hl + 1, :]
            dec = jnp.exp(jnp.where(mask, diff, NEG_BIG))
            gmats.append((cb[h // (SSD_HEADS // SSD_GROUPS)] * dec).astype(BF16))
        lhs = jnp.concatenate(gmats, axis=1)
        x2 = xdt[:, pair * LANES:(pair + 1) * LANES]
        rhs = jnp.concatenate([jnp.where(lo_half, x2, 0.0), jnp.where(lo_half, 0.0, x2)],
                              axis=0).astype(BF16)
        y_parts.append(_dot(lhs, rhs))
    y_diag = jnp.concatenate(y_parts, axis=1)

    srow = lax.broadcasted_iota(jnp.int32, (SSD_GN, SSD_WIDTH), 0)
    scol = lax.broadcasted_iota(jnp.int32, (SSD_GN, SSD_WIDTH), 1)
    on_group = (srow // SSD_STATE) == (scol // (SSD_WIDTH // SSD_GROUPS))
    st_new = _dot(bm.T.astype(BF16), (xdt * dec_end_w).astype(BF16))
    st_new = jnp.where(on_group, st_new, 0.0)
    st_prev = state_ref[...]
    y_off = _dot(cm.astype(BF16), st_prev.astype(BF16)) * dec_in_w
    state_ref[...] = st_prev * tdec_w + st_new
    return y_diag + y_off, xs


def _ssd_bwd_kernel(xc_ref, xp_ref, xn_ref, dt_ref, convw_ref, convb_ref, dtb_ref, a_ref,
                    cum_ref, expand_ref, yb_ref, act_ref, state_ref, *, n_steps):
    c = pl.program_id(1)
    step = n_steps - 1 - c

    @pl.when(c == 0)
    def _():
        state_ref[...] = jnp.zeros_like(state_ref)

    act = _conv_silu(xc_ref, xp_ref, xn_ref, convw_ref, convb_ref,
                     first=step == 0, last=step == n_steps - 1)
    act_ref[...] = act.astype(BF16)
    for s in reversed(range(SSD_STEP_CHUNKS)):
        rows = slice(s * CHUNK, (s + 1) * CHUNK)
        y, _ = _ssd_chunk(act[rows], dt_ref[rows, :], dtb_ref, a_ref, cum_ref, expand_ref, state_ref,
                          backward=True)
        yb_ref[rows, :] = y


def _ssd_fwd_kernel(act_ref, dt_ref, z_ref, yb_ref, dtb_ref, a_ref, cum_ref, expand_ref,
                    dskip_ref, normg_ref, o_ref, state_ref):
    c = pl.program_id(1)

    @pl.when(c == 0)
    def _():
        state_ref[...] = jnp.zeros_like(state_ref)

    gw = SSD_WIDTH // SSD_GROUPS
    for s in range(SSD_STEP_CHUNKS):
        rows = slice(s * CHUNK, (s + 1) * CHUNK)
        y, xs = _ssd_chunk(act_ref[rows, :].astype(F32), dt_ref[rows, :], dtb_ref, a_ref, cum_ref,
                           expand_ref, state_ref, backward=False)
        z = z_ref[rows, :].astype(F32)
        y = (y + yb_ref[rows, :] + xs * dskip_ref[...]) * (z * jax.nn.sigmoid(z))
        outs = []
        for g in range(SSD_GROUPS):
            yg = y[:, g * gw:(g + 1) * gw]
            outs.append(yg * lax.rsqrt(jnp.mean(yg * yg, -1, keepdims=True) + RMS_EPS))
        o_ref[rows, :] = (jnp.concatenate(outs, axis=1) * normg_ref[...]).astype(BF16)


def _ssd_mixer(z, xbc, dt, row0, B, L, convw, convb, dtb, a_neg, dskip_w, normg):
    R = SSD_STEP_CHUNKS * CHUNK
    C = L // R
    c0 = row0 // R
    hpc = R // HALO
    n_halo = z.shape[0] // HALO
    consts = _ssd_consts()

    def specs(chunk_of):
        cur = lambda w: pl.BlockSpec((R, w), lambda b, c: (c0 + b * C + chunk_of(c), 0))
        prv = pl.BlockSpec((HALO, SSD_XBC),
                           lambda b, c: (jnp.maximum((c0 + b * C + chunk_of(c)) * hpc - 1, 0), 0))
        nxt = pl.BlockSpec((HALO, SSD_XBC),
                           lambda b, c: (jnp.minimum((c0 + b * C + chunk_of(c) + 1) * hpc, n_halo - 1), 0))
        return cur, prv, nxt

    out_row = lambda chunk_of, w: pl.BlockSpec((R, w), lambda b, c: (b * C + chunk_of(c), 0))

    rev = lambda c: C - 1 - c
    cur, prv, nxt = specs(rev)
    yb, act = pl.pallas_call(
        functools.partial(_ssd_bwd_kernel, n_steps=C),
        grid=(B, C),
        in_specs=[cur(SSD_XBC), prv, nxt, cur(LANES), _full(convw.shape), _full(convb.shape),
                  _full(dtb.shape), _full(a_neg.shape),
                  _full(consts["cum_b"].shape), _full(consts["expand_b"].shape)],
        out_specs=[out_row(rev, SSD_WIDTH), out_row(rev, SSD_XBC)],
        out_shape=[jax.ShapeDtypeStruct((B * L, SSD_WIDTH), F32),
                   jax.ShapeDtypeStruct((B * L, SSD_XBC), BF16)],
        scratch_shapes=[pltpu.VMEM((SSD_GN, SSD_WIDTH), F32)],
        compiler_params=_params(1, 1),
        name="ssd_bwd",
    )(xbc, xbc, xbc, dt, convw, convb, dtb, a_neg, consts["cum_b"], consts["expand_b"])

    fwd = lambda c: c
    cur, _, _ = specs(fwd)
    return pl.pallas_call(
        _ssd_fwd_kernel,
        grid=(B, C),
        in_specs=[out_row(fwd, SSD_XBC), cur(LANES), cur(SSD_WIDTH), out_row(fwd, SSD_WIDTH),
                  _full(dtb.shape), _full(a_neg.shape),
                  _full(consts["cum_f"].shape), _full(consts["expand_f"].shape),
                  _full(dskip_w.shape), _full(normg.shape)],
        out_specs=out_row(fwd, SSD_WIDTH),
        out_shape=jax.ShapeDtypeStruct((B * L, SSD_WIDTH), BF16),
        scratch_shapes=[pltpu.VMEM((SSD_GN, SSD_WIDTH), F32)],
        compiler_params=_params(1, 1),
        name="ssd_fwd",
    )(act, dt, z, yb, dtb, a_neg, consts["cum_f"], consts["expand_f"], dskip_w, normg)


@functools.lru_cache(maxsize=None)
def _ssd_consts_np():
    q = np.arange(CHUNK)
    cum_f = (q[None, :] <= q[:, None]).astype(np.float32)
    cum_b = (q[None, :] >= q[:, None]).astype(np.float32)
    out = {"cum_f": cum_f, "cum_b": cum_b}
    for name, off in (("expand_f", 0), ("expand_b", SSD_HEADS)):
        e = np.zeros((LANES, SSD_WIDTH), np.float32)
        for h in range(SSD_HEADS):
            e[off + h, h * SSD_HEAD_DIM:(h + 1) * SSD_HEAD_DIM] = 1.0
        out[name] = e
    return out


def _ssd_consts():
    return {k: jnp.asarray(v, BF16) for k, v in _ssd_consts_np().items()}


@functools.lru_cache(maxsize=None)
def _fnet_tables_np(L):
    L2 = CHUNK
    L1 = L // L2
    k1 = np.arange(L1)
    ang1 = 2.0 * np.pi * ((k1[:, None] * k1[None, :]) % L1) / L1
    f1 = np.concatenate([np.cos(ang1), np.sin(ang1)], axis=0) / math.sqrt(L1)
    k = k1[:, None, None] + L1 * np.arange(L2)[None, :, None]
    l2 = np.arange(L2)[None, None, :]
    ang2 = 2.0 * np.pi * ((k * l2) % L) / L
    gc = np.cos(ang2) / math.sqrt(L2)
    gs = np.sin(ang2) / math.sqrt(L2)
    return f1.astype(np.float32), gc.astype(np.float32), gs.astype(np.float32)


@functools.lru_cache(maxsize=None)
def _fnet_channel_np():
    c = np.arange(FNET_GDIM)
    ang = 2.0 * np.pi * ((c[:, None] * c[None, :]) % FNET_GDIM) / FNET_GDIM
    cc = np.cos(ang) / math.sqrt(FNET_GDIM)
    sc = np.sin(ang) / math.sqrt(FNET_GDIM)
    eye = np.eye(FNET_GROUPS)
    return np.concatenate([np.kron(eye, cc), np.kron(eye, -sc)], axis=1).astype(np.float32)


def _fnet_stage1_kernel(f_ref, xr0_ref, xr1_ref, xi0_ref, xi1_ref, yr0_ref, yr1_ref, yi0_ref, yi1_ref,
                        *, L1, group):
    f = f_ref[...]
    flat = lambda ref: ref.reshape(L1 * group, LANES)
    xs = [flat(r) for r in (xr0_ref, xr1_ref, xi0_ref, xi1_ref)]
    ys = [flat(r) for r in (yr0_ref, yr1_ref, yi0_ref, yi1_ref)]
    W = FNET_WIDTH
    for j in range(group):
        rows = pl.ds(j, L1, stride=group)
        x = jnp.concatenate([r[rows, :] for r in xs], axis=1).astype(BF16)
        p = _dot(f, x)
        yr = p[:L1, :W] + p[L1:, W:]
        yi = p[:L1, W:] - p[L1:, :W]
        ys[0][rows, :] = yr[:, :LANES]
        ys[1][rows, :] = yr[:, LANES:]
        ys[2][rows, :] = yi[:, :LANES]
        ys[3][rows, :] = yi[:, LANES:]


def _fnet_stage2_kernel(yr0_ref, yr1_ref, yi0_ref, yi1_ref, g_ref, w_ref, b_ref, o0_ref, o1_ref, *, group):
    L2 = o0_ref.shape[0]
    o0 = o0_ref.reshape(L2 * group, LANES)
    o1 = o1_ref.reshape(L2 * group, LANES)
    for j in range(group):
        y = jnp.concatenate([jnp.concatenate([yr0_ref[j], yr1_ref[j]], axis=1),
                             jnp.concatenate([yi0_ref[j], yi1_ref[j]], axis=1)], axis=0).astype(BF16)
        zr = _dot(g_ref[j], y)
        out = _dot(zr.astype(BF16), w_ref[...]) + b_ref[...]
        rows = pl.ds(j, L2, stride=group)
        o0[rows, :] = out[:, :LANES]
        o1[rows, :] = out[:, LANES:]


def _fourier_mixer(fr, fi, row0, B, L, w_bd, b_f):
    L2 = CHUNK
    L1 = L // L2
    W = FNET_WIDTH
    f1, gc, gs = _fnet_tables_np(L)
    f1 = jnp.asarray(f1, BF16)
    g_cat = jnp.asarray(np.concatenate([gc, gs], axis=2), BF16)
    xr = fr.reshape(-1, L2, W)
    xi = fi.reshape(-1, L2, W)
    blk0 = row0 // L2 // L1
    grp1 = FNET_ROW_GROUP
    xblk = lambda h: pl.BlockSpec((L1, grp1, LANES), lambda b, j: (blk0 + b, j, h))
    yblk = pl.BlockSpec((L1, grp1, LANES), lambda b, j: (b, j, 0))
    halves = pl.pallas_call(
        functools.partial(_fnet_stage1_kernel, L1=L1, group=grp1),
        grid=(B, L2 // grp1),
        in_specs=[_full(f1.shape), xblk(0), xblk(1), xblk(0), xblk(1)],
        out_specs=[yblk] * 4,
        out_shape=[jax.ShapeDtypeStruct((B * L1, L2, LANES), F32)] * 4,
        compiler_params=_params(2),
        name="fnet_stage1",
    )(f1, xr, xr, xi, xi)
    grp2 = FNET_ROW_GROUP // 2
    n2 = L1 // grp2
    yblk2 = pl.BlockSpec((grp2, L2, LANES), lambda b, k: (b * n2 + k, 0, 0))
    gblk = pl.BlockSpec((grp2, L2, 2 * L2), lambda b, k: (k, 0, 0))
    oblk = pl.BlockSpec((L2, grp2, LANES), lambda b, k: (b, k, 0))
    out = pl.pallas_call(
        functools.partial(_fnet_stage2_kernel, group=grp2),
        grid=(B, n2),
        in_specs=[yblk2] * 4 + [gblk, _full(w_bd.shape), _full(b_f.shape)],
        out_specs=[oblk, oblk],
        out_shape=[jax.ShapeDtypeStruct((B * L2, L1, LANES), F32)] * 2,
        compiler_params=_params(2),
        name="fnet_stage2",
    )(*halves, g_cat, w_bd, b_f)
    return [o.reshape(B * L, LANES) for o in out]


def _gmlp_kernel(uv_ref, g_ref, b_ref, ws_ref, bs_ref, o_ref, *, n_chunks):
    uv = uv_ref[...].astype(F32)
    ge = 0.5 * uv * (1.0 + lax.erf(uv * (1.0 / math.sqrt(2.0))))
    u = ge[:, :GMLP_WIDTH]
    v = _ln_rows(ge[:, GMLP_WIDTH:], g_ref[...], b_ref[...])
    lane = lax.broadcasted_iota(jnp.int32, (CHUNK, GMLP_WIDTH), 1)
    ws = ws_ref[...]
    bs = bs_ref[...]
    for j in range(n_chunks):
        vj = v[j * CHUNK:(j + 1) * CHUNK]
        rhs = jnp.concatenate(
            [jnp.where(lane // GMLP_HDIM == h, vj, 0.0) for h in range(GMLP_HEADS)], axis=0)
        sv = _dot(ws, rhs.astype(BF16)) + bs
        o_ref[j * CHUNK:(j + 1) * CHUNK, :] = (u[j * CHUNK:(j + 1) * CHUNK] * sv).astype(BF16)


def _gmlp_mixer(uv, ln_g, ln_b, ws_cat, bs_wide):
    T = uv.shape[0]
    tm = TOK_TILE
    return pl.pallas_call(
        functools.partial(_gmlp_kernel, n_chunks=tm // CHUNK),
        grid=(T // tm,),
        in_specs=[pl.BlockSpec((tm, 2 * GMLP_WIDTH), lambda i: (i, 0)), _full(ln_g.shape),
                  _full(ln_b.shape), _full(ws_cat.shape), _full(bs_wide.shape)],
        out_specs=pl.BlockSpec((tm, GMLP_WIDTH), lambda i: (i, 0)),
        out_shape=jax.ShapeDtypeStruct((T, GMLP_WIDTH), BF16),
        compiler_params=_params(1),
        name="gmlp",
    )(uv, ln_g, ln_b, ws_cat, bs_wide)


def _outproj_kernel(x_ref, ssd_a_ref, ssd_b_ref, fn0_a_ref, fn1_a_ref, fn0_b_ref, fn1_b_ref,
                    gm_ref, wo_ref, bo_ref, g_ref, b_ref,
                    wrh_ref, wrl_ref, br_ref, tri_ref, x1_ref, route_ref, gate_ref, cnt_ref, seen_ref,
                    *, tiles_a):
    @pl.when(pl.program_id(0) == 0)
    def _():
        seen_ref[...] = jnp.zeros_like(seen_ref)

    in_a = pl.program_id(0) < tiles_a
    ssd = jnp.where(in_a, ssd_a_ref[...], ssd_b_ref[...])
    fn = jnp.concatenate([jnp.where(in_a, fn0_a_ref[...], fn0_b_ref[...]),
                          jnp.where(in_a, fn1_a_ref[...], fn1_b_ref[...])], axis=1)
    acc = _dot(ssd, wo_ref[0:SSD_WIDTH, :])
    acc = acc + _dot(fn.astype(BF16), wo_ref[SSD_WIDTH:SSD_WIDTH + FNET_WIDTH, :])
    acc = acc + _dot(gm_ref[...], wo_ref[SSD_WIDTH + FNET_WIDTH:, :])
    h = DN_ALPHA * x_ref[...] + acc + bo_ref[...]
    x1 = _ln_rows(h, g_ref[...], b_ref[...])
    x1_ref[...] = x1

    xh, xl = _split2(x1)
    wrh = wrh_ref[...]
    logits = _dot(xh, wrh) + _dot(xl, wrh) + _dot(xh, wrl_ref[...]) + br_ref[...]
    work = logits.T[:N_EXPERTS]
    n_tok = work.shape[1]
    row = lax.broadcasted_iota(jnp.int32, work.shape, 0)
    vals, idxs = [], []
    for _ in range(TOP_K):
        m = jnp.max(work, axis=0, keepdims=True)
        i = jnp.min(jnp.where(work == m, row, N_EXPERTS), axis=0, keepdims=True)
        vals.append(m)
        idxs.append(i)
        work = jnp.where(row == i, -jnp.inf, work)
    exps = [jnp.exp(v - vals[0]) for v in vals]
    denom = exps[0]
    for e in exps[1:]:
        denom = denom + e

    onehots = [row == i for i in idxs]
    chosen = jnp.zeros(work.shape, F32)
    for oh in onehots:
        chosen = jnp.where(oh, 1.0, chosen)
    seen = seen_ref[...]
    before = _dot(chosen.astype(BF16), tri_ref[...]) + seen[:, 0:1]
    seen = seen + jnp.sum(chosen, axis=1, keepdims=True)
    seen_ref[...] = seen
    cnt_ref[...] = seen

    row8 = lax.broadcasted_iota(jnp.int32, (2 * TOP_K, n_tok), 0)
    route = jnp.zeros((2 * TOP_K, n_tok), jnp.int32)
    gates_t = jnp.zeros((2 * TOP_K, n_tok), F32)
    for k in range(TOP_K):
        rank = jnp.sum(jnp.where(onehots[k], before, 0.0), axis=0, keepdims=True).astype(jnp.int32)
        route = jnp.where(row8 == k, idxs[k], route)
        route = jnp.where(row8 == TOP_K + k, rank, route)
        gates_t = jnp.where(row8 == k, exps[k] / denom, gates_t)
    route_ref[...] = route
    pad = jnp.zeros((LANES - 2 * TOP_K, n_tok), F32)
    gate_ref[...] = jnp.concatenate([gates_t, pad], axis=0).T


def _outproj(x, ssd_parts, fn_parts, gm, wo, bo, g, b, wrh, wrl, br):
    T, D = x.shape
    tm = TOK_TILE
    row = lambda w: pl.BlockSpec((tm, w), lambda i: (i, 0))
    na = ssd_parts[0].shape[0] // tm
    part_a = lambda w: pl.BlockSpec((tm, w), lambda i: (jnp.minimum(i, na - 1), 0))
    part_b = lambda w: pl.BlockSpec((tm, w), lambda i: (jnp.maximum(i - na, 0), 0))
    q = np.arange(tm)
    tri = jnp.asarray((q[:, None] < q[None, :]).astype(np.float32), BF16)
    parts = (*ssd_parts, *fn_parts[0], *fn_parts[1])
    return pl.pallas_call(
        functools.partial(_outproj_kernel, tiles_a=na),
        grid=(T // tm,),
        in_specs=[row(D), part_a(SSD_WIDTH), part_b(SSD_WIDTH),
                  part_a(LANES), part_a(LANES), part_b(LANES), part_b(LANES),
                  row(GMLP_WIDTH), _full(wo.shape),
                  _full(bo.shape), _full(g.shape), _full(b.shape), _full(wrh.shape),
                  _full(wrl.shape), _full(br.shape), _full(tri.shape)],
        out_specs=[row(D), pl.BlockSpec((2 * TOP_K, tm), lambda i: (0, i)), row(LANES),
                   _full((N_EXPERTS, LANES))],
        out_shape=[jax.ShapeDtypeStruct((T, D), F32),
                   jax.ShapeDtypeStruct((2 * TOP_K, T), jnp.int32),
                   jax.ShapeDtypeStruct((T, LANES), F32),
                   jax.ShapeDtypeStruct((N_EXPERTS, LANES), F32)],
        scratch_shapes=[pltpu.VMEM((N_EXPERTS, LANES), F32)],
        compiler_params=_params(0, 1),
        name="outproj_ln_router",
    )(x, *parts, gm, wo, bo, g, b, wrh, wrl, br, tri)


SUB = D_MODEL // LANES


def _to_slot_major(dst_ref, dst_row0, x, n):
    for c in range(SUB):
        dst_ref[pl.ds(dst_row0 + c, n, stride=SUB), :] = x[:, c * LANES:(c + 1) * LANES]


def _from_slot_major(src_ref, src_row0, n):
    return jnp.concatenate([src_ref[pl.ds(src_row0 + c, n, stride=SUB), :] for c in range(SUB)], axis=1)


def _slot_copy(src, dst, sem, src_slot, dst_slot):
    s0 = pl.multiple_of(src_slot * SUB, SUB)
    d0 = pl.multiple_of(dst_slot * SUB, SUB)
    return pltpu.make_async_copy(src.at[pl.ds(s0, SUB), :], dst.at[pl.ds(d0, SUB), :], sem)


def _slots_wait(src, dst, sem, n_slots):
    pltpu.make_async_copy(src.at[pl.ds(0, n_slots * SUB), :], dst.at[pl.ds(0, n_slots * SUB), :], sem).wait()


def _dispatch_kernel(pos_ref, x_ref, xbuf_hbm, xs_ref, sem, *, rows):
    _to_slot_major(xs_ref, 0, x_ref[...], rows)

    def body(j, carry):
        for u in range(ROW_UNROLL):
            r = j * ROW_UNROLL + u
            for k in range(TOP_K):
                _slot_copy(xs_ref, xbuf_hbm, sem, r, pos_ref[k * rows + r]).start(priority=k % 2)
        return carry

    lax.fori_loop(0, rows // ROW_UNROLL, body, 0)
    for _ in range(TOP_K):
        _slots_wait(xs_ref, xbuf_hbm, sem, rows)


def _dispatch(x, pos_flat, n_slots):
    T, D = x.shape
    R = DISPATCH_TILE
    return pl.pallas_call(
        functools.partial(_dispatch_kernel, rows=R),
        grid=(T // R,),
        in_specs=[pl.BlockSpec((R * TOP_K,), lambda i: (i,), memory_space=pltpu.SMEM),
                  pl.BlockSpec((R, D), lambda i: (i, 0))],
        out_specs=pl.BlockSpec(memory_space=pl.ANY),
        out_shape=jax.ShapeDtypeStruct((n_slots * SUB, LANES), F32),
        scratch_shapes=[pltpu.VMEM((R * SUB, LANES), F32), pltpu.SemaphoreType.DMA(())],
        compiler_params=_params(1),
        name="moe_dispatch",
    )(pos_flat, x)


def _expert_kernel(blk_e_ref, valid_ref, x_ref, wgu_ref, bgu_ref, wdn_ref, bdn_ref, y_ref, wgu_b, wdn_b):
    i = pl.program_id(0)
    valid = valid_ref[i]
    prev_e = blk_e_ref[jnp.maximum(i - 1, 0)]
    new_expert = jnp.logical_or(i == 0, blk_e_ref[i] != prev_e)

    @pl.when(jnp.logical_and(valid > 0, new_expert))
    def _():
        wgu_b[...] = wgu_ref[...].astype(BF16)
        wdn_b[...] = wdn_ref[...].astype(BF16)

    @pl.when(valid > 0)
    def _():
        x = _from_slot_major(x_ref, 0, EXPERT_TILE)
        row = lax.broadcasted_iota(jnp.int32, x.shape, 0)
        xb = jnp.where(row < valid, x, 0.0).astype(BF16)
        acc = jnp.broadcast_to(bdn_ref[...], x.shape)
        nc = EXPERT_COL_CHUNK
        for j in range(D_EXPERT // nc):
            g = _dot(xb, wgu_b[:, j * nc:(j + 1) * nc]) + bgu_ref[:, j * nc:(j + 1) * nc]
            u = (_dot(xb, wgu_b[:, D_EXPERT + j * nc:D_EXPERT + (j + 1) * nc])
                 + bgu_ref[:, D_EXPERT + j * nc:D_EXPERT + (j + 1) * nc])
            g = jnp.minimum(g, SWIGLU_LIMIT)
            u = jnp.clip(u, -SWIGLU_LIMIT, SWIGLU_LIMIT)
            act = (u + 1.0) * (g * jax.nn.sigmoid(SWIGLU_ALPHA * g))
            acc = acc + _dot(act.astype(BF16), wdn_b[j * nc:(j + 1) * nc, :])
        _to_slot_major(y_ref, 0, acc, EXPERT_TILE)

    @pl.when(valid <= 0)
    def _():
        y_ref[...] = jnp.zeros_like(y_ref)


def _expert_mlp(x_buf, blk_e, blk_valid, layer, wgu, bgu, wdn, bdn):
    D = D_MODEL
    tm = EXPERT_TILE
    per_expert = lambda r, c: pl.BlockSpec((None, None, r, c), lambda i, be, bv: (layer, be[i], 0, 0))
    slots = pl.BlockSpec((tm * SUB, LANES), lambda i, be, bv: (i, 0))
    grid_spec = pltpu.PrefetchScalarGridSpec(
        num_scalar_prefetch=2,
        grid=(x_buf.shape[0] // (tm * SUB),),
        in_specs=[slots, per_expert(D, 2 * D_EXPERT), per_expert(1, 2 * D_EXPERT),
                  per_expert(D_EXPERT, D), per_expert(1, D)],
        out_specs=slots,
        scratch_shapes=[pltpu.VMEM((D, 2 * D_EXPERT), BF16), pltpu.VMEM((D_EXPERT, D), BF16)],
    )
    return pl.pallas_call(
        _expert_kernel,
        grid_spec=grid_spec,
        out_shape=jax.ShapeDtypeStruct(x_buf.shape, F32),
        compiler_params=_params(0, 1),
        name="moe_expert_mlp",
    )(blk_e, blk_valid, x_buf, wgu, bgu, wdn, bdn)


def _combine_kernel(pos_ref, x_ref, gate_ref, y_hbm, g_ref, b_ref, o_ref, *scratch, rows, n_tiles):
    n_slots = COMBINE_LAG + 1
    bufs, sem = scratch[:n_slots], scratch[n_slots]
    i = pl.program_id(0)
    sub = COMBINE_SUB

    def run(s_issue, s_done):
        def issue(j):
            for u in range(sub):
                r = j * sub + u
                for k in range(TOP_K):
                    _slot_copy(y_hbm, bufs[s_issue].at[k], sem.at[s_issue], pos_ref[k * rows + r], r).start(
                        priority=k % 2)

        def finish(j):
            rs = pl.ds(pl.multiple_of(j * sub, sub), sub)
            row0 = pl.multiple_of(j * (sub * SUB), sub * SUB)
            gates = gate_ref[rs, :]
            y = gates[:, 0:1] * _from_slot_major(bufs[s_done].at[0], row0, sub)
            for k in range(1, TOP_K):
                y = y + gates[:, k:k + 1] * _from_slot_major(bufs[s_done].at[k], row0, sub)
            o_ref[rs, :] = _ln_rows(DN_ALPHA * x_ref[rs, :] + y, g_ref[...], b_ref[...])

        def wait_done():
            for k in range(TOP_K):
                _slots_wait(y_hbm, bufs[s_done].at[k], sem.at[s_done], rows)

        def loop(do_issue, do_finish):
            def body(j, carry):
                if do_issue:
                    issue(j)
                if do_finish:
                    finish(j)
                return carry

            lax.fori_loop(0, rows // sub, body, 0)

        @pl.when(i < COMBINE_LAG)
        def _():
            loop(True, False)

        @pl.when(jnp.logical_and(i >= COMBINE_LAG, i < n_tiles))
        def _():
            wait_done()
            loop(True, True)

        @pl.when(i >= n_tiles)
        def _():
            wait_done()
            loop(False, True)

    for s in range(n_slots):
        pl.when(i % n_slots == s)(functools.partial(run, s, (s + 1) % n_slots))


def _combine(x, gates, pos_flat, y_buf, g, b):
    T, D = x.shape
    R = COMBINE_TILE
    n = T // R
    lag = COMBINE_LAG
    done = lambda i: jnp.maximum(i - lag, 0)
    return pl.pallas_call(
        functools.partial(_combine_kernel, rows=R, n_tiles=n),
        grid=(n + lag,),
        in_specs=[pl.BlockSpec((R * TOP_K,), lambda i: (jnp.minimum(i, n - 1),), memory_space=pltpu.SMEM),
                  pl.BlockSpec((R, D), lambda i: (done(i), 0)),
                  pl.BlockSpec((R, LANES), lambda i: (done(i), 0)),
                  pl.BlockSpec(memory_space=pl.ANY),
                  _full(g.shape), _full(b.shape)],
        out_specs=pl.BlockSpec((R, D), lambda i: (done(i), 0)),
        out_shape=jax.ShapeDtypeStruct((T, D), F32),
        scratch_shapes=[pltpu.VMEM((TOP_K, R * SUB, LANES), F32) for _ in range(lag + 1)]
                       + [pltpu.SemaphoreType.DMA((lag + 1,))],
        compiler_params=_params(0, 1),
        name="moe_combine_ln",
    )(pos_flat, x, gates, y_buf, g, b)


def _tile_major(pos, tile):
    K, T = pos.shape
    return pos.reshape(K, T // tile, tile).transpose(1, 0, 2).reshape(-1)


def _routing_plan(route, counts, n_tokens):
    tm = EXPERT_TILE
    e = route[:TOP_K]
    rank = route[TOP_K:]
    counts = counts[:, 0].astype(jnp.int32)
    padded = (counts + tm - 1) // tm * tm
    pend = jnp.cumsum(padded)
    pstart = pend - padded
    experts = jnp.arange(N_EXPERTS, dtype=jnp.int32)[:, None, None]
    base = jnp.sum(jnp.where(e[None] == experts, pstart[:, None, None], 0), axis=0)
    pos = (base + rank).astype(jnp.int32)
    n_slots = n_tokens * TOP_K + N_EXPERTS * tm
    blk_start = jnp.arange(n_slots // tm, dtype=jnp.int32) * tm
    blk_e = jnp.minimum(jnp.sum(blk_start[:, None] >= pend[None, :], axis=1), N_EXPERTS - 1).astype(jnp.int32)
    blk_valid = jnp.clip(pstart[blk_e] + counts[blk_e] - blk_start, 0, tm).astype(jnp.int32)
    return pos, blk_e, blk_valid, n_slots


def _moe(x1, route, gates, counts, layer, wgu, bgu, wdn, bdn, ln_g, ln_b):
    T, D = x1.shape
    pos, blk_e, blk_valid, n_slots = _routing_plan(route, counts, T)
    x_buf = _dispatch(x1, _tile_major(pos, DISPATCH_TILE), n_slots)
    y_buf = _expert_mlp(x_buf, blk_e, blk_valid, layer, wgu, bgu[:, :, None, :], wdn, bdn[:, :, None, :])
    return _combine(x1, gates, _tile_major(pos, COMBINE_TILE), y_buf, ln_g.reshape(1, D), ln_b.reshape(1, D))


def _trunk(x_parts, trunks, p):
    x = _embed_ln(x_parts[0], x_parts[1], p["emb_ln_g"], p["emb_ln_b"])
    T, D = x.shape
    mc = jnp.asarray(_fnet_channel_np(), BF16)
    for l in range(DEPTH):
        w_in = p["w_in"][l]
        wa = w_in[:, :SSD_WIDTH + SSD_XBC].astype(BF16)
        wdt = jnp.pad(w_in[:, SSD_WIDTH + SSD_XBC:SSD_IN], ((0, 0), (0, LANES - 2 * SSD_HEADS))).astype(BF16)
        wf = w_in[:, SSD_IN:SSD_IN + FNET_WIDTH].astype(BF16)
        wuv = w_in[:, SSD_IN + FNET_WIDTH:].astype(BF16)
        z, xbc, dt, fr, fi, uv = _inproj(x, wa, wdt, wf, wuv, mc)

        lane_pad = LANES - 2 * SSD_HEADS
        dtb = jnp.pad(p["dt_bias"][l].reshape(1, -1), ((0, 0), (0, lane_pad)))
        a_neg = jnp.pad(-jnp.exp(p["a_log"][l].astype(F32)).reshape(1, -1), ((0, 0), (0, lane_pad)))
        dskip_w = jnp.repeat(p["d_skip"][l], SSD_HEAD_DIM).reshape(1, SSD_WIDTH)
        normg = p["ssd_norm_g"][l].reshape(1, SSD_WIDTH)
        convw = p["conv_w"][l]
        convb = p["conv_b"][l].reshape(1, SSD_XBC)
        w_bd = jax.scipy.linalg.block_diag(*[p["fnet_w"][l, g] for g in range(FNET_GROUPS)]).astype(BF16)
        b_f = p["fnet_b"][l].reshape(1, FNET_WIDTH)

        ssd_parts, fn_parts = [], []
        row0 = 0
        for (B, L) in trunks:
            ssd_parts.append(_ssd_mixer(z, xbc, dt, row0, B, L, convw, convb, dtb, a_neg, dskip_w, normg))
            fn_parts.append(_fourier_mixer(fr, fi, row0, B, L, w_bd, b_f))
            row0 += B * L

        ws_cat = jnp.concatenate([p["gmlp_ws"][l, h] for h in range(GMLP_HEADS)], axis=1).astype(BF16)
        bs_wide = jnp.repeat(p["gmlp_bs"][l].T, GMLP_HDIM, axis=1)
        gm = _gmlp_mixer(uv, p["gmlp_ln_g"][l].reshape(1, -1), p["gmlp_ln_b"][l].reshape(1, -1),
                         ws_cat, bs_wide)

        w_r = jnp.pad(p["router_w"][l], ((0, 0), (0, LANES - N_EXPERTS)))
        wrh = w_r.astype(BF16)
        wrl = (w_r - wrh.astype(F32)).astype(BF16)
        b_r = jnp.pad(p["router_b"][l].reshape(1, -1), ((0, 0), (0, LANES - N_EXPERTS)),
                      constant_values=NEG_BIG)
        x1, idx, gates, counts = _outproj(
            x, ssd_parts, fn_parts, gm, p["w_out"][l].astype(BF16), p["b_out"][l].reshape(1, D),
            p["ln1_g"][l].reshape(1, D), p["ln1_b"][l].reshape(1, D), wrh, wrl, b_r)

        x = _moe(x1, idx, gates, counts, l, p["exp_w_gu"], p["exp_b_gu"], p["exp_w_dn"], p["exp_b_dn"],
                 p["ln2_g"][l], p["ln2_b"][l])
    return x


def kernel(x_prompt, x_sample, emb_ln_g, emb_ln_b, w_in, conv_w, conv_b, dt_bias, a_log, d_skip, ssd_norm_g, fnet_w, fnet_b, gmlp_ln_g, gmlp_ln_b, gmlp_ws, gmlp_bs, w_out, b_out, ln1_g, ln1_b, router_w, router_b, exp_w_gu, exp_b_gu, exp_w_dn, exp_b_dn, ln2_g, ln2_b):
    p = dict(emb_ln_g=emb_ln_g, emb_ln_b=emb_ln_b, w_in=w_in, conv_w=conv_w, conv_b=conv_b,
             dt_bias=dt_bias, a_log=a_log, d_skip=d_skip, ssd_norm_g=ssd_norm_g, fnet_w=fnet_w,
             fnet_b=fnet_b, gmlp_ln_g=gmlp_ln_g, gmlp_ln_b=gmlp_ln_b, gmlp_ws=gmlp_ws, gmlp_bs=gmlp_bs,
             w_out=w_out, b_out=b_out, ln1_g=ln1_g, ln1_b=ln1_b, router_w=router_w, router_b=router_b,
             exp_w_gu=exp_w_gu, exp_b_gu=exp_b_gu, exp_w_dn=exp_w_dn, exp_b_dn=exp_b_dn,
             ln2_g=ln2_g, ln2_b=ln2_b)
    D = x_prompt.shape[-1]
    trunks = [x_prompt.shape[:2], x_sample.shape[:2]]
    y = _trunk([x_prompt.reshape(-1, D), x_sample.reshape(-1, D)], trunks, p)
    n_p = x_prompt.shape[0] * x_prompt.shape[1]
    return (y[:n_p].reshape(x_prompt.shape), y[n_p:].reshape(x_sample.shape))
```
